```python
import jax, jax.numpy as jnp
from jax import lax
import numpy as np

D_MODEL = 1024
BATCH = 2
SEQ = 8192
DEPTH = 2

HEAD_DIM = 128
HEADS_PER_GROUP = 4
ATT_PATTERNS = ((128, 1), (512, 4), (2048, 16))
N_GROUPS = 3
N_ATT_HEADS = N_GROUPS * HEADS_PER_GROUP
ATT_QKV_WIDTH = N_ATT_HEADS * HEAD_DIM
ATT_OUT_WIDTH = HEADS_PER_GROUP * HEAD_DIM
ATT_BLOCK = 64
NEG_INF = -1e30
LRU_WIDTH = D_MODEL
LRU_BLOCKS = 4
LRU_BLOCK_WIDTH = LRU_WIDTH // LRU_BLOCKS
CONV_WIDTH = 4
CONV_LEFT = 2
RG_C = 8.0
NORM_EPS = 1e-6
IN_WIDTH = 3 * ATT_QKV_WIDTH + ATT_OUT_WIDTH + 2 * LRU_WIDTH + 2 * D_MODEL

kernel_name = "hybrid_dilated_attn_rglru_encoder"


def rms_norm(x, g):
    x32 = x.astype(jnp.float32)
    y = x32 * lax.rsqrt(jnp.mean(x32 * x32, axis=-1, keepdims=True) + NORM_EPS)
    return (y * g.astype(jnp.float32)).astype(x.dtype)


def alibi_slopes():
    return 2.0 ** (-8.0 * jnp.arange(1, N_ATT_HEADS + 1, dtype=jnp.float32) / N_ATT_HEADS)


def dilated_window_attention(q, k, v, window, dilation, slopes):
    b, s, h, dh = q.shape
    half = window // (2 * dilation)
    c = ATT_BLOCK
    n = s // dilation
    nblk = -(-n // c)
    pad = nblk * c - n

    def strided(t):
        return t.reshape(b, n, dilation, h, dh).transpose(0, 2, 1, 3, 4)

    qs = jnp.pad(strided(q), ((0, 0), (0, 0), (0, pad), (0, 0), (0, 0))).reshape(b, dilation, nblk, c, h, dh)

    def band(t):
        tp = jnp.pad(strided(t), ((0, 0), (0, 0), (c, pad + c), (0, 0), (0, 0))).reshape(b, dilation, nblk + 2, c, h, dh)
        return jnp.concatenate([tp[:, :, :-2], tp[:, :, 1:-1], tp[:, :, 2:]], axis=3)

    kw, vw = band(k), band(v)
    scores = jnp.einsum('brnqhd,brnkhd->brnhqk', qs, kw, preferred_element_type=jnp.float32) * (dh ** -0.5)
    qi = jnp.arange(c)
    kj = jnp.arange(3 * c) - c
    delta = kj[None, :] - qi[:, None]
    key_idx = jnp.arange(nblk)[:, None] * c + kj[None, :]
    valid = (jnp.abs(delta) <= half)[None] & ((key_idx >= 0) & (key_idx < n))[:, None, :]
    dist = (jnp.abs(delta) * dilation).astype(jnp.float32)
    bias = -slopes.astype(jnp.float32)[:, None, None] * dist[None]
    scores = jnp.where(valid[:, None], scores + bias, NEG_INF)
    m = jnp.max(scores, axis=-1, keepdims=True)
    p = jnp.exp(scores - m)
    den = jnp.sum(p, axis=-1, keepdims=True)
    out = jnp.einsum('brnhqk,brnkhd->brnqhd', p / den, vw.astype(jnp.float32))
    lse = (m + jnp.log(den))[..., 0].transpose(0, 1, 2, 4, 3)

    def unstride(t):
        t = t.reshape((b, dilation, nblk * c) + t.shape[4:])[:, :, :n]
        t = jnp.moveaxis(t, 1, 2)
        return t.reshape((b, s) + t.shape[3:])

    return unstride(out), unstride(lse)


def dilated_attention_mixer(q, k, v):
    b, s, _ = q.shape
    shp = (b, s, N_GROUPS, HEADS_PER_GROUP, HEAD_DIM)
    q, k, v = q.reshape(shp), k.reshape(shp), v.reshape(shp)
    slopes = alibi_slopes().reshape(N_GROUPS, HEADS_PER_GROUP)
    outs, lses = [], []
    for g, (window, dilation) in enumerate(ATT_PATTERNS):
        o, l = dilated_window_attention(q[:, :, g], k[:, :, g], v[:, :, g], window, dilation, slopes[g])
        outs.append(o)
        lses.append(l)
    outs = jnp.stack(outs, axis=0)
    alpha = jax.nn.softmax(jnp.stack(lses, axis=0), axis=0)
    mixed = jnp.sum(alpha[..., None] * outs, axis=0)
    return mixed.reshape(b, s, ATT_OUT_WIDTH).astype(q.dtype)


def centred_depthwise_conv(x, w, bias):
    s = x.shape[1]
    xp = jnp.pad(x, ((0, 0), (CONV_LEFT, CONV_WIDTH - 1 - CONV_LEFT), (0, 0)))
    y = bias + xp[:, 0:s] * w[0]
    for j in range(1, CONV_WIDTH):
        y = y + xp[:, j:j + s] * w[j]
    return y


def block_diag_linear(x, w, b):
    xr = x.reshape(x.shape[:-1] + (LRU_BLOCKS, LRU_BLOCK_WIDTH))
    return jnp.einsum('bsni,nij->bsnj', xr, w).reshape(x.shape) + b


def _linear_combine(left, right):
    a1, b1 = left
    a2, b2 = right
    return a1 * a2, a2 * b1 + b2


def rg_lru(x, w_a, b_a, w_x, b_x, lam, reverse):
    s = x.shape[1]
    r = jax.nn.sigmoid(block_diag_linear(x, w_a, b_a)).astype(jnp.float32)
    i = jax.nn.sigmoid(block_diag_linear(x, w_x, b_x))
    log_a = -RG_C * r * jax.nn.softplus(-lam.astype(jnp.float32))
    a = jnp.exp(log_a)
    mult = jnp.sqrt(-jnp.expm1(2.0 * log_a))
    first = (jnp.arange(s) == (s - 1 if reverse else 0))[None, :, None]
    mult = jnp.where(first, 1.0, mult)
    bterm = mult * (i * x).astype(jnp.float32)
    _, h = lax.associative_scan(_linear_combine, (a, bterm), axis=1, reverse=reverse)
    return h


def recurrent_mixer(x, conv_w, conv_b, w_a, b_a, w_x, b_x, lam):
    xc = centred_depthwise_conv(x, conv_w, conv_b)
    h = rg_lru(xc, w_a[0], b_a[0], w_x[0], b_x[0], lam[0], False) + \
        rg_lru(xc, w_a[1], b_a[1], w_x[1], b_x[1], lam[1], True)
    return h.astype(x.dtype)


def hybrid_layer(x, norm_g, w_in, b_gate, conv_w, conv_b, rg_w_a, rg_b_a, rg_w_x, rg_b_x, rg_lam,
                 w_o_att, w_o_lru, w_out):
    xn = rms_norm(x, norm_g)
    proj = xn @ w_in
    widths = [ATT_QKV_WIDTH, ATT_QKV_WIDTH, ATT_QKV_WIDTH, ATT_OUT_WIDTH, LRU_WIDTH, LRU_WIDTH]
    cuts = [int(c) for c in np.cumsum(widths)]
    q, k, v, g_att, x_lru, g_lru, gate_logits = jnp.split(proj, cuts, axis=-1)
    y_att = dilated_attention_mixer(q, k, v) * jax.nn.silu(g_att)
    y_lru = recurrent_mixer(x_lru, conv_w, conv_b, rg_w_a, rg_b_a, rg_w_x, rg_b_x, rg_lam) * jax.nn.silu(g_lru)
    gates = jax.nn.sigmoid(gate_logits + b_gate)
    gate_att, gate_lru = gates[..., :D_MODEL], gates[..., D_MODEL:]
    merged = gate_att * (y_att @ w_o_att) + gate_lru * (y_lru @ w_o_lru)
    return x + merged @ w_out


def setup_inputs(seed: int = 0) -> dict:
    key = jax.random.key(seed)
    ks = jax.random.split(key, 16)
    f32 = jnp.float32
    nrm = lambda k, shp, scale: jax.random.normal(k, shp, f32) * scale
    a_init = jax.random.uniform(ks[11], (DEPTH, 2, LRU_WIDTH), f32, 0.9, 0.999)
    p0 = a_init ** (1.0 / RG_C)
    return {
        "x": nrm(ks[0], (BATCH, SEQ, D_MODEL), 1.0),
        "norm_g": 1.0 + nrm(ks[1], (DEPTH, D_MODEL), 0.1),
        "w_in": nrm(ks[2], (DEPTH, D_MODEL, IN_WIDTH), D_MODEL ** -0.5),
        "b_gate": nrm(ks[3], (DEPTH, 2 * D_MODEL), 0.1),
        "conv_w": nrm(ks[4], (DEPTH, CONV_WIDTH, LRU_WIDTH), CONV_WIDTH ** -0.5),
        "conv_b": nrm(ks[5], (DEPTH, LRU_WIDTH), 0.05),
        "rg_w_a": nrm(ks[6], (DEPTH, 2, LRU_BLOCKS, LRU_BLOCK_WIDTH, LRU_BLOCK_WIDTH), LRU_BLOCK_WIDTH ** -0.5),
        "rg_b_a": nrm(ks[7], (DEPTH, 2, LRU_WIDTH), 0.1),
        "rg_w_x": nrm(ks[8], (DEPTH, 2, LRU_BLOCKS, LRU_BLOCK_WIDTH, LRU_BLOCK_WIDTH), LRU_BLOCK_WIDTH ** -0.5),
        "rg_b_x": nrm(ks[9], (DEPTH, 2, LRU_WIDTH), 0.1),
        "rg_lam": jnp.log(p0) - jnp.log1p(-p0),
        "w_o_att": nrm(ks[12], (DEPTH, ATT_OUT_WIDTH, D_MODEL), ATT_OUT_WIDTH ** -0.5),
        "w_o_lru": nrm(ks[13], (DEPTH, LRU_WIDTH, D_MODEL), LRU_WIDTH ** -0.5),
        "w_out": nrm(ks[14], (DEPTH, D_MODEL, D_MODEL), D_MODEL ** -0.5),
        "final_g": 1.0 + nrm(ks[15], (D_MODEL,), 0.1),
    }


def reference(x, norm_g, w_in, b_gate, conv_w, conv_b, rg_w_a, rg_b_a, rg_w_x, rg_b_x, rg_lam,
              w_o_att, w_o_lru, w_out, final_g):
    for l in range(DEPTH):
        x = hybrid_layer(x, norm_g[l], w_in[l], b_gate[l], conv_w[l], conv_b[l], rg_w_a[l], rg_b_a[l],
                         rg_w_x[l], rg_b_x[l], rg_lam[l], w_o_att[l], w_o_lru[l], w_out[l])
    return rms_norm(x, final_g)
```

```python
import functools

import jax
import jax.numpy as jnp
from jax import lax
from jax.experimental import pallas as pl
from jax.experimental.pallas import tpu as pltpu

F32 = jnp.float32
BF16 = jnp.bfloat16

HEAD_DIM = 128
HEADS_PER_GROUP = 4
ATT_PATTERNS = ((128, 1), (512, 4), (2048, 16))
N_GROUPS = len(ATT_PATTERNS)
N_ATT_HEADS = N_GROUPS * HEADS_PER_GROUP
GROUP_WIDTH = HEADS_PER_GROUP * HEAD_DIM
NEG_INF = -1e30
LRU_BLOCKS = 4
CONV_WIDTH = 4
CONV_LEFT = 2
RG_C = 8.0
NORM_EPS = 1e-6

Q_COL, K_COL, V_COL, GATT_COL = 0, 3, 6, 9
XLRU_COL, GLRU_COL, LOGIT_ATT_COL, LOGIT_LRU_COL = 5, 6, 7, 8

HALF_KEYS = 64
Q_SUB = 128
K_SUB = Q_SUB + 2 * HALF_KEYS
LSE_LANES = 128
LSE_REP = LSE_LANES // HEADS_PER_GROUP
BF16_ROWS = 16


def _sigmoid(x):
    return 1.0 / (1.0 + jnp.exp(-x))


def _norm_proj_body(x_ref, g_ref, w_ref, o_ref, xn_ref):
    @pl.when(pl.program_id(1) == 0)
    def _():
        x = x_ref[...]
        ms = jnp.mean(x * x, axis=-1, keepdims=True)
        xn_ref[...] = (x * lax.rsqrt(ms + NORM_EPS) * g_ref[...]).astype(BF16)

    o_ref[...] = jnp.dot(xn_ref[...], w_ref[...], preferred_element_type=F32).astype(BF16)


def _norm_proj(x2d, g, w):
    t, dm = x2d.shape
    n = w.shape[1]
    tm, tn = 1024, 1024
    return pl.pallas_call(
        _norm_proj_body,
        grid=(t // tm, n // tn),
        in_specs=[
            pl.BlockSpec((tm, dm), lambda i, j: (i, 0)),
            pl.BlockSpec((1, dm), lambda i, j: (0, 0)),
            pl.BlockSpec((dm, tn), lambda i, j: (0, j)),
        ],
        out_specs=pl.BlockSpec((tm, tn), lambda i, j: (i, j)),
        out_shape=jax.ShapeDtypeStruct((t, n), BF16),
        scratch_shapes=[pltpu.VMEM((tm, dm), BF16)],
        compiler_params=pltpu.CompilerParams(dimension_semantics=("parallel", "arbitrary")),
        name="norm_proj",
    )(x2d, g.reshape(1, dm), w)


def _attn_body(q_ref, kp_ref, kc_ref, kn_ref, vp_ref, vc_ref, vn_ref, o_ref, lse_ref,
               kcat, vcat, *, group, dilation, tq, n_sub):
    qi = pl.program_id(2)
    kcat[0:HALF_KEYS] = kp_ref[0]
    kcat[HALF_KEYS:HALF_KEYS + tq] = kc_ref[0]
    kcat[HALF_KEYS + tq:] = kn_ref[0]
    vcat[0:HALF_KEYS] = vp_ref[0]
    vcat[HALF_KEYS:HALF_KEYS + tq] = vc_ref[0]
    vcat[HALF_KEYS + tq:] = vn_ref[0]

    qrow = lax.broadcasted_iota(jnp.int32, (Q_SUB, K_SUB), 0)
    kcol = lax.broadcasted_iota(jnp.int32, (Q_SUB, K_SUB), 1)
    absd = jnp.abs(kcol - HALF_KEYS - qrow)
    band = absd <= HALF_KEYS
    dist = (absd * dilation).astype(F32)
    lane = lax.broadcasted_iota(jnp.int32, (Q_SUB, LSE_LANES), 1)
    scale = HEAD_DIM ** -0.5

    for sub in range(tq // Q_SUB):
        r0 = sub * Q_SUB
        kpos = qi * tq + r0 - HALF_KEYS + kcol
        valid = band & (kpos >= 0) & (kpos < n_sub)
        lse_tile = jnp.zeros((Q_SUB, LSE_LANES), F32)
        for h in range(HEADS_PER_GROUP):
            slope = 2.0 ** (-8.0 * (group * HEADS_PER_GROUP + h + 1) / N_ATT_HEADS)
            c0 = h * HEAD_DIM
            q = q_ref[0, r0:r0 + Q_SUB, c0:c0 + HEAD_DIM]
            k = kcat[r0:r0 + K_SUB, c0:c0 + HEAD_DIM]
            v = vcat[r0:r0 + K_SUB, c0:c0 + HEAD_DIM]
            s = lax.dot_general(q, k, (((1,), (1,)), ((), ())), preferred_element_type=F32) * scale
            s = jnp.where(valid, s - slope * dist, NEG_INF)
            m = jnp.max(s, axis=-1, keepdims=True)
            p = jnp.exp(s - m)
            den = jnp.sum(p, axis=-1, keepdims=True)
            o = jnp.dot(p.astype(BF16), v, preferred_element_type=F32) / den
            o_ref[0, r0:r0 + Q_SUB, c0:c0 + HEAD_DIM] = o.astype(BF16)
            lse = m + jnp.log(den)
            lse_tile = jnp.where((lane >= h * LSE_REP) & (lane < (h + 1) * LSE_REP), lse, lse_tile)
        lse_ref[0, r0:r0 + Q_SUB, :] = lse_tile


def _attention_group(proj, group, dilation, tq):
    b, s, n_cols = proj.shape
    n_sub = s // dilation
    cols = n_cols // GROUP_WIDTH
    view = proj.reshape(b, n_sub, dilation * n_cols)
    halo_per_tile = tq // HALF_KEYS
    last_halo = n_sub // HALF_KEYS - 1

    def main_spec(col):
        return pl.BlockSpec((1, tq, GROUP_WIDTH), lambda bi, r, qi: (bi, qi, r * cols + col + group))

    def prev_spec(col):
        return pl.BlockSpec((1, HALF_KEYS, GROUP_WIDTH),
                            lambda bi, r, qi: (bi, jnp.maximum(qi * halo_per_tile - 1, 0), r * cols + col + group))

    def next_spec(col):
        return pl.BlockSpec((1, HALF_KEYS, GROUP_WIDTH),
                            lambda bi, r, qi: (bi, jnp.minimum((qi + 1) * halo_per_tile, last_halo),
                                               r * cols + col + group))

    out, lse = pl.pallas_call(
        functools.partial(_attn_body, group=group, dilation=dilation, tq=tq, n_sub=n_sub),
        grid=(b, dilation, n_sub // tq),
        in_specs=[main_spec(Q_COL),
                  prev_spec(K_COL), main_spec(K_COL), next_spec(K_COL),
                  prev_spec(V_COL), main_spec(V_COL), next_spec(V_COL)],
        out_specs=[pl.BlockSpec((1, tq, GROUP_WIDTH), lambda bi, r, qi: (bi, qi, r)),
                   pl.BlockSpec((1, tq, LSE_LANES), lambda bi, r, qi: (bi, qi, r))],
        out_shape=[jax.ShapeDtypeStruct((b, n_sub, dilation * GROUP_WIDTH), BF16),
                   jax.ShapeDtypeStruct((b, n_sub, dilation * LSE_LANES), F32)],
        scratch_shapes=[pltpu.VMEM((tq + 2 * HALF_KEYS, GROUP_WIDTH), BF16),
                        pltpu.VMEM((tq + 2 * HALF_KEYS, GROUP_WIDTH), BF16)],
        compiler_params=pltpu.CompilerParams(dimension_semantics=("parallel", "parallel", "parallel")),
        name=f"attn_d{dilation}",
    )(view, view, view, view, view, view, view)
    return out.reshape(b, s, GROUP_WIDTH), lse.reshape(b, s, LSE_LANES)


def _lru_direction(xp_ref, xc_ref, xn_ref, h_ref, tile, n_tiles, rev, dirn,
                   cw_ref, cb_ref, wa_ref, ba_ref, wx_ref, bx_ref, lam_ref,
                   xcat, a_s, b_s, carry, *, tt, seq):
    width = xc_ref.shape[-1]
    bw = width // LRU_BLOCKS
    pad = BF16_ROWS
    xcat[0:pad] = jnp.where(tile == 0, 0.0, xp_ref[0].astype(F32))
    xcat[pad:pad + tt] = xc_ref[0].astype(F32)
    xcat[pad + tt:] = jnp.where(tile == n_tiles - 1, 0.0, xn_ref[0].astype(F32))

    xc = cb_ref[...] + xcat[pad - CONV_LEFT:pad - CONV_LEFT + tt] * cw_ref[0:1]
    for j in range(1, CONV_WIDTH):
        xc = xc + xcat[pad - CONV_LEFT + j:pad - CONV_LEFT + j + tt] * cw_ref[j:j + 1]
    xcb = xc.astype(BF16)

    lam = lam_ref[dirn:dirn + 1]
    neg_lam = -lam
    softplus = jnp.maximum(neg_lam, 0.0) + jnp.log1p(jnp.exp(-jnp.abs(neg_lam)))
    coef = -RG_C * softplus

    row = lax.broadcasted_iota(jnp.int32, (tt, 1), 0)
    first = (tile * tt + row) == (seq - 1 if rev else 0)

    for blk in range(LRU_BLOCKS):
        c0 = blk * bw
        xblk = xcb[:, c0:c0 + bw]
        pa = jnp.dot(xblk, wa_ref[dirn, blk], preferred_element_type=F32) + ba_ref[dirn:dirn + 1, c0:c0 + bw]
        px = jnp.dot(xblk, wx_ref[dirn, blk], preferred_element_type=F32) + bx_ref[dirn:dirn + 1, c0:c0 + bw]
        log_a = coef[:, c0:c0 + bw] * _sigmoid(pa)
        a = jnp.exp(log_a)
        mult = jnp.sqrt(-jnp.tanh(log_a) * (a * a + 1.0))
        mult = jnp.where(first, 1.0, mult)
        a_s[:, c0:c0 + bw] = a
        b_s[:, c0:c0 + bw] = mult * (_sigmoid(px) * xc[:, c0:c0 + bw])

    lanes = 128
    rows = lax.broadcasted_iota(jnp.int32, (tt, lanes), 0)
    for cc in range(width // lanes):
        c0 = cc * lanes
        a = a_s[:, c0:c0 + lanes]
        b = b_s[:, c0:c0 + lanes]
        shift = 1
        while shift < tt:
            if rev:
                a_sh = pltpu.roll(a, tt - shift, 0)
                b_sh = pltpu.roll(b, tt - shift, 0)
                keep = rows < tt - shift
            else:
                a_sh = pltpu.roll(a, shift, 0)
                b_sh = pltpu.roll(b, shift, 0)
                keep = rows >= shift
            b = jnp.where(keep, a * b_sh, 0.0) + b
            a = jnp.where(keep, a * a_sh, a)
            shift *= 2
        h = a * carry[dirn:dirn + 1, c0:c0 + lanes] + b
        h_ref[0, :, c0:c0 + lanes] = h.astype(h_ref.dtype)
        last = 0 if rev else tt - 1
        carry[dirn:dirn + 1, c0:c0 + lanes] = h[last:last + 1]


def _lru_body(xfp_ref, xfc_ref, xfn_ref, xbp_ref, xbc_ref, xbn_ref,
              cw_ref, cb_ref, wa_ref, ba_ref, wx_ref, bx_ref, lam_ref,
              hf_ref, hb_ref, xcat, a_s, b_s, carry, *, tt, seq):
    i = pl.program_id(1)
    n_tiles = pl.num_programs(1)

    @pl.when(i == 0)
    def _():
        carry[...] = jnp.zeros_like(carry)

    params = (cw_ref, cb_ref, wa_ref, ba_ref, wx_ref, bx_ref, lam_ref, xcat, a_s, b_s, carry)
    _lru_direction(xfp_ref, xfc_ref, xfn_ref, hf_ref, i, n_tiles, False, 0, *params, tt=tt, seq=seq)
    _lru_direction(xbp_ref, xbc_ref, xbn_ref, hb_ref, n_tiles - 1 - i, n_tiles, True, 1, *params, tt=tt, seq=seq)


def _lru(proj, conv_w, conv_b, w_a, b_a, w_x, b_x, lam, tt):
    b, s, _ = proj.shape
    width = conv_w.shape[-1]
    n_tiles = s // tt
    halo_per_tile = tt // BF16_ROWS
    last_halo = s // BF16_ROWS - 1

    def tile_of(i, rev):
        return n_tiles - 1 - i if rev else i

    def specs(rev):
        return [
            pl.BlockSpec((1, BF16_ROWS, width),
                         lambda bi, i: (bi, jnp.maximum(tile_of(i, rev) * halo_per_tile - 1, 0), XLRU_COL)),
            pl.BlockSpec((1, tt, width), lambda bi, i: (bi, tile_of(i, rev), XLRU_COL)),
            pl.BlockSpec((1, BF16_ROWS, width),
                         lambda bi, i: (bi, jnp.minimum((tile_of(i, rev) + 1) * halo_per_tile, last_halo), XLRU_COL)),
        ]

    def whole(a):
        return pl.BlockSpec(a.shape, lambda bi, i: (0,) * a.ndim)

    consts = (conv_w, conv_b.reshape(1, width), w_a, b_a, w_x, b_x, lam)
    return pl.pallas_call(
        functools.partial(_lru_body, tt=tt, seq=s),
        grid=(b, n_tiles),
        in_specs=specs(False) + specs(True) + [whole(a) for a in consts],
        out_specs=[pl.BlockSpec((1, tt, width), lambda bi, i: (bi, i, 0)),
                   pl.BlockSpec((1, tt, width), lambda bi, i: (bi, n_tiles - 1 - i, 0))],
        out_shape=[jax.ShapeDtypeStruct((b, s, width), BF16)] * 2,
        scratch_shapes=[pltpu.VMEM((tt + 2 * BF16_ROWS, width), F32),
                        pltpu.VMEM((tt, width), F32),
                        pltpu.VMEM((tt, width), F32),
                        pltpu.VMEM((8, width), F32)],
        compiler_params=pltpu.CompilerParams(dimension_semantics=("parallel", "arbitrary")),
        name="lru",
    )(proj, proj, proj, proj, proj, proj, *consts)


def _merge_body(o0_ref, o1_ref, o2_ref, l0_ref, l1_ref, l2_ref, gatt_ref, hf_ref, hb_ref, glru_ref,
                za_ref, zl_ref, x_ref, bg_ref, woa_ref, wol_ref, wout_ref, fg_ref, out_ref, *, final):
    tm = x_ref.shape[0]
    dm = x_ref.shape[1]
    l0, l1, l2 = l0_ref[...], l1_ref[...], l2_ref[...]
    mx = jnp.maximum(jnp.maximum(l0, l1), l2)
    e0, e1, e2 = jnp.exp(l0 - mx), jnp.exp(l1 - mx), jnp.exp(l2 - mx)
    inv = 1.0 / (e0 + e1 + e2)
    alphas = (e0 * inv, e1 * inv, e2 * inv)
    o_refs = (o0_ref, o1_ref, o2_ref)

    heads = []
    for h in range(HEADS_PER_GROUP):
        c0 = h * HEAD_DIM
        acc = jnp.zeros((tm, HEAD_DIM), F32)
        for g in range(N_GROUPS):
            w = jnp.broadcast_to(alphas[g][:, h * LSE_REP:h * LSE_REP + 1], (tm, HEAD_DIM))
            acc = acc + w * o_refs[g][:, c0:c0 + HEAD_DIM].astype(F32)
        heads.append(acc)
    mixed = jnp.concatenate(heads, axis=-1)

    gatt = gatt_ref[...].astype(F32)
    y_att = (mixed * (gatt * _sigmoid(gatt))).astype(BF16)
    glru = glru_ref[...].astype(F32)
    h_sum = hf_ref[...].astype(F32) + hb_ref[...].astype(F32)
    y_lru = (h_sum * (glru * _sigmoid(glru))).astype(BF16)

    p_att = jnp.dot(y_att, woa_ref[...], preferred_element_type=F32)
    p_lru = jnp.dot(y_lru, wol_ref[...], preferred_element_type=F32)
    gate_att = _sigmoid(za_ref[...].astype(F32) + bg_ref[:, 0:dm])
    gate_lru = _sigmoid(zl_ref[...].astype(F32) + bg_ref[:, dm:2 * dm])
    merged = (gate_att * p_att + gate_lru * p_lru).astype(BF16)
    y = x_ref[...] + jnp.dot(merged, wout_ref[...], preferred_element_type=F32)
    if final:
        ms = jnp.mean(y * y, axis=-1, keepdims=True)
        y = y * lax.rsqrt(ms + NORM_EPS) * fg_ref[...]
    out_ref[...] = y


def _merge(x2d, proj2d, outs, lses, h_f, h_b, b_gate, w_o_att, w_o_lru, w_out, final_g, final, tm):
    t, dm = x2d.shape

    def rows(width, col=0):
        return pl.BlockSpec((tm, width), lambda i: (i, col))

    def whole(a):
        return pl.BlockSpec(a.shape, lambda i: (0,) * a.ndim)

    consts = (b_gate.reshape(1, 2 * dm), w_o_att, w_o_lru, w_out, final_g.reshape(1, dm))
    return pl.pallas_call(
        functools.partial(_merge_body, final=final),
        grid=(t // tm,),
        in_specs=[rows(GROUP_WIDTH)] * 3 + [rows(LSE_LANES)] * 3
                 + [rows(GROUP_WIDTH, GATT_COL), rows(dm), rows(dm), rows(dm, GLRU_COL),
                    rows(dm, LOGIT_ATT_COL), rows(dm, LOGIT_LRU_COL), rows(dm)]
                 + [whole(a) for a in consts],
        out_specs=rows(dm),
        out_shape=jax.ShapeDtypeStruct((t, dm), F32),
        compiler_params=pltpu.CompilerParams(dimension_semantics=("parallel",)),
        name="merge_final" if final else "merge",
    )(*outs, *lses, proj2d, h_f, h_b, proj2d, proj2d, proj2d, x2d, *consts)


def kernel(x, norm_g, w_in, b_gate, conv_w, conv_b, rg_w_a, rg_b_a, rg_w_x, rg_b_x, rg_lam,
           w_o_att, w_o_lru, w_out, final_g):
    b, s, dm = x.shape
    depth = w_in.shape[0]
    t = b * s
    x2d = x.reshape(t, dm)
    for l in range(depth):
        proj2d = _norm_proj(x2d, norm_g[l], w_in[l].astype(BF16))
        proj = proj2d.reshape(b, s, -1)
        outs, lses = [], []
        for g, (_, dilation) in enumerate(ATT_PATTERNS):
            o, lse = _attention_group(proj, g, dilation, tq=min(256, s // dilation))
            outs.append(o.reshape(t, GROUP_WIDTH))
            lses.append(lse.reshape(t, LSE_LANES))
        h_f, h_b = _lru(proj, conv_w[l], conv_b[l], rg_w_a[l].astype(BF16), rg_b_a[l],
                        rg_w_x[l].astype(BF16), rg_b_x[l], rg_lam[l], tt=256)
        x2d = _merge(x2d, proj2d, outs, lses, h_f.reshape(t, dm), h_b.reshape(t, dm), b_gate[l],
                     w_o_att[l].astype(BF16), w_o_lru[l].astype(BF16), w_out[l].astype(BF16),
                     final_g, final=(l == depth - 1), tm=512)
    return x2d.reshape(b, s, dm)
```

```python
import functools

import jax
import jax.numpy as jnp
from jax import lax
from jax.experimental import pallas as pl
from jax.experimental.pallas import tpu as pltpu

F32 = jnp.float32
BF16 = jnp.bfloat16

HEAD_DIM = 128
HEADS_PER_GROUP = 4
ATT_PATTERNS = ((128, 1), (512, 4), (2048, 16))
N_GROUPS = len(ATT_PATTERNS)
N_ATT_HEADS = N_GROUPS * HEADS_PER_GROUP
GROUP_WIDTH = HEADS_PER_GROUP * HEAD_DIM
QKV_WIDTH = 3 * GROUP_WIDTH
NEG_INF = -1e30
LRU_BLOCKS = 4
CONV_WIDTH = 4
CONV_LEFT = 2
RG_C = 8.0
NORM_EPS = 1e-6

Q_COL, K_COL, V_COL, GATT_COL = 0, 1, 2, 3
XLRU_COL, GLRU_COL, LOGIT_ATT_COL, LOGIT_LRU_COL = 2, 3, 4, 5

LANES = 128
HALF_KEYS = 64
Q_SUB = 128
K_SUB = Q_SUB + 2 * HALF_KEYS
LSE_REP = LANES // HEADS_PER_GROUP
BF16_ROWS = 16


def _sigmoid(x):
    return 1.0 / (1.0 + jnp.exp(-x))


def _norm_proj_body(x_ref, g_ref, w_ref, nat_ref, s4_ref, s16_ref, xn_ref, slab_ref, *, j_nat, j_s4):
    j = pl.program_id(1)

    @pl.when(j == 0)
    def _():
        x = x_ref[...]
        ms = jnp.mean(x * x, axis=-1, keepdims=True)
        xn_ref[...] = (x * lax.rsqrt(ms + NORM_EPS) * g_ref[...]).astype(BF16)

    acc = jnp.dot(xn_ref[...], w_ref[...], preferred_element_type=F32)
    tm, tn = acc.shape

    @pl.when(j < j_nat)
    def _():
        nat_ref[...] = acc.astype(BF16)

    def deinterleave(out_ref, d):
        for s in range(tn // LANES):
            slab_ref[s] = acc[:, s * LANES:(s + 1) * LANES]
        for s in range(tn // LANES):
            for r in range(d):
                out_ref[0, r, :, s * LANES:(s + 1) * LANES] = (
                    slab_ref[s, pl.ds(r, tm // d, stride=d), :].astype(BF16))

    @pl.when((j >= j_nat) & (j < j_nat + j_s4))
    def _():
        deinterleave(s4_ref, 4)

    @pl.when(j >= j_nat + j_s4)
    def _():
        deinterleave(s16_ref, 16)


def _norm_proj(x2d, g, w, batch):
    t, dm = x2d.shape
    s = t // batch
    n_nat = w.shape[1] - 2 * QKV_WIDTH
    tm, tn = 1024, 768
    j_nat, j_s = n_nat // tn, QKV_WIDTH // tn
    tiles_per_b = s // tm

    def strided_spec(d, first):
        return pl.BlockSpec((1, d, tm // d, tn),
                            lambda i, j: (i // tiles_per_b, 0, i % tiles_per_b, jnp.clip(j - first, 0, j_s - 1)))

    return pl.pallas_call(
        functools.partial(_norm_proj_body, j_nat=j_nat, j_s4=j_s),
        grid=(t // tm, j_nat + 2 * j_s),
        in_specs=[
            pl.BlockSpec((tm, dm), lambda i, j: (i, 0)),
            pl.BlockSpec((1, dm), lambda i, j: (0, 0)),
            pl.BlockSpec((dm, tn), lambda i, j: (0, j)),
        ],
        out_specs=[pl.BlockSpec((tm, tn), lambda i, j: (i, jnp.minimum(j, j_nat - 1))),
                   strided_spec(4, j_nat), strided_spec(16, j_nat + j_s)],
        out_shape=[jax.ShapeDtypeStruct((t, n_nat), BF16),
                   jax.ShapeDtypeStruct((batch, 4, s // 4, QKV_WIDTH), BF16),
                   jax.ShapeDtypeStruct((batch, 16, s // 16, QKV_WIDTH), BF16)],
        scratch_shapes=[pltpu.VMEM((tm, dm), BF16), pltpu.VMEM((tn // LANES, tm, LANES), F32)],
        compiler_params=pltpu.CompilerParams(dimension_semantics=("parallel", "arbitrary")),
        name="norm_proj",
    )(x2d, g.reshape(1, dm), w)


def _attn_body(q_ref, kp_ref, kc_ref, kn_ref, vp_ref, vc_ref, vn_ref, o_ref, lse_ref,
               kcat, vcat, *o_slab, group, dilation, tq, n_sub):
    qi = pl.program_id(1)
    r = pl.program_id(2)
    blk = (0,) if dilation == 1 else (0, 0)
    kcat[0:HALF_KEYS] = kp_ref[blk]
    kcat[HALF_KEYS:HALF_KEYS + tq] = kc_ref[blk]
    kcat[HALF_KEYS + tq:] = kn_ref[blk]
    vcat[0:HALF_KEYS] = vp_ref[blk]
    vcat[HALF_KEYS:HALF_KEYS + tq] = vc_ref[blk]
    vcat[HALF_KEYS + tq:] = vn_ref[blk]

    qrow = lax.broadcasted_iota(jnp.int32, (Q_SUB, K_SUB), 0)
    kcol = lax.broadcasted_iota(jnp.int32, (Q_SUB, K_SUB), 1)
    absd = jnp.abs(kcol - HALF_KEYS - qrow)
    band = absd <= HALF_KEYS
    dist = (absd * dilation).astype(F32)
    lane = lax.broadcasted_iota(jnp.int32, (Q_SUB, LANES), 1)
    scale = HEAD_DIM ** -0.5

    for sub in range(tq // Q_SUB):
        r0 = sub * Q_SUB
        kpos = qi * tq + r0 - HALF_KEYS + kcol
        valid = band & (kpos >= 0) & (kpos < n_sub)
        lse_tile = jnp.zeros((Q_SUB, LANES), F32)
        for h in range(HEADS_PER_GROUP):
            slope = 2.0 ** (-8.0 * (group * HEADS_PER_GROUP + h + 1) / N_ATT_HEADS)
            c0 = h * HEAD_DIM
            q = q_ref[blk + (slice(r0, r0 + Q_SUB), slice(c0, c0 + HEAD_DIM))]
            k = kcat[r0:r0 + K_SUB, c0:c0 + HEAD_DIM]
            v = vcat[r0:r0 + K_SUB, c0:c0 + HEAD_DIM]
            s = lax.dot_general(q, k, (((1,), (1,)), ((), ())), preferred_element_type=F32) * scale
            s = jnp.where(valid, s - slope * dist, NEG_INF)
            m = jnp.max(s, axis=-1, keepdims=True)
            p = jnp.exp(s - m)
            den = jnp.sum(p, axis=-1, keepdims=True)
            o = jnp.dot(p.astype(BF16), v, preferred_element_type=F32) / den
            if dilation == 1:
                o_ref[0, r0:r0 + Q_SUB, c0:c0 + HEAD_DIM] = o.astype(BF16)
            else:
                o_slab[0][h, pl.ds(r0 * dilation + r, Q_SUB, stride=dilation), :] = o
            lse = m + jnp.log(den)
            lse_tile = jnp.where((lane >= h * LSE_REP) & (lane < (h + 1) * LSE_REP), lse, lse_tile)
        if dilation == 1:
            lse_ref[0, r0:r0 + Q_SUB, :] = lse_tile
        else:
            lse_ref[0, pl.ds(r0 * dilation + r, Q_SUB, stride=dilation), :] = lse_tile

    if dilation > 1:
        @pl.when(r == dilation - 1)
        def _():
            for h in range(HEADS_PER_GROUP):
                o_ref[0, :, h * HEAD_DIM:(h + 1) * HEAD_DIM] = o_slab[0][h].astype(BF16)


def _attention_group(qkv, group, dilation, tq):
    if dilation == 1:
        b, s, _ = qkv.shape
    else:
        b, _, n_sub, _ = qkv.shape
        s = n_sub * dilation
    n_sub = s // dilation
    halo_per_tile = tq // HALF_KEYS
    last_halo = n_sub // HALF_KEYS - 1

    def spec(rows, row_index, col):
        if dilation == 1:
            return pl.BlockSpec((1, rows, GROUP_WIDTH), lambda bi, qi, r: (bi, row_index(qi), col))
        return pl.BlockSpec((1, 1, rows, GROUP_WIDTH), lambda bi, qi, r: (bi, r, row_index(qi), col))

    def main_spec(col):
        return spec(tq, lambda qi: qi, col)

    def prev_spec(col):
        return spec(HALF_KEYS, lambda qi: jnp.maximum(qi * halo_per_tile - 1, 0), col)

    def next_spec(col):
        return spec(HALF_KEYS, lambda qi: jnp.minimum((qi + 1) * halo_per_tile, last_halo), col)

    scratch = [pltpu.VMEM((tq + 2 * HALF_KEYS, GROUP_WIDTH), BF16),
               pltpu.VMEM((tq + 2 * HALF_KEYS, GROUP_WIDTH), BF16)]
    if dilation > 1:
        scratch.append(pltpu.VMEM((HEADS_PER_GROUP, tq * dilation, LANES), F32))

    return pl.pallas_call(
        functools.partial(_attn_body, group=group, dilation=dilation, tq=tq, n_sub=n_sub),
        grid=(b, n_sub // tq, dilation),
        in_specs=[main_spec(Q_COL),
                  prev_spec(K_COL), main_spec(K_COL), next_spec(K_COL),
                  prev_spec(V_COL), main_spec(V_COL), next_spec(V_COL)],
        out_specs=[pl.BlockSpec((1, tq * dilation, GROUP_WIDTH), lambda bi, qi, r: (bi, qi, 0)),
                   pl.BlockSpec((1, tq * dilation, LANES), lambda bi, qi, r: (bi, qi, 0))],
        out_shape=[jax.ShapeDtypeStruct((b, s, GROUP_WIDTH), BF16),
                   jax.ShapeDtypeStruct((b, s, LANES), F32)],
        scratch_shapes=scratch,
        compiler_params=pltpu.CompilerParams(dimension_semantics=("parallel", "parallel", "arbitrary")),
        name=f"attn_d{dilation}",
    )(qkv, qkv, qkv, qkv, qkv, qkv, qkv)


def _lru_direction(xp_ref, xc_ref, xn_ref, h_ref, tile, n_tiles, rev, dirn,
                   cw_ref, cb_ref, wa_ref, ba_ref, wx_ref, bx_ref, lam_ref,
                   xcat, a_s, b_s, carry, *, tt, seq):
    width = xc_ref.shape[-1]
    bw = width // LRU_BLOCKS
    pad = BF16_ROWS
    xcat[0:pad] = jnp.where(tile == 0, 0.0, xp_ref[0].astype(F32))
    xcat[pad:pad + tt] = xc_ref[0].astype(F32)
    xcat[pad + tt:] = jnp.where(tile == n_tiles - 1, 0.0, xn_ref[0].astype(F32))

    xc = cb_ref[...] + xcat[pad - CONV_LEFT:pad - CONV_LEFT + tt] * cw_ref[0:1]
    for j in range(1, CONV_WIDTH):
        xc = xc + xcat[pad - CONV_LEFT + j:pad - CONV_LEFT + j + tt] * cw_ref[j:j + 1]
    xcb = xc.astype(BF16)

    lam = lam_ref[dirn:dirn + 1]
    neg_lam = -lam
    softplus = jnp.maximum(neg_lam, 0.0) + jnp.log1p(jnp.exp(-jnp.abs(neg_lam)))
    coef = -RG_C * softplus

    row = lax.broadcasted_iota(jnp.int32, (tt, 1), 0)
    first = (tile * tt + row) == (seq - 1 if rev else 0)

    for blk in range(LRU_BLOCKS):
        c0 = blk * bw
        xblk = xcb[:, c0:c0 + bw]
        pa = jnp.dot(xblk, wa_ref[dirn, blk], preferred_element_type=F32) + ba_ref[dirn:dirn + 1, c0:c0 + bw]
        px = jnp.dot(xblk, wx_ref[dirn, blk], preferred_element_type=F32) + bx_ref[dirn:dirn + 1, c0:c0 + bw]
        log_a = coef[:, c0:c0 + bw] * _sigmoid(pa)
        a = jnp.exp(log_a)
        mult = jnp.sqrt(-jnp.tanh(log_a) * (a * a + 1.0))
        mult = jnp.where(first, 1.0, mult)
        a_s[:, c0:c0 + bw] = a
        b_s[:, c0:c0 + bw] = mult * (_sigmoid(px) * xc[:, c0:c0 + bw])

    rows = lax.broadcasted_iota(jnp.int32, (tt, LANES), 0)
    for cc in range(width // LANES):
        c0 = cc * LANES
        a = a_s[:, c0:c0 + LANES]
        b = b_s[:, c0:c0 + LANES]
        shift = 1
        while shift < tt:
            if rev:
                a_sh = pltpu.roll(a, tt - shift, 0)
                b_sh = pltpu.roll(b, tt - shift, 0)
                keep = rows < tt - shift
            else:
                a_sh = pltpu.roll(a, shift, 0)
                b_sh = pltpu.roll(b, shift, 0)
                keep = rows >= shift
            b = jnp.where(keep, a * b_sh, 0.0) + b
            a = jnp.where(keep, a * a_sh, a)
            shift *= 2
        h = a * carry[dirn:dirn + 1, c0:c0 + LANES] + b
        h_ref[0, :, c0:c0 + LANES] = h.astype(h_ref.dtype)
        last = 0 if rev else tt - 1
        carry[dirn:dirn + 1, c0:c0 + LANES] = h[last:last + 1]


def _lru_body(xfp_ref, xfc_ref, xfn_ref, xbp_ref, xbc_ref, xbn_ref,
              cw_ref, cb_ref, wa_ref, ba_ref, wx_ref, bx_ref, lam_ref,
              hf_ref, hb_ref, xcat, a_s, b_s, carry, *, tt, seq):
    i = pl.program_id(1)
    n_tiles = pl.num_programs(1)

    @pl.when(i == 0)
    def _():
        carry[...] = jnp.zeros_like(carry)

    params = (cw_ref, cb_ref, wa_ref, ba_ref, wx_ref, bx_ref, lam_ref, xcat, a_s, b_s, carry)
    _lru_direction(xfp_ref, xfc_ref, xfn_ref, hf_ref, i, n_tiles, False, 0, *params, tt=tt, seq=seq)
    _lru_direction(xbp_ref, xbc_ref, xbn_ref, hb_ref, n_tiles - 1 - i, n_tiles, True, 1, *params, tt=tt, seq=seq)


def _lru(nat, conv_w, conv_b, w_a, b_a, w_x, b_x, lam, tt):
    b, s, _ = nat.shape
    width = conv_w.shape[-1]
    n_tiles = s // tt
    halo_per_tile = tt // BF16_ROWS
    last_halo = s // BF16_ROWS - 1

    def tile_of(i, rev):
        return n_tiles - 1 - i if rev else i

    def specs(rev):
        return [
            pl.BlockSpec((1, BF16_ROWS, width),
                         lambda bi, i: (bi, jnp.maximum(tile_of(i, rev) * halo_per_tile - 1, 0), XLRU_COL)),
            pl.BlockSpec((1, tt, width), lambda bi, i: (bi, tile_of(i, rev), XLRU_COL)),
            pl.BlockSpec((1, BF16_ROWS, width),
                         lambda bi, i: (bi, jnp.minimum((tile_of(i, rev) + 1) * halo_per_tile, last_halo), XLRU_COL)),
        ]

    def whole(a):
        return pl.BlockSpec(a.shape, lambda bi, i: (0,) * a.ndim)

    consts = (conv_w, conv_b.reshape(1, width), w_a, b_a, w_x, b_x, lam)
    return pl.pallas_call(
        functools.partial(_lru_body, tt=tt, seq=s),
        grid=(b, n_tiles),
        in_specs=specs(False) + specs(True) + [whole(a) for a in consts],
        out_specs=[pl.BlockSpec((1, tt, width), lambda bi, i: (bi, i, 0)),
                   pl.BlockSpec((1, tt, width), lambda bi, i: (bi, n_tiles - 1 - i, 0))],
        out_shape=[jax.ShapeDtypeStruct((b, s, width), BF16)] * 2,
        scratch_shapes=[pltpu.VMEM((tt + 2 * BF16_ROWS, width), F32),
                        pltpu.VMEM((tt, width), F32),
                        pltpu.VMEM((tt, width), F32),
                        pltpu.VMEM((8, width), F32)],
        compiler_params=pltpu.CompilerParams(dimension_semantics=("parallel", "arbitrary")),
        name="lru",
    )(nat, nat, nat, nat, nat, nat, *consts)


def _merge_body(o0_ref, o1_ref, o2_ref, l0_ref, l1_ref, l2_ref, gatt_ref, hf_ref, hb_ref, glru_ref,
                za_ref, zl_ref, x_ref, bg_ref, woa_ref, wol_ref, wout_ref, fg_ref, out_ref, *, final):
    tm = x_ref.shape[0]
    dm = x_ref.shape[1]
    l0, l1, l2 = l0_ref[...], l1_ref[...], l2_ref[...]
    mx = jnp.maximum(jnp.maximum(l0, l1), l2)
    e0, e1, e2 = jnp.exp(l0 - mx), jnp.exp(l1 - mx), jnp.exp(l2 - mx)
    inv = 1.0 / (e0 + e1 + e2)
    alphas = (e0 * inv, e1 * inv, e2 * inv)
    o_refs = (o0_ref, o1_ref, o2_ref)

    heads = []
    for h in range(HEADS_PER_GROUP):
        c0 = h * HEAD_DIM
        acc = jnp.zeros((tm, HEAD_DIM), F32)
        for g in range(N_GROUPS):
            w = jnp.broadcast_to(alphas[g][:, h * LSE_REP:h * LSE_REP + 1], (tm, HEAD_DIM))
            acc = acc + w * o_refs[g][:, c0:c0 + HEAD_DIM].astype(F32)
        heads.append(acc)
    mixed = jnp.concatenate(heads, axis=-1)

    gatt = gatt_ref[...].astype(F32)
    y_att = (mixed * (gatt * _sigmoid(gatt))).astype(BF16)
    glru = glru_ref[...].astype(F32)
    h_sum = hf_ref[...].astype(F32) + hb_ref[...].astype(F32)
    y_lru = (h_sum * (glru * _sigmoid(glru))).astype(BF16)

    p_att = jnp.dot(y_att, woa_ref[...], preferred_element_type=F32)
    p_lru = jnp.dot(y_lru, wol_ref[...], preferred_element_type=F32)
    gate_att = _sigmoid(za_ref[...].astype(F32) + bg_ref[:, 0:dm])
    gate_lru = _sigmoid(zl_ref[...].astype(F32) + bg_ref[:, dm:2 * dm])
    merged = (gate_att * p_att + gate_lru * p_lru).astype(BF16)
    y = x_ref[...] + jnp.dot(merged, wout_ref[...], preferred_element_type=F32)
    if final:
        ms = jnp.mean(y * y, axis=-1, keepdims=True)
        y = y * lax.rsqrt(ms + NORM_EPS) * fg_ref[...]
    out_ref[...] = y


def _merge(x2d, nat2d, outs, lses, h_f, h_b, b_gate, w_o_att, w_o_lru, w_out, final_g, final, tm):
    t, dm = x2d.shape

    def rows(width, col=0):
        return pl.BlockSpec((tm, width), lambda i: (i, col))

    def whole(a):
        return pl.BlockSpec(a.shape, lambda i: (0,) * a.ndim)

    consts = (b_gate.reshape(1, 2 * dm), w_o_att, w_o_lru, w_out, final_g.reshape(1, dm))
    return pl.pallas_call(
        functools.partial(_merge_body, final=final),
        grid=(t // tm,),
        in_specs=[rows(GROUP_WIDTH)] * 3 + [rows(LANES)] * 3
                 + [rows(GROUP_WIDTH, GATT_COL), rows(dm), rows(dm), rows(dm, GLRU_COL),
                    rows(dm, LOGIT_ATT_COL), rows(dm, LOGIT_LRU_COL), rows(dm)]
                 + [whole(a) for a in consts],
        out_specs=rows(dm),
        out_shape=jax.ShapeDtypeStruct((t, dm), F32),
        compiler_params=pltpu.CompilerParams(dimension_semantics=("parallel",)),
        name="merge_final" if final else "merge",
    )(*outs, *lses, nat2d, h_f, h_b, nat2d, nat2d, nat2d, x2d, *consts)


def _split_projection_weight(w):
    gw = GROUP_WIDTH
    qkv = [w[:, i * gw:(i + 1) * gw] for i in range(3 * N_GROUPS)]
    rest = w[:, 3 * N_GROUPS * gw:]
    ordered = [qkv[0], qkv[3], qkv[6], rest] + [qkv[g + 3 * i] for g in (1, 2) for i in range(3)]
    return jnp.concatenate(ordered, axis=1).astype(BF16)


def kernel(x, norm_g, w_in, b_gate, conv_w, conv_b, rg_w_a, rg_b_a, rg_w_x, rg_b_x, rg_lam,
           w_o_att, w_o_lru, w_out, final_g):
    b, s, dm = x.shape
    depth = w_in.shape[0]
    t = b * s
    x2d = x.reshape(t, dm)
    for l in range(depth):
        nat2d, s4, s16 = _norm_proj(x2d, norm_g[l], _split_projection_weight(w_in[l]), b)
        nat = nat2d.reshape(b, s, -1)
        outs, lses = [], []
        for g, (qkv, tq) in enumerate(((nat, 256), (s4, 256), (s16, 128))):
            o, lse = _attention_group(qkv, g, ATT_PATTERNS[g][1], tq=tq)
            outs.append(o.reshape(t, GROUP_WIDTH))
            lses.append(lse.reshape(t, LANES))
        h_f, h_b = _lru(nat, conv_w[l], conv_b[l], rg_w_a[l].astype(BF16), rg_b_a[l],
                        rg_w_x[l].astype(BF16), rg_b_x[l], rg_lam[l], tt=256)
        x2d = _merge(x2d, nat2d, outs, lses, h_f.reshape(t, dm), h_b.reshape(t, dm), b_gate[l],
                     w_o_att[l].astype(BF16), w_o_lru[l].astype(BF16), w_out[l].astype(BF16),
                     final_g, final=(l == depth - 1), tm=512)
    return x2d.reshape(b, s, dm)
```

```python
import functools

import jax
import jax.numpy as jnp
from jax import lax
from jax.experimental import pallas as pl
from jax.experimental.pallas import tpu as pltpu

F32 = jnp.float32
BF16 = jnp.bfloat16

HEAD_DIM = 128
HEADS_PER_GROUP = 4
ATT_PATTERNS = ((128, 1), (512, 4), (2048, 16))
N_GROUPS = len(ATT_PATTERNS)
N_ATT_HEADS = N_GROUPS * HEADS_PER_GROUP
GROUP_WIDTH = HEADS_PER_GROUP * HEAD_DIM
QKV_WIDTH = 3 * GROUP_WIDTH
NEG_INF = -1e30
LRU_BLOCKS = 4
CONV_WIDTH = 4
CONV_LEFT = 2
RG_C = 8.0
NORM_EPS = 1e-6

Q_COL, K_COL, V_COL, GATT_COL = 0, 1, 2, 3
XLRU_COL, GLRU_COL, LOGIT_ATT_COL, LOGIT_LRU_COL = 2, 3, 4, 5

LANES = 128
HALF_KEYS = 64
Q_SUB = 128
K_SUB = Q_SUB + 2 * HALF_KEYS
LSE_REP = LANES // HEADS_PER_GROUP
BF16_ROWS = 16
SEGMENTS = 8


NEG_LOG2_E = -1.4426950408889634


def _sigmoid(x):
    return 1.0 / (1.0 + jnp.exp2(x * NEG_LOG2_E))


def _norm_proj_body(x_ref, g_ref, w_ref, nat_ref, s4_ref, s16_ref, xn_ref, xn4_ref, xn16_ref, slab_ref,
                    *, j_nat, j_s4):
    j = pl.program_id(1)
    tm, dm = x_ref.shape
    tn = w_ref.shape[1]

    @pl.when(j == 0)
    def _():
        x = x_ref[...]
        ms = jnp.mean(x * x, axis=-1, keepdims=True)
        y = x * lax.rsqrt(ms + NORM_EPS) * g_ref[...]
        xn_ref[...] = y.astype(BF16)
        for c in range(dm // LANES):
            slab_ref[c] = y[:, c * LANES:(c + 1) * LANES]
        for ref, d in ((xn4_ref, 4), (xn16_ref, 16)):
            rows = tm // d
            for c in range(dm // LANES):
                for r in range(d):
                    ref[r * rows:(r + 1) * rows, c * LANES:(c + 1) * LANES] = (
                        slab_ref[c, pl.ds(r, rows, stride=d), :].astype(BF16))

    @pl.when(j < j_nat)
    def _():
        nat_ref[...] = jnp.dot(xn_ref[...], w_ref[...], preferred_element_type=F32).astype(BF16)

    @pl.when((j >= j_nat) & (j < j_nat + j_s4))
    def _():
        acc = jnp.dot(xn4_ref[...], w_ref[...], preferred_element_type=F32)
        s4_ref[0] = acc.astype(BF16).reshape(4, tm // 4, tn)

    @pl.when(j >= j_nat + j_s4)
    def _():
        acc = jnp.dot(xn16_ref[...], w_ref[...], preferred_element_type=F32)
        s16_ref[0] = acc.astype(BF16).reshape(16, tm // 16, tn)


def _norm_proj(x2d, g, w, batch):
    t, dm = x2d.shape
    s = t // batch
    n_nat = w.shape[1] - 2 * QKV_WIDTH
    tm, tn = 1024, 768
    j_nat, j_s = n_nat // tn, QKV_WIDTH // tn
    tiles_per_b = s // tm

    def strided_spec(d, first):
        return pl.BlockSpec((1, d, tm // d, tn),
                            lambda i, j: (i // tiles_per_b, 0, i % tiles_per_b, jnp.clip(j - first, 0, j_s - 1)))

    return pl.pallas_call(
        functools.partial(_norm_proj_body, j_nat=j_nat, j_s4=j_s),
        grid=(t // tm, j_nat + 2 * j_s),
        in_specs=[
            pl.BlockSpec((tm, dm), lambda i, j: (i, 0)),
            pl.BlockSpec((1, dm), lambda i, j: (0, 0)),
            pl.BlockSpec((dm, tn), lambda i, j: (0, j)),
        ],
        out_specs=[pl.BlockSpec((tm, tn), lambda i, j: (i, jnp.minimum(j, j_nat - 1))),
                   strided_spec(4, j_nat), strided_spec(16, j_nat + j_s)],
        out_shape=[jax.ShapeDtypeStruct((t, n_nat), BF16),
                   jax.ShapeDtypeStruct((batch, 4, s // 4, QKV_WIDTH), BF16),
                   jax.ShapeDtypeStruct((batch, 16, s // 16, QKV_WIDTH), BF16)],
        scratch_shapes=[pltpu.VMEM((tm, dm), BF16), pltpu.VMEM((tm, dm), BF16), pltpu.VMEM((tm, dm), BF16),
                        pltpu.VMEM((dm // LANES, tm, LANES), F32)],
        compiler_params=pltpu.CompilerParams(dimension_semantics=("parallel", "arbitrary")),
        name="norm_proj",
    )(x2d, g.reshape(1, dm), w)


def _attn_body(q_ref, kp_ref, kc_ref, kn_ref, vp_ref, vc_ref, vn_ref, o_ref, lse_ref,
               kcat, vcat, *o_slab, group, dilation, tq, n_sub):
    qi = pl.program_id(1)
    r = pl.program_id(2)
    blk = (0,) if dilation == 1 else (0, 0)
    kcat[0:HALF_KEYS] = kp_ref[blk]
    kcat[HALF_KEYS:HALF_KEYS + tq] = kc_ref[blk]
    kcat[HALF_KEYS + tq:] = kn_ref[blk]
    vcat[0:HALF_KEYS] = vp_ref[blk]
    vcat[HALF_KEYS:HALF_KEYS + tq] = vc_ref[blk]
    vcat[HALF_KEYS + tq:] = vn_ref[blk]

    qrow = lax.broadcasted_iota(jnp.int32, (Q_SUB, K_SUB), 0)
    kcol = lax.broadcasted_iota(jnp.int32, (Q_SUB, K_SUB), 1)
    absd = jnp.abs(kcol - HALF_KEYS - qrow)
    band = absd <= HALF_KEYS
    dist = (absd * dilation).astype(F32)
    lane = lax.broadcasted_iota(jnp.int32, (Q_SUB, LANES), 1)
    scale = HEAD_DIM ** -0.5

    for sub in range(tq // Q_SUB):
        r0 = sub * Q_SUB
        kpos = qi * tq + r0 - HALF_KEYS + kcol
        valid = band & (kpos >= 0) & (kpos < n_sub)
        lse_tile = jnp.zeros((Q_SUB, LANES), F32)
        for h in range(HEADS_PER_GROUP):
            slope = 2.0 ** (-8.0 * (group * HEADS_PER_GROUP + h + 1) / N_ATT_HEADS)
            c0 = h * HEAD_DIM
            q = q_ref[blk + (slice(r0, r0 + Q_SUB), slice(c0, c0 + HEAD_DIM))]
            k = kcat[r0:r0 + K_SUB, c0:c0 + HEAD_DIM]
            v = vcat[r0:r0 + K_SUB, c0:c0 + HEAD_DIM]
            s = lax.dot_general(q, k, (((1,), (1,)), ((), ())), preferred_element_type=F32) * scale
            s = jnp.where(valid, s - slope * dist, NEG_INF)
            m = jnp.max(s, axis=-1, keepdims=True)
            p = jnp.exp(s - m)
            den = jnp.sum(p, axis=-1, keepdims=True)
            o = jnp.dot(p.astype(BF16), v, preferred_element_type=F32) / den
            if dilation == 1:
                o_ref[0, r0:r0 + Q_SUB, c0:c0 + HEAD_DIM] = o.astype(BF16)
            else:
                o_slab[0][h, pl.ds(r0 * dilation + r, Q_SUB, stride=dilation), :] = o
            lse = m + jnp.log(den)
            lse_tile = jnp.where((lane >= h * LSE_REP) & (lane < (h + 1) * LSE_REP), lse, lse_tile)
        if dilation == 1:
            lse_ref[0, r0:r0 + Q_SUB, :] = lse_tile
        else:
            lse_ref[0, pl.ds(r0 * dilation + r, Q_SUB, stride=dilation), :] = lse_tile

    if dilation > 1:
        @pl.when(r == dilation - 1)
        def _():
            for h in range(HEADS_PER_GROUP):
                o_ref[0, :, h * HEAD_DIM:(h + 1) * HEAD_DIM] = o_slab[0][h].astype(BF16)


def _attention_group(qkv, group, dilation, tq):
    if dilation == 1:
        b, s, _ = qkv.shape
    else:
        b, _, n_sub, _ = qkv.shape
        s = n_sub * dilation
    n_sub = s // dilation
    halo_per_tile = tq // HALF_KEYS
    last_halo = n_sub // HALF_KEYS - 1

    def spec(rows, row_index, col):
        if dilation == 1:
            return pl.BlockSpec((1, rows, GROUP_WIDTH), lambda bi, qi, r: (bi, row_index(qi), col))
        return pl.BlockSpec((1, 1, rows, GROUP_WIDTH), lambda bi, qi, r: (bi, r, row_index(qi), col))

    def main_spec(col):
        return spec(tq, lambda qi: qi, col)

    def prev_spec(col):
        return spec(HALF_KEYS, lambda qi: jnp.maximum(qi * halo_per_tile - 1, 0), col)

    def next_spec(col):
        return spec(HALF_KEYS, lambda qi: jnp.minimum((qi + 1) * halo_per_tile, last_halo), col)

    scratch = [pltpu.VMEM((tq + 2 * HALF_KEYS, GROUP_WIDTH), BF16),
               pltpu.VMEM((tq + 2 * HALF_KEYS, GROUP_WIDTH), BF16)]
    if dilation > 1:
        scratch.append(pltpu.VMEM((HEADS_PER_GROUP, tq * dilation, LANES), F32))

    return pl.pallas_call(
        functools.partial(_attn_body, group=group, dilation=dilation, tq=tq, n_sub=n_sub),
        grid=(b, n_sub // tq, dilation),
        in_specs=[main_spec(Q_COL),
                  prev_spec(K_COL), main_spec(K_COL), next_spec(K_COL),
                  prev_spec(V_COL), main_spec(V_COL), next_spec(V_COL)],
        out_specs=[pl.BlockSpec((1, tq * dilation, GROUP_WIDTH), lambda bi, qi, r: (bi, qi, 0)),
                   pl.BlockSpec((1, tq * dilation, LANES), lambda bi, qi, r: (bi, qi, 0))],
        out_shape=[jax.ShapeDtypeStruct((b, s, GROUP_WIDTH), BF16),
                   jax.ShapeDtypeStruct((b, s, LANES), F32)],
        scratch_shapes=scratch,
        compiler_params=pltpu.CompilerParams(dimension_semantics=("parallel", "parallel", "arbitrary")),
        name=f"attn_d{dilation}",
    )(qkv, qkv, qkv, qkv, qkv, qkv, qkv)


def _lru_direction(xp_ref, xc_ref, xn_ref, h_ref, tile, n_tiles, rev, dirn,
                   cw_ref, cb_ref, wa_ref, ba_ref, wx_ref, bx_ref, lam_ref,
                   xpad, a_s, b_s, hpad, carry, *, tt):
    width = xc_ref.shape[-1]
    bw = width // LRU_BLOCKS
    n_chunks = width // LANES
    seg_len = tt // SEGMENTS
    pitch = seg_len + SEGMENTS
    sub = lax.broadcasted_iota(jnp.int32, (SEGMENTS, width), 0)

    xf = xc_ref[0].astype(F32)
    for c in range(n_chunks):
        for k in range(SEGMENTS):
            xpad[c, k * pitch:k * pitch + seg_len, :] = xf[k * seg_len:(k + 1) * seg_len, c * LANES:(c + 1) * LANES]
    xs = jnp.concatenate(
        [jnp.concatenate([xpad[c, pl.ds(j, SEGMENTS, stride=pitch), :] for j in range(seg_len)], axis=0)
         for c in range(n_chunks)], axis=1)

    halo_prev = jnp.where(tile == 0, 0.0, xp_ref[0, BF16_ROWS - CONV_LEFT:BF16_ROWS, :].astype(F32))
    halo_next = jnp.where(tile == n_tiles - 1, 0.0, xn_ref[0, 0:1, :].astype(F32))
    before1 = jnp.where(sub == 0, halo_prev[1:2], pltpu.roll(xs[tt - SEGMENTS:tt], 1, 0))
    before2 = jnp.where(sub == 0, halo_prev[0:1], pltpu.roll(xs[tt - 2 * SEGMENTS:tt - SEGMENTS], 1, 0))
    after1 = jnp.where(sub == SEGMENTS - 1, halo_next, pltpu.roll(xs[0:SEGMENTS], SEGMENTS - 1, 0))
    xm2 = jnp.concatenate([before2, before1, xs[:tt - 2 * SEGMENTS]], axis=0)
    xm1 = jnp.concatenate([before1, xs[:tt - SEGMENTS]], axis=0)
    xp1 = jnp.concatenate([xs[SEGMENTS:], after1], axis=0)
    xc = cb_ref[...] + xm2 * cw_ref[0:1] + xm1 * cw_ref[1:2] + xs * cw_ref[2:3] + xp1 * cw_ref[3:4]
    xcb = xc.astype(BF16)

    lam = lam_ref[dirn:dirn + 1]
    neg_lam = -lam
    softplus = jnp.maximum(neg_lam, 0.0) + jnp.log1p(jnp.exp(-jnp.abs(neg_lam)))
    coef = -RG_C * softplus

    sub_blk = lax.broadcasted_iota(jnp.int32, (SEGMENTS, bw), 0)
    if rev:
        start_rows = slice(tt - SEGMENTS, tt)
        first = (tile == n_tiles - 1) & (sub_blk == SEGMENTS - 1)
    else:
        start_rows = slice(0, SEGMENTS)
        first = (tile == 0) & (sub_blk == 0)

    for blk in range(LRU_BLOCKS):
        c0 = blk * bw
        xblk = xcb[:, c0:c0 + bw]
        pa = jnp.dot(xblk, wa_ref[dirn, blk], preferred_element_type=F32) + ba_ref[dirn:dirn + 1, c0:c0 + bw]
        px = jnp.dot(xblk, wx_ref[dirn, blk], preferred_element_type=F32) + bx_ref[dirn:dirn + 1, c0:c0 + bw]
        log_a = coef[:, c0:c0 + bw] * _sigmoid(pa)
        a = jnp.exp(log_a)
        u = jnp.tanh(log_a) * (-1.0 - a * a)
        mult = jnp.where(u > 0.0, u * lax.rsqrt(u), 0.0)
        gated = _sigmoid(px) * xc[:, c0:c0 + bw]
        b = mult * gated
        a_s[:, c0:c0 + bw] = a
        b_s[:, c0:c0 + bw] = b
        b_s[start_rows, c0:c0 + bw] = jnp.where(first, gated[start_rows], b[start_rows])

    steps = range(seg_len - 1, -1, -1) if rev else range(seg_len)
    segs = range(SEGMENTS - 1, -1, -1) if rev else range(SEGMENTS)
    sub1 = lax.broadcasted_iota(jnp.int32, (SEGMENTS, LANES), 0)
    for c in range(n_chunks):
        cs = slice(c * LANES, (c + 1) * LANES)
        prod = jnp.ones((SEGMENTS, LANES), F32)
        h = jnp.zeros((SEGMENTS, LANES), F32)
        for j in steps:
            aj = a_s[j * SEGMENTS:(j + 1) * SEGMENTS, cs]
            h = aj * h + b_s[j * SEGMENTS:(j + 1) * SEGMENTS, cs]
            prod = aj * prod
        state = carry[dirn:dirn + 1, cs]
        h0 = jnp.zeros((SEGMENTS, LANES), F32)
        for k in segs:
            h0 = jnp.where(sub1 == k, state, h0)
            state = prod[k:k + 1] * state + h[k:k + 1]
        carry[dirn:dirn + 1, cs] = state
        h = h0
        for j in steps:
            h = a_s[j * SEGMENTS:(j + 1) * SEGMENTS, cs] * h + b_s[j * SEGMENTS:(j + 1) * SEGMENTS, cs]
            hpad[c, pl.ds(j, SEGMENTS, stride=pitch), :] = h
        for k in range(SEGMENTS):
            h_ref[0, k * seg_len:(k + 1) * seg_len, cs] = hpad[c, k * pitch:k * pitch + seg_len, :].astype(h_ref.dtype)


def _lru_body(xfp_ref, xfc_ref, xfn_ref, xbp_ref, xbc_ref, xbn_ref,
              cw_ref, cb_ref, wa_ref, ba_ref, wx_ref, bx_ref, lam_ref,
              hf_ref, hb_ref, xpad_f, a_f, b_f, hpad_f, xpad_b, a_b, b_b, hpad_b, carry, *, tt):
    i = pl.program_id(1)
    n_tiles = pl.num_programs(1)

    @pl.when(i == 0)
    def _():
        carry[...] = jnp.zeros_like(carry)

    params = (cw_ref, cb_ref, wa_ref, ba_ref, wx_ref, bx_ref, lam_ref)
    _lru_direction(xfp_ref, xfc_ref, xfn_ref, hf_ref, i, n_tiles, False, 0, *params,
                   xpad_f, a_f, b_f, hpad_f, carry, tt=tt)
    _lru_direction(xbp_ref, xbc_ref, xbn_ref, hb_ref, n_tiles - 1 - i, n_tiles, True, 1, *params,
                   xpad_b, a_b, b_b, hpad_b, carry, tt=tt)


def _lru(nat, conv_w, conv_b, w_a, b_a, w_x, b_x, lam, tt):
    b, s, _ = nat.shape
    width = conv_w.shape[-1]
    n_tiles = s // tt
    halo_per_tile = tt // BF16_ROWS
    last_halo = s // BF16_ROWS - 1

    def tile_of(i, rev):
        return n_tiles - 1 - i if rev else i

    def specs(rev):
        return [
            pl.BlockSpec((1, BF16_ROWS, width),
                         lambda bi, i: (bi, jnp.maximum(tile_of(i, rev) * halo_per_tile - 1, 0), XLRU_COL)),
            pl.BlockSpec((1, tt, width), lambda bi, i: (bi, tile_of(i, rev), XLRU_COL)),
            pl.BlockSpec((1, BF16_ROWS, width),
                         lambda bi, i: (bi, jnp.minimum((tile_of(i, rev) + 1) * halo_per_tile, last_halo), XLRU_COL)),
        ]

    def whole(a):
        return pl.BlockSpec(a.shape, lambda bi, i: (0,) * a.ndim)

    consts = (conv_w, conv_b.reshape(1, width), w_a, b_a, w_x, b_x, lam)
    padded_rows = tt + SEGMENTS * SEGMENTS
    per_direction = [pltpu.VMEM((width // LANES, padded_rows, LANES), F32),
                     pltpu.VMEM((tt, width), F32),
                     pltpu.VMEM((tt, width), F32),
                     pltpu.VMEM((width // LANES, padded_rows, LANES), F32)]
    return pl.pallas_call(
        functools.partial(_lru_body, tt=tt),
        grid=(b, n_tiles),
        in_specs=specs(False) + specs(True) + [whole(a) for a in consts],
        out_specs=[pl.BlockSpec((1, tt, width), lambda bi, i: (bi, i, 0)),
                   pl.BlockSpec((1, tt, width), lambda bi, i: (bi, n_tiles - 1 - i, 0))],
        out_shape=[jax.ShapeDtypeStruct((b, s, width), BF16)] * 2,
        scratch_shapes=per_direction + per_direction + [pltpu.VMEM((8, width), F32)],
        compiler_params=pltpu.CompilerParams(dimension_semantics=("parallel", "arbitrary")),
        name="lru",
    )(nat, nat, nat, nat, nat, nat, *consts)


def _merge_body(o0_ref, o1_ref, o2_ref, l0_ref, l1_ref, l2_ref, gatt_ref, hf_ref, hb_ref, glru_ref,
                za_ref, zl_ref, x_ref, bg_ref, woa_ref, wol_ref, wout_ref, fg_ref, out_ref, *, final):
    tm = x_ref.shape[0]
    dm = x_ref.shape[1]
    l0, l1, l2 = l0_ref[...], l1_ref[...], l2_ref[...]
    mx = jnp.maximum(jnp.maximum(l0, l1), l2)
    e0, e1, e2 = jnp.exp(l0 - mx), jnp.exp(l1 - mx), jnp.exp(l2 - mx)
    inv = 1.0 / (e0 + e1 + e2)
    alphas = (e0 * inv, e1 * inv, e2 * inv)
    o_refs = (o0_ref, o1_ref, o2_ref)

    heads = []
    for h in range(HEADS_PER_GROUP):
        c0 = h * HEAD_DIM
        acc = jnp.zeros((tm, HEAD_DIM), F32)
        for g in range(N_GROUPS):
            w = jnp.broadcast_to(alphas[g][:, h * LSE_REP:h * LSE_REP + 1], (tm, HEAD_DIM))
            acc = acc + w * o_refs[g][:, c0:c0 + HEAD_DIM].astype(F32)
        heads.append(acc)
    mixed = jnp.concatenate(heads, axis=-1)

    gatt = gatt_ref[...].astype(F32)
    y_att = (mixed * (gatt * _sigmoid(gatt))).astype(BF16)
    glru = glru_ref[...].astype(F32)
    h_sum = hf_ref[...].astype(F32) + hb_ref[...].astype(F32)
    y_lru = (h_sum * (glru * _sigmoid(glru))).astype(BF16)

    p_att = jnp.dot(y_att, woa_ref[...], preferred_element_type=F32)
    p_lru = jnp.dot(y_lru, wol_ref[...], preferred_element_type=F32)
    gate_att = _sigmoid(za_ref[...].astype(F32) + bg_ref[:, 0:dm])
    gate_lru = _sigmoid(zl_ref[...].astype(F32) + bg_ref[:, dm:2 * dm])
    merged = (gate_att * p_att + gate_lru * p_lru).astype(BF16)
    y = x_ref[...] + jnp.dot(merged, wout_ref[...], preferred_element_type=F32)
    if final:
        ms = jnp.mean(y * y, axis=-1, keepdims=True)
        y = y * lax.rsqrt(ms + NORM_EPS) * fg_ref[...]
    out_ref[...] = y


def _merge(x2d, nat2d, outs, lses, h_f, h_b, b_gate, w_o_att, w_o_lru, w_out, final_g, final, tm):
    t, dm = x2d.shape

    def rows(width, col=0):
        return pl.BlockSpec((tm, width), lambda i: (i, col))

    def whole(a):
        return pl.BlockSpec(a.shape, lambda i: (0,) * a.ndim)

    consts = (b_gate.reshape(1, 2 * dm), w_o_att, w_o_lru, w_out, final_g.reshape(1, dm))
    return pl.pallas_call(
        functools.partial(_merge_body, final=final),
        grid=(t // tm,),
        in_specs=[rows(GROUP_WIDTH)] * 3 + [rows(LANES)] * 3
                 + [rows(GROUP_WIDTH, GATT_COL), rows(dm), rows(dm), rows(dm, GLRU_COL),
                    rows(dm, LOGIT_ATT_COL), rows(dm, LOGIT_LRU_COL), rows(dm)]
                 + [whole(a) for a in consts],
        out_specs=rows(dm),
        out_shape=jax.ShapeDtypeStruct((t, dm), F32),
        compiler_params=pltpu.CompilerParams(dimension_semantics=("parallel",)),
        name="merge_final" if final else "merge",
    )(*outs, *lses, nat2d, h_f, h_b, nat2d, nat2d, nat2d, x2d, *consts)


def _split_projection_weight(w):
    gw = GROUP_WIDTH
    qkv = [w[:, i * gw:(i + 1) * gw] for i in range(3 * N_GROUPS)]
    rest = w[:, 3 * N_GROUPS * gw:]
    ordered = [qkv[0], qkv[3], qkv[6], rest] + [qkv[g + 3 * i] for g in (1, 2) for i in range(3)]
    return jnp.concatenate(ordered, axis=1).astype(BF16)


def kernel(x, norm_g, w_in, b_gate, conv_w, conv_b, rg_w_a, rg_b_a, rg_w_x, rg_b_x, rg_lam,
           w_o_att, w_o_lru, w_out, final_g):
    b, s, dm = x.shape
    depth = w_in.shape[0]
    t = b * s
    x2d = x.reshape(t, dm)
    for l in range(depth):
        nat2d, s4, s16 = _norm_proj(x2d, norm_g[l], _split_projection_weight(w_in[l]), b)
        nat = nat2d.reshape(b, s, -1)
        outs, lses = [], []
        for g, (qkv, tq) in enumerate(((nat, 256), (s4, 256), (s16, 128))):
            o, lse = _attention_group(qkv, g, ATT_PATTERNS[g][1], tq=tq)
            outs.append(o.reshape(t, GROUP_WIDTH))
            lses.append(lse.reshape(t, LANES))
        h_f, h_b = _lru(nat, conv_w[l], conv_b[l], rg_w_a[l].astype(BF16), rg_b_a[l],
                        rg_w_x[l].astype(BF16), rg_b_x[l], rg_lam[l], tt=256)
        x2d = _merge(x2d, nat2d, outs, lses, h_f.reshape(t, dm), h_b.reshape(t, dm), b_gate[l],
                     w_o_att[l].astype(BF16), w_o_lru[l].astype(BF16), w_out[l].astype(BF16),
                     final_g, final=(l == depth - 1), tm=512)
    return x2d.reshape(b, s, dm)
```

```python
import functools

import jax
import jax.numpy as jnp
from jax import lax
from jax.experimental import pallas as pl
from jax.experimental.pallas import tpu as pltpu

F32 = jnp.float32
BF16 = jnp.bfloat16

HEAD_DIM = 128
HEADS_PER_GROUP = 4
ATT_PATTERNS = ((128, 1), (512, 4), (2048, 16))
N_GROUPS = len(ATT_PATTERNS)
N_ATT_HEADS = N_GROUPS * HEADS_PER_GROUP
GROUP_WIDTH = HEADS_PER_GROUP * HEAD_DIM
QKV_WIDTH = 3 * GROUP_WIDTH
NEG_INF = -1e30
LRU_BLOCKS = 4
CONV_WIDTH = 4
CONV_LEFT = 2
RG_C = 8.0
NORM_EPS = 1e-6

Q_COL, K_COL, V_COL, GATT_COL = 0, 1, 2, 3
XLRU_COL, GLRU_COL, LOGIT_ATT_COL, LOGIT_LRU_COL = 2, 3, 4, 5

LANES = 128
HALF_KEYS = 64
Q_SUB = 128
K_SUB = Q_SUB + 2 * HALF_KEYS
LSE_REP = LANES // HEADS_PER_GROUP
BF16_ROWS = 16
SEGMENTS = 8


NEG_LOG2_E = -1.4426950408889634


def _sigmoid(x):
    return 1.0 / (1.0 + jnp.exp2(x * NEG_LOG2_E))


def _norm_proj_body(x_ref, g_ref, w_ref, nat_ref, s4_ref, s16_ref, xn_ref, xn4_ref, xn16_ref, slab_ref, *, tn):
    tm, dm = x_ref.shape
    n_nat = nat_ref.shape[1]

    x = x_ref[...]
    ms = jnp.mean(x * x, axis=-1, keepdims=True)
    y = x * lax.rsqrt(ms + NORM_EPS) * g_ref[...]
    xn_ref[...] = y.astype(BF16)
    for c in range(dm // LANES):
        slab_ref[c] = y[:, c * LANES:(c + 1) * LANES]
    for ref, d in ((xn4_ref, 4), (xn16_ref, 16)):
        rows = tm // d
        for c in range(dm // LANES):
            for r in range(d):
                ref[r * rows:(r + 1) * rows, c * LANES:(c + 1) * LANES] = (
                    slab_ref[c, pl.ds(r, rows, stride=d), :].astype(BF16))

    def project(lhs_ref, col):
        return jnp.dot(lhs_ref[...], w_ref[:, col:col + tn], preferred_element_type=F32).astype(BF16)

    for n0 in range(0, n_nat, tn):
        nat_ref[:, n0:n0 + tn] = project(xn_ref, n0)
    for n0 in range(0, QKV_WIDTH, tn):
        s4_ref[0, :, :, n0:n0 + tn] = project(xn4_ref, n_nat + n0).reshape(4, tm // 4, tn)
        s16_ref[0, :, :, n0:n0 + tn] = project(xn16_ref, n_nat + QKV_WIDTH + n0).reshape(16, tm // 16, tn)


def _norm_proj(x2d, g, w, batch):
    t, dm = x2d.shape
    s = t // batch
    n_nat = w.shape[1] - 2 * QKV_WIDTH
    tm, tn = 512, 768
    tiles_per_b = s // tm

    def strided_spec(d):
        return pl.BlockSpec((1, d, tm // d, QKV_WIDTH), lambda i: (i // tiles_per_b, 0, i % tiles_per_b, 0))

    return pl.pallas_call(
        functools.partial(_norm_proj_body, tn=tn),
        grid=(t // tm,),
        in_specs=[
            pl.BlockSpec((tm, dm), lambda i: (i, 0)),
            pl.BlockSpec((1, dm), lambda i: (0, 0)),
            pl.BlockSpec(w.shape, lambda i: (0, 0), pipeline_mode=pl.Buffered(1)),
        ],
        out_specs=[pl.BlockSpec((tm, n_nat), lambda i: (i, 0)), strided_spec(4), strided_spec(16)],
        out_shape=[jax.ShapeDtypeStruct((t, n_nat), BF16),
                   jax.ShapeDtypeStruct((batch, 4, s // 4, QKV_WIDTH), BF16),
                   jax.ShapeDtypeStruct((batch, 16, s // 16, QKV_WIDTH), BF16)],
        scratch_shapes=[pltpu.VMEM((tm, dm), BF16), pltpu.VMEM((tm, dm), BF16), pltpu.VMEM((tm, dm), BF16),
                        pltpu.VMEM((dm // LANES, tm, LANES), F32)],
        compiler_params=pltpu.CompilerParams(dimension_semantics=("parallel",)),
        name="norm_proj",
    )(x2d, g.reshape(1, dm), w)


def _attn_body(q_ref, kp_ref, kc_ref, kn_ref, vp_ref, vc_ref, vn_ref, o_ref, lse_ref,
               kcat, vcat, *o_slab, group, dilation, tq, n_sub):
    qi = pl.program_id(1)
    r = pl.program_id(2)
    blk = (0,) if dilation == 1 else (0, 0)
    kcat[0:HALF_KEYS] = kp_ref[blk]
    kcat[HALF_KEYS:HALF_KEYS + tq] = kc_ref[blk]
    kcat[HALF_KEYS + tq:] = kn_ref[blk]
    vcat[0:HALF_KEYS] = vp_ref[blk]
    vcat[HALF_KEYS:HALF_KEYS + tq] = vc_ref[blk]
    vcat[HALF_KEYS + tq:] = vn_ref[blk]

    qrow = lax.broadcasted_iota(jnp.int32, (Q_SUB, K_SUB), 0)
    kcol = lax.broadcasted_iota(jnp.int32, (Q_SUB, K_SUB), 1)
    absd = jnp.abs(kcol - HALF_KEYS - qrow)
    band = absd <= HALF_KEYS
    dist = (absd * dilation).astype(F32)
    lane = lax.broadcasted_iota(jnp.int32, (Q_SUB, LANES), 1)
    scale = HEAD_DIM ** -0.5

    for sub in range(tq // Q_SUB):
        r0 = sub * Q_SUB
        kpos = qi * tq + r0 - HALF_KEYS + kcol
        valid = band & (kpos >= 0) & (kpos < n_sub)
        lse_tile = jnp.zeros((Q_SUB, LANES), F32)
        for h in range(HEADS_PER_GROUP):
            slope = 2.0 ** (-8.0 * (group * HEADS_PER_GROUP + h + 1) / N_ATT_HEADS)
            c0 = h * HEAD_DIM
            q = q_ref[blk + (slice(r0, r0 + Q_SUB), slice(c0, c0 + HEAD_DIM))]
            k = kcat[r0:r0 + K_SUB, c0:c0 + HEAD_DIM]
            v = vcat[r0:r0 + K_SUB, c0:c0 + HEAD_DIM]
            s = lax.dot_general(q, k, (((1,), (1,)), ((), ())), preferred_element_type=F32) * scale
            s = jnp.where(valid, s - slope * dist, NEG_INF)
            m = jnp.max(s, axis=-1, keepdims=True)
            p = jnp.exp(s - m)
            den = jnp.sum(p, axis=-1, keepdims=True)
            o = jnp.dot(p.astype(BF16), v, preferred_element_type=F32) / den
            if dilation == 1:
                o_ref[0, r0:r0 + Q_SUB, c0:c0 + HEAD_DIM] = o.astype(BF16)
            else:
                o_slab[0][h, pl.ds(r0 * dilation + r, Q_SUB, stride=dilation), :] = o
            lse = m + jnp.log(den)
            lse_tile = jnp.where((lane >= h * LSE_REP) & (lane < (h + 1) * LSE_REP), lse, lse_tile)
        if dilation == 1:
            lse_ref[0, r0:r0 + Q_SUB, :] = lse_tile
        else:
            lse_ref[0, pl.ds(r0 * dilation + r, Q_SUB, stride=dilation), :] = lse_tile

    if dilation > 1:
        @pl.when(r == dilation - 1)
        def _():
            for h in range(HEADS_PER_GROUP):
                o_ref[0, :, h * HEAD_DIM:(h + 1) * HEAD_DIM] = o_slab[0][h].astype(BF16)


def _attention_group(qkv, group, dilation, tq):
    if dilation == 1:
        b, s, _ = qkv.shape
    else:
        b, _, n_sub, _ = qkv.shape
        s = n_sub * dilation
    n_sub = s // dilation
    halo_per_tile = tq // HALF_KEYS
    last_halo = n_sub // HALF_KEYS - 1

    def spec(rows, row_index, col):
        if dilation == 1:
            return pl.BlockSpec((1, rows, GROUP_WIDTH), lambda bi, qi, r: (bi, row_index(qi), col))
        return pl.BlockSpec((1, 1, rows, GROUP_WIDTH), lambda bi, qi, r: (bi, r, row_index(qi), col))

    def main_spec(col):
        return spec(tq, lambda qi: qi, col)

    def prev_spec(col):
        return spec(HALF_KEYS, lambda qi: jnp.maximum(qi * halo_per_tile - 1, 0), col)

    def next_spec(col):
        return spec(HALF_KEYS, lambda qi: jnp.minimum((qi + 1) * halo_per_tile, last_halo), col)

    scratch = [pltpu.VMEM((tq + 2 * HALF_KEYS, GROUP_WIDTH), BF16),
               pltpu.VMEM((tq + 2 * HALF_KEYS, GROUP_WIDTH), BF16)]
    if dilation > 1:
        scratch.append(pltpu.VMEM((HEADS_PER_GROUP, tq * dilation, LANES), F32))

    return pl.pallas_call(
        functools.partial(_attn_body, group=group, dilation=dilation, tq=tq, n_sub=n_sub),
        grid=(b, n_sub // tq, dilation),
        in_specs=[main_spec(Q_COL),
                  prev_spec(K_COL), main_spec(K_COL), next_spec(K_COL),
                  prev_spec(V_COL), main_spec(V_COL), next_spec(V_COL)],
        out_specs=[pl.BlockSpec((1, tq * dilation, GROUP_WIDTH), lambda bi, qi, r: (bi, qi, 0)),
                   pl.BlockSpec((1, tq * dilation, LANES), lambda bi, qi, r: (bi, qi, 0))],
        out_shape=[jax.ShapeDtypeStruct((b, s, GROUP_WIDTH), BF16),
                   jax.ShapeDtypeStruct((b, s, LANES), F32)],
        scratch_shapes=scratch,
        compiler_params=pltpu.CompilerParams(dimension_semantics=("parallel", "parallel", "arbitrary")),
        name=f"attn_d{dilation}",
    )(qkv, qkv, qkv, qkv, qkv, qkv, qkv)


def _lru_direction(xp_ref, xc_ref, xn_ref, h_ref, tile, n_tiles, rev, dirn,
                   cw_ref, cb_ref, wa_ref, ba_ref, wx_ref, bx_ref, lam_ref,
                   xpad, a_s, b_s, hpad, carry, *, tt):
    width = xc_ref.shape[-1]
    bw = width // LRU_BLOCKS
    n_chunks = width // LANES
    seg_len = tt // SEGMENTS
    pitch = seg_len + SEGMENTS
    sub = lax.broadcasted_iota(jnp.int32, (SEGMENTS, width), 0)

    xf = xc_ref[0].astype(F32)
    for c in range(n_chunks):
        for k in range(SEGMENTS):
            xpad[c, k * pitch:k * pitch + seg_len, :] = xf[k * seg_len:(k + 1) * seg_len, c * LANES:(c + 1) * LANES]
    xs = jnp.concatenate(
        [jnp.concatenate([xpad[c, pl.ds(j, SEGMENTS, stride=pitch), :] for j in range(seg_len)], axis=0)
         for c in range(n_chunks)], axis=1)

    halo_prev = jnp.where(tile == 0, 0.0, xp_ref[0, BF16_ROWS - CONV_LEFT:BF16_ROWS, :].astype(F32))
    halo_next = jnp.where(tile == n_tiles - 1, 0.0, xn_ref[0, 0:1, :].astype(F32))
    before1 = jnp.where(sub == 0, halo_prev[1:2], pltpu.roll(xs[tt - SEGMENTS:tt], 1, 0))
    before2 = jnp.where(sub == 0, halo_prev[0:1], pltpu.roll(xs[tt - 2 * SEGMENTS:tt - SEGMENTS], 1, 0))
    after1 = jnp.where(sub == SEGMENTS - 1, halo_next, pltpu.roll(xs[0:SEGMENTS], SEGMENTS - 1, 0))
    xm2 = jnp.concatenate([before2, before1, xs[:tt - 2 * SEGMENTS]], axis=0)
    xm1 = jnp.concatenate([before1, xs[:tt - SEGMENTS]], axis=0)
    xp1 = jnp.concatenate([xs[SEGMENTS:], after1], axis=0)
    xc = cb_ref[...] + xm2 * cw_ref[0:1] + xm1 * cw_ref[1:2] + xs * cw_ref[2:3] + xp1 * cw_ref[3:4]
    xcb = xc.astype(BF16)

    lam = lam_ref[dirn:dirn + 1]
    neg_lam = -lam
    softplus = jnp.maximum(neg_lam, 0.0) + jnp.log1p(jnp.exp(-jnp.abs(neg_lam)))
    coef = -RG_C * softplus

    sub_blk = lax.broadcasted_iota(jnp.int32, (SEGMENTS, bw), 0)
    if rev:
        start_rows = slice(tt - SEGMENTS, tt)
        first = (tile == n_tiles - 1) & (sub_blk == SEGMENTS - 1)
    else:
        start_rows = slice(0, SEGMENTS)
        first = (tile == 0) & (sub_blk == 0)

    for blk in range(LRU_BLOCKS):
        c0 = blk * bw
        xblk = xcb[:, c0:c0 + bw]
        pa = jnp.dot(xblk, wa_ref[dirn, blk], preferred_element_type=F32) + ba_ref[dirn:dirn + 1, c0:c0 + bw]
        px = jnp.dot(xblk, wx_ref[dirn, blk], preferred_element_type=F32) + bx_ref[dirn:dirn + 1, c0:c0 + bw]
        log_a = coef[:, c0:c0 + bw] * _sigmoid(pa)
        a = jnp.exp(log_a)
        u = jnp.tanh(log_a) * (-1.0 - a * a)
        mult = jnp.where(u > 0.0, u * lax.rsqrt(u), 0.0)
        gated = _sigmoid(px) * xc[:, c0:c0 + bw]
        b = mult * gated
        a_s[:, c0:c0 + bw] = a
        b_s[:, c0:c0 + bw] = b
        b_s[start_rows, c0:c0 + bw] = jnp.where(first, gated[start_rows], b[start_rows])

    steps = range(seg_len - 1, -1, -1) if rev else range(seg_len)
    segs = range(SEGMENTS - 1, -1, -1) if rev else range(SEGMENTS)
    sub1 = lax.broadcasted_iota(jnp.int32, (SEGMENTS, LANES), 0)
    for c in range(n_chunks):
        cs = slice(c * LANES, (c + 1) * LANES)
        prod = jnp.ones((SEGMENTS, LANES), F32)
        h = jnp.zeros((SEGMENTS, LANES), F32)
        for j in steps:
            aj = a_s[j * SEGMENTS:(j + 1) * SEGMENTS, cs]
            h = aj * h + b_s[j * SEGMENTS:(j + 1) * SEGMENTS, cs]
            prod = aj * prod
        state = carry[dirn:dirn + 1, cs]
        h0 = jnp.zeros((SEGMENTS, LANES), F32)
        for k in segs:
            h0 = jnp.where(sub1 == k, state, h0)
            state = prod[k:k + 1] * state + h[k:k + 1]
        carry[dirn:dirn + 1, cs] = state
        h = h0
        for j in steps:
            h = a_s[j * SEGMENTS:(j + 1) * SEGMENTS, cs] * h + b_s[j * SEGMENTS:(j + 1) * SEGMENTS, cs]
            hpad[c, pl.ds(j, SEGMENTS, stride=pitch), :] = h
        for k in range(SEGMENTS):
            h_ref[0, k * seg_len:(k + 1) * seg_len, cs] = hpad[c, k * pitch:k * pitch + seg_len, :].astype(h_ref.dtype)


def _lru_body(xfp_ref, xfc_ref, xfn_ref, xbp_ref, xbc_ref, xbn_ref,
              cw_ref, cb_ref, wa_ref, ba_ref, wx_ref, bx_ref, lam_ref,
              hf_ref, hb_ref, xpad_f, a_f, b_f, hpad_f, xpad_b, a_b, b_b, hpad_b, carry, *, tt):
    i = pl.program_id(1)
    n_tiles = pl.num_programs(1)

    @pl.when(i == 0)
    def _():
        carry[...] = jnp.zeros_like(carry)

    params = (cw_ref, cb_ref, wa_ref, ba_ref, wx_ref, bx_ref, lam_ref)
    _lru_direction(xfp_ref, xfc_ref, xfn_ref, hf_ref, i, n_tiles, False, 0, *params,
                   xpad_f, a_f, b_f, hpad_f, carry, tt=tt)
    _lru_direction(xbp_ref, xbc_ref, xbn_ref, hb_ref, n_tiles - 1 - i, n_tiles, True, 1, *params,
                   xpad_b, a_b, b_b, hpad_b, carry, tt=tt)


def _lru(nat, conv_w, conv_b, w_a, b_a, w_x, b_x, lam, tt):
    b, s, _ = nat.shape
    width = conv_w.shape[-1]
    n_tiles = s // tt
    halo_per_tile = tt // BF16_ROWS
    last_halo = s // BF16_ROWS - 1

    def tile_of(i, rev):
        return n_tiles - 1 - i if rev else i

    def specs(rev):
        return [
            pl.BlockSpec((1, BF16_ROWS, width),
                         lambda bi, i: (bi, jnp.maximum(tile_of(i, rev) * halo_per_tile - 1, 0), XLRU_COL)),
            pl.BlockSpec((1, tt, width), lambda bi, i: (bi, tile_of(i, rev), XLRU_COL)),
            pl.BlockSpec((1, BF16_ROWS, width),
                         lambda bi, i: (bi, jnp.minimum((tile_of(i, rev) + 1) * halo_per_tile, last_halo), XLRU_COL)),
        ]

    def whole(a):
        return pl.BlockSpec(a.shape, lambda bi, i: (0,) * a.ndim)

    consts = (conv_w, conv_b.reshape(1, width), w_a, b_a, w_x, b_x, lam)
    padded_rows = tt + SEGMENTS * SEGMENTS
    per_direction = [pltpu.VMEM((width // LANES, padded_rows, LANES), F32),
                     pltpu.VMEM((tt, width), F32),
                     pltpu.VMEM((tt, width), F32),
                     pltpu.VMEM((width // LANES, padded_rows, LANES), F32)]
    return pl.pallas_call(
        functools.partial(_lru_body, tt=tt),
        grid=(b, n_tiles),
        in_specs=specs(False) + specs(True) + [whole(a) for a in consts],
        out_specs=[pl.BlockSpec((1, tt, width), lambda bi, i: (bi, i, 0)),
                   pl.BlockSpec((1, tt, width), lambda bi, i: (bi, n_tiles - 1 - i, 0))],
        out_shape=[jax.ShapeDtypeStruct((b, s, width), BF16)] * 2,
        scratch_shapes=per_direction + per_direction + [pltpu.VMEM((8, width), F32)],
        compiler_params=pltpu.CompilerParams(dimension_semantics=("parallel", "arbitrary")),
        name="lru",
    )(nat, nat, nat, nat, nat, nat, *consts)


def _merge_body(o0_ref, o1_ref, o2_ref, l0_ref, l1_ref, l2_ref, gatt_ref, hf_ref, hb_ref, glru_ref,
                za_ref, zl_ref, x_ref, bg_ref, woa_ref, wol_ref, wout_ref, fg_ref, out_ref, *, final):
    tm = x_ref.shape[0]
    dm = x_ref.shape[1]
    l0, l1, l2 = l0_ref[...], l1_ref[...], l2_ref[...]
    mx = jnp.maximum(jnp.maximum(l0, l1), l2)
    e0, e1, e2 = jnp.exp(l0 - mx), jnp.exp(l1 - mx), jnp.exp(l2 - mx)
    inv = 1.0 / (e0 + e1 + e2)
    alphas = (e0 * inv, e1 * inv, e2 * inv)
    o_refs = (o0_ref, o1_ref, o2_ref)

    heads = []
    for h in range(HEADS_PER_GROUP):
        c0 = h * HEAD_DIM
        acc = jnp.zeros((tm, HEAD_DIM), F32)
        for g in range(N_GROUPS):
            w = jnp.broadcast_to(alphas[g][:, h * LSE_REP:h * LSE_REP + 1], (tm, HEAD_DIM))
            acc = acc + w * o_refs[g][:, c0:c0 + HEAD_DIM].astype(F32)
        heads.append(acc)
    mixed = jnp.concatenate(heads, axis=-1)

    gatt = gatt_ref[...].astype(F32)
    y_att = (mixed * (gatt * _sigmoid(gatt))).astype(BF16)
    glru = glru_ref[...].astype(F32)
    h_sum = hf_ref[...].astype(F32) + hb_ref[...].astype(F32)
    y_lru = (h_sum * (glru * _sigmoid(glru))).astype(BF16)

    p_att = jnp.dot(y_att, woa_ref[...], preferred_element_type=F32)
    p_lru = jnp.dot(y_lru, wol_ref[...], preferred_element_type=F32)
    gate_att = _sigmoid(za_ref[...].astype(F32) + bg_ref[:, 0:dm])
    gate_lru = _sigmoid(zl_ref[...].astype(F32) + bg_ref[:, dm:2 * dm])
    merged = (gate_att * p_att + gate_lru * p_lru).astype(BF16)
    y = x_ref[...] + jnp.dot(merged, wout_ref[...], preferred_element_type=F32)
    if final:
        ms = jnp.mean(y * y, axis=-1, keepdims=True)
        y = y * lax.rsqrt(ms + NORM_EPS) * fg_ref[...]
    out_ref[...] = y


def _merge(x2d, nat2d, outs, lses, h_f, h_b, b_gate, w_o_att, w_o_lru, w_out, final_g, final, tm):
    t, dm = x2d.shape

    def rows(width, col=0):
        return pl.BlockSpec((tm, width), lambda i: (i, col))

    def whole(a):
        return pl.BlockSpec(a.shape, lambda i: (0,) * a.ndim)

    consts = (b_gate.reshape(1, 2 * dm), w_o_att, w_o_lru, w_out, final_g.reshape(1, dm))
    return pl.pallas_call(
        functools.partial(_merge_body, final=final),
        grid=(t // tm,),
        in_specs=[rows(GROUP_WIDTH)] * 3 + [rows(LANES)] * 3
                 + [rows(GROUP_WIDTH, GATT_COL), rows(dm), rows(dm), rows(dm, GLRU_COL),
                    rows(dm, LOGIT_ATT_COL), rows(dm, LOGIT_LRU_COL), rows(dm)]
                 + [whole(a) for a in consts],
        out_specs=rows(dm),
        out_shape=jax.ShapeDtypeStruct((t, dm), F32),
        compiler_params=pltpu.CompilerParams(dimension_semantics=("parallel",)),
        name="merge_final" if final else "merge",
    )(*outs, *lses, nat2d, h_f, h_b, nat2d, nat2d, nat2d, x2d, *consts)


def _split_projection_weight(w):
    gw = GROUP_WIDTH
    qkv = [w[:, i * gw:(i + 1) * gw] for i in range(3 * N_GROUPS)]
    rest = w[:, 3 * N_GROUPS * gw:]
    ordered = [qkv[0], qkv[3], qkv[6], rest] + [qkv[g + 3 * i] for g in (1, 2) for i in range(3)]
    return jnp.concatenate(ordered, axis=1).astype(BF16)


def kernel(x, norm_g, w_in, b_gate, conv_w, conv_b, rg_w_a, rg_b_a, rg_w_x, rg_b_x, rg_lam,
           w_o_att, w_o_lru, w_out, final_g):
    b, s, dm = x.shape
    depth = w_in.shape[0]
    t = b * s
    x2d = x.reshape(t, dm)
    for l in range(depth):
        nat2d, s4, s16 = _norm_proj(x2d, norm_g[l], _split_projection_weight(w_in[l]), b)
        nat = nat2d.reshape(b, s, -1)
        outs, lses = [], []
        for g, (qkv, tq) in enumerate(((nat, 1024), (s4, 512), (s16, 256))):
            o, lse = _attention_group(qkv, g, ATT_PATTERNS[g][1], tq=tq)
            outs.append(o.reshape(t, GROUP_WIDTH))
            lses.append(lse.reshape(t, LANES))
        h_f, h_b = _lru(nat, conv_w[l], conv_b[l], rg_w_a[l].astype(BF16), rg_b_a[l],
                        rg_w_x[l].astype(BF16), rg_b_x[l], rg_lam[l], tt=512)
        x2d = _merge(x2d, nat2d, outs, lses, h_f.reshape(t, dm), h_b.reshape(t, dm), b_gate[l],
                     w_o_att[l].astype(BF16), w_o_lru[l].astype(BF16), w_out[l].astype(BF16),
                     final_g, final=(l == depth - 1), tm=512)
    return x2d.reshape(b, s, dm)
```

```python
import functools
import itertools

import jax
import jax.numpy as jnp
from jax import lax
from jax.experimental import pallas as pl
from jax.experimental.pallas import tpu as pltpu

F32 = jnp.float32
BF16 = jnp.bfloat16

HEAD_DIM = 128
HEADS_PER_GROUP = 4
ATT_PATTERNS = ((128, 1), (512, 4), (2048, 16))
N_GROUPS = len(ATT_PATTERNS)
N_ATT_HEADS = N_GROUPS * HEADS_PER_GROUP
GROUP_WIDTH = HEADS_PER_GROUP * HEAD_DIM
QKV_WIDTH = 3 * GROUP_WIDTH
NEG_INF = -1e30
LRU_BLOCKS = 4
RG_C = 8.0
NORM_EPS = 1e-6
CONV_LEFT = 2

XLRU_COL, GLRU_COL, LOGIT_ATT_COL, LOGIT_LRU_COL = 0, 1, 2, 3
GATT_COL = 8
Q_COL, K_COL, V_COL = 0, 1, 2

LANES = 128
F32_ROWS = 8
BF16_ROWS = 16
HALF_KEYS = 64
Q_SUB = 128
K_SUB = Q_SUB + 2 * HALF_KEYS
LSE_REP = LANES // HEADS_PER_GROUP
SEGMENTS = F32_ROWS

GATE_PROJ_ROWS = 1024
QKV_ROWS = 512
PROJ_COLS = 768
ATTN_Q_ROWS = {1: 1024, 4: 512, 16: 256}
MERGE_ROWS = 512
QKV_LRU_VMEM_BYTES = 60 * 1024 * 1024
LRU_STEPS_PER_PROJ_STEP = 4

NEG_LOG2_E = -1.4426950408889634


def _sigmoid(x):
    return 1.0 / (1.0 + jnp.exp2(x * NEG_LOG2_E))


def _rms_norm(x, g):
    ms = jnp.mean(x * x, axis=-1, keepdims=True)
    return x * lax.rsqrt(ms + NORM_EPS) * g


def _resident(a):
    return pl.BlockSpec(a.shape, lambda *_: (0,) * a.ndim, pipeline_mode=pl.Buffered(1))


def _gate_proj_body(x_ref, g_ref, w_ref, o_ref):
    xn = _rms_norm(x_ref[...], g_ref[...]).astype(BF16)
    for n0 in range(0, o_ref.shape[1], PROJ_COLS):
        o_ref[:, n0:n0 + PROJ_COLS] = jnp.dot(
            xn, w_ref[:, n0:n0 + PROJ_COLS], preferred_element_type=F32).astype(BF16)


def _gate_proj(x2d, g, w):
    t, dm = x2d.shape
    n = w.shape[1]
    tm = GATE_PROJ_ROWS
    return pl.pallas_call(
        _gate_proj_body,
        grid=(t // tm,),
        in_specs=[pl.BlockSpec((tm, dm), lambda i: (i, 0)), _resident(g), _resident(w)],
        out_specs=pl.BlockSpec((tm, n), lambda i: (i, 0)),
        out_shape=jax.ShapeDtypeStruct((t, n), BF16),
        compiler_params=pltpu.CompilerParams(dimension_semantics=("parallel",)),
        name="gate_proj",
    )(x2d, g, w)


def _qkv_proj_steps(x_ref, g_ref, w_ref, nat_ref, s4_ref, s16_ref, xn_ref, xn4_ref, xn16_ref, slab_ref):
    tm, dm = x_ref.shape[1:]

    y = _rms_norm(x_ref[0], g_ref[...])
    xn_ref[...] = y.astype(BF16)
    for c in range(dm // LANES):
        slab_ref[c] = y[:, c * LANES:(c + 1) * LANES]
    for ref, d in ((xn4_ref, 4), (xn16_ref, 16)):
        rows = tm // d
        for c in range(dm // LANES):
            for r in range(d):
                ref[r * rows:(r + 1) * rows, c * LANES:(c + 1) * LANES] = (
                    slab_ref[c, pl.ds(r, rows, stride=d), :].astype(BF16))

    yield

    def project(lhs_ref, col):
        return jnp.dot(lhs_ref[...], w_ref[:, col:col + PROJ_COLS], preferred_element_type=F32).astype(BF16)

    for n0 in range(0, QKV_WIDTH, PROJ_COLS):
        nat_ref[0, :, n0:n0 + PROJ_COLS] = project(xn_ref, n0)
        yield
        s4_ref[0, :, :, n0:n0 + PROJ_COLS] = project(xn4_ref, QKV_WIDTH + n0).reshape(4, tm // 4, PROJ_COLS)
        yield
        s16_ref[0, :, :, n0:n0 + PROJ_COLS] = project(xn16_ref, 2 * QKV_WIDTH + n0).reshape(16, tm // 16, PROJ_COLS)
        yield


_EXHAUSTED = object()


def _interleave(major, minor, ratio):
    live = [True, True]
    while any(live):
        for which, gen, count in ((0, major, 1), (1, minor, ratio)):
            for _ in range(count):
                if live[which]:
                    live[which] = next(gen, _EXHAUSTED) is not _EXHAUSTED


def _lru_direction_steps(xp_ref, xc_ref, xn_ref, h_ref, tile, n_tiles, rev, dirn,
                         cw_ref, cb_ref, wa_ref, ba_ref, wx_ref, bx_ref, lam_ref,
                         xpad, a_s, b_s, hpad, carry):
    tt, width = xc_ref.shape[1:]
    bw = width // LRU_BLOCKS
    n_chunks = width // LANES
    seg_len = tt // SEGMENTS
    pitch = seg_len + SEGMENTS
    sub = lax.broadcasted_iota(jnp.int32, (SEGMENTS, width), 0)

    xf = xc_ref[0].astype(F32)
    for c in range(n_chunks):
        for k in range(SEGMENTS):
            xpad[c, k * pitch:k * pitch + seg_len, :] = xf[k * seg_len:(k + 1) * seg_len, c * LANES:(c + 1) * LANES]
    xs = jnp.concatenate(
        [jnp.concatenate([xpad[c, pl.ds(j, SEGMENTS, stride=pitch), :] for j in range(seg_len)], axis=0)
         for c in range(n_chunks)], axis=1)

    halo_prev = jnp.where(tile == 0, 0.0, xp_ref[0, BF16_ROWS - CONV_LEFT:BF16_ROWS, :].astype(F32))
    halo_next = jnp.where(tile == n_tiles - 1, 0.0, xn_ref[0, 0:1, :].astype(F32))
    before1 = jnp.where(sub == 0, halo_prev[1:2], pltpu.roll(xs[tt - SEGMENTS:tt], 1, 0))
    before2 = jnp.where(sub == 0, halo_prev[0:1], pltpu.roll(xs[tt - 2 * SEGMENTS:tt - SEGMENTS], 1, 0))
    after1 = jnp.where(sub == SEGMENTS - 1, halo_next, pltpu.roll(xs[0:SEGMENTS], SEGMENTS - 1, 0))
    xm2 = jnp.concatenate([before2, before1, xs[:tt - 2 * SEGMENTS]], axis=0)
    xm1 = jnp.concatenate([before1, xs[:tt - SEGMENTS]], axis=0)
    xp1 = jnp.concatenate([xs[SEGMENTS:], after1], axis=0)
    xc = cb_ref[...] + xm2 * cw_ref[0:1] + xm1 * cw_ref[1:2] + xs * cw_ref[2:3] + xp1 * cw_ref[3:4]
    xcb = xc.astype(BF16)
    yield

    lam = lam_ref[dirn:dirn + 1]
    neg_lam = -lam
    softplus = jnp.maximum(neg_lam, 0.0) + jnp.log1p(jnp.exp(-jnp.abs(neg_lam)))
    coef = -RG_C * softplus

    sub_blk = lax.broadcasted_iota(jnp.int32, (SEGMENTS, bw), 0)
    if rev:
        start_rows = slice(tt - SEGMENTS, tt)
        first = (tile == n_tiles - 1) & (sub_blk == SEGMENTS - 1)
    else:
        start_rows = slice(0, SEGMENTS)
        first = (tile == 0) & (sub_blk == 0)

    for blk in range(LRU_BLOCKS):
        c0 = blk * bw
        xblk = xcb[:, c0:c0 + bw]
        pa = jnp.dot(xblk, wa_ref[dirn, blk], preferred_element_type=F32) + ba_ref[dirn:dirn + 1, c0:c0 + bw]
        px = jnp.dot(xblk, wx_ref[dirn, blk], preferred_element_type=F32) + bx_ref[dirn:dirn + 1, c0:c0 + bw]
        log_a = coef[:, c0:c0 + bw] * _sigmoid(pa)
        a = jnp.exp(log_a)
        u = jnp.tanh(log_a) * (-1.0 - a * a)
        mult = jnp.where(u > 0.0, u * lax.rsqrt(u), 0.0)
        gated = _sigmoid(px) * xc[:, c0:c0 + bw]
        b = mult * gated
        a_s[:, c0:c0 + bw] = a
        b_s[:, c0:c0 + bw] = b
        b_s[start_rows, c0:c0 + bw] = jnp.where(first, gated[start_rows], b[start_rows])
        yield

    steps = range(seg_len - 1, -1, -1) if rev else range(seg_len)
    segs = range(SEGMENTS - 1, -1, -1) if rev else range(SEGMENTS)
    sub1 = lax.broadcasted_iota(jnp.int32, (SEGMENTS, LANES), 0)
    for c in range(n_chunks):
        cs = slice(c * LANES, (c + 1) * LANES)
        prod = jnp.ones((SEGMENTS, LANES), F32)
        h = jnp.zeros((SEGMENTS, LANES), F32)
        for j in steps:
            aj = a_s[j * SEGMENTS:(j + 1) * SEGMENTS, cs]
            h = aj * h + b_s[j * SEGMENTS:(j + 1) * SEGMENTS, cs]
            prod = aj * prod
        state = carry[dirn:dirn + 1, cs]
        h0 = jnp.zeros((SEGMENTS, LANES), F32)
        for k in segs:
            h0 = jnp.where(sub1 == k, state, h0)
            state = prod[k:k + 1] * state + h[k:k + 1]
        carry[dirn:dirn + 1, cs] = state
        h = h0
        for j in steps:
            h = a_s[j * SEGMENTS:(j + 1) * SEGMENTS, cs] * h + b_s[j * SEGMENTS:(j + 1) * SEGMENTS, cs]
            hpad[c, pl.ds(j, SEGMENTS, stride=pitch), :] = h
        for k in range(SEGMENTS):
            h_ref[0, k * seg_len:(k + 1) * seg_len, cs] = hpad[c, k * pitch:k * pitch + seg_len, :].astype(h_ref.dtype)
        yield


def _qkv_lru_body(x_ref, g_ref, w_ref,
                  xfp_ref, xfc_ref, xfn_ref, xbp_ref, xbc_ref, xbn_ref,
                  cw_ref, cb_ref, wa_ref, ba_ref, wx_ref, bx_ref, lam_ref,
                  nat_ref, s4_ref, s16_ref, hf_ref, hb_ref,
                  xn_ref, xn4_ref, xn16_ref, slab_ref, xpad, a_s, b_s, hpad, carry):
    i = pl.program_id(1)
    n_tiles = pl.num_programs(1)

    @pl.when(i == 0)
    def _():
        carry[...] = jnp.zeros_like(carry)

    params = (cw_ref, cb_ref, wa_ref, ba_ref, wx_ref, bx_ref, lam_ref, xpad, a_s, b_s, hpad, carry)
    proj = _qkv_proj_steps(x_ref, g_ref, w_ref, nat_ref, s4_ref, s16_ref, xn_ref, xn4_ref, xn16_ref, slab_ref)
    lru = itertools.chain(
        _lru_direction_steps(xfp_ref, xfc_ref, xfn_ref, hf_ref, i, n_tiles, False, 0, *params),
        _lru_direction_steps(xbp_ref, xbc_ref, xbn_ref, hb_ref, n_tiles - 1 - i, n_tiles, True, 1, *params))
    _interleave(proj, lru, LRU_STEPS_PER_PROJ_STEP)


def _qkv_lru(x, g, w_qkv, gates, conv_w, conv_b, w_a, b_a, w_x, b_x, lam):
    b, s, dm = x.shape
    width = conv_w.shape[-1]
    tt = QKV_ROWS
    n_tiles = s // tt
    halo_per_tile = tt // BF16_ROWS
    last_halo = s // BF16_ROWS - 1

    def tile_of(i, rev):
        return n_tiles - 1 - i if rev else i

    def lru_specs(rev):
        return [
            pl.BlockSpec((1, BF16_ROWS, width),
                         lambda bi, i: (bi, jnp.maximum(tile_of(i, rev) * halo_per_tile - 1, 0), XLRU_COL)),
            pl.BlockSpec((1, tt, width), lambda bi, i: (bi, tile_of(i, rev), XLRU_COL)),
            pl.BlockSpec((1, BF16_ROWS, width),
                         lambda bi, i: (bi, jnp.minimum((tile_of(i, rev) + 1) * halo_per_tile, last_halo), XLRU_COL)),
        ]

    def strided_spec(d):
        return pl.BlockSpec((1, d, tt // d, QKV_WIDTH), lambda bi, i: (bi, 0, i, 0))

    consts = (conv_w, conv_b.reshape(1, width), w_a, b_a, w_x, b_x, lam)
    padded_rows = tt + SEGMENTS * SEGMENTS
    return pl.pallas_call(
        _qkv_lru_body,
        grid=(b, n_tiles),
        in_specs=[pl.BlockSpec((1, tt, dm), lambda bi, i: (bi, i, 0)), _resident(g), _resident(w_qkv)]
                 + lru_specs(False) + lru_specs(True) + [_resident(a) for a in consts],
        out_specs=[pl.BlockSpec((1, tt, QKV_WIDTH), lambda bi, i: (bi, i, 0)), strided_spec(4), strided_spec(16),
                   pl.BlockSpec((1, tt, width), lambda bi, i: (bi, i, 0)),
                   pl.BlockSpec((1, tt, width), lambda bi, i: (bi, n_tiles - 1 - i, 0))],
        out_shape=[jax.ShapeDtypeStruct((b, s, QKV_WIDTH), BF16),
                   jax.ShapeDtypeStruct((b, 4, s // 4, QKV_WIDTH), BF16),
                   jax.ShapeDtypeStruct((b, 16, s // 16, QKV_WIDTH), BF16),
                   jax.ShapeDtypeStruct((b, s, width), BF16),
                   jax.ShapeDtypeStruct((b, s, width), BF16)],
        scratch_shapes=[pltpu.VMEM((tt, dm), BF16), pltpu.VMEM((tt, dm), BF16), pltpu.VMEM((tt, dm), BF16),
                        pltpu.VMEM((dm // LANES, tt, LANES), F32),
                        pltpu.VMEM((width // LANES, padded_rows, LANES), F32),
                        pltpu.VMEM((tt, width), F32),
                        pltpu.VMEM((tt, width), F32),
                        pltpu.VMEM((width // LANES, padded_rows, LANES), F32),
                        pltpu.VMEM((F32_ROWS, width), F32)],
        compiler_params=pltpu.CompilerParams(dimension_semantics=("parallel", "arbitrary"),
                                             vmem_limit_bytes=QKV_LRU_VMEM_BYTES),
        name="qkv_lru",
    )(x, g, w_qkv, gates, gates, gates, gates, gates, gates, *consts)


def _attn_body(q_ref, kp_ref, kc_ref, kn_ref, vp_ref, vc_ref, vn_ref, o_ref, lse_ref,
               kcat, vcat, *o_slab, group, dilation, tq, n_sub):
    qi = pl.program_id(1)
    r = pl.program_id(2)
    blk = (0,) if dilation == 1 else (0, 0)
    kcat[0:HALF_KEYS] = kp_ref[blk]
    kcat[HALF_KEYS:HALF_KEYS + tq] = kc_ref[blk]
    kcat[HALF_KEYS + tq:] = kn_ref[blk]
    vcat[0:HALF_KEYS] = vp_ref[blk]
    vcat[HALF_KEYS:HALF_KEYS + tq] = vc_ref[blk]
    vcat[HALF_KEYS + tq:] = vn_ref[blk]

    qrow = lax.broadcasted_iota(jnp.int32, (Q_SUB, K_SUB), 0)
    kcol = lax.broadcasted_iota(jnp.int32, (Q_SUB, K_SUB), 1)
    absd = jnp.abs(kcol - HALF_KEYS - qrow)
    band = absd <= HALF_KEYS
    dist = (absd * dilation).astype(F32)
    lane = lax.broadcasted_iota(jnp.int32, (Q_SUB, LANES), 1)
    scale = HEAD_DIM ** -0.5

    for sub in range(tq // Q_SUB):
        r0 = sub * Q_SUB
        kpos = qi * tq + r0 - HALF_KEYS + kcol
        valid = band & (kpos >= 0) & (kpos < n_sub)
        lse_tile = jnp.zeros((Q_SUB, LANES), F32)
        for h in range(HEADS_PER_GROUP):
            slope = 2.0 ** (-8.0 * (group * HEADS_PER_GROUP + h + 1) / N_ATT_HEADS)
            c0 = h * HEAD_DIM
            q = q_ref[blk + (slice(r0, r0 + Q_SUB), slice(c0, c0 + HEAD_DIM))]
            k = kcat[r0:r0 + K_SUB, c0:c0 + HEAD_DIM]
            v = vcat[r0:r0 + K_SUB, c0:c0 + HEAD_DIM]
            s = lax.dot_general(q, k, (((1,), (1,)), ((), ())), preferred_element_type=F32) * scale
            s = jnp.where(valid, s - slope * dist, NEG_INF)
            m = jnp.max(s, axis=-1, keepdims=True)
            p = jnp.exp(s - m)
            den = jnp.sum(p, axis=-1, keepdims=True)
            o = jnp.dot(p.astype(BF16), v, preferred_element_type=F32) / den
            if dilation == 1:
                o_ref[0, r0:r0 + Q_SUB, c0:c0 + HEAD_DIM] = o.astype(BF16)
            else:
                o_slab[0][h, pl.ds(r0 * dilation + r, Q_SUB, stride=dilation), :] = o
            lse = m + jnp.log(den)
            lse_tile = jnp.where((lane >= h * LSE_REP) & (lane < (h + 1) * LSE_REP), lse, lse_tile)
        if dilation == 1:
            lse_ref[0, r0:r0 + Q_SUB, :] = lse_tile
        else:
            lse_ref[0, pl.ds(r0 * dilation + r, Q_SUB, stride=dilation), :] = lse_tile

    if dilation > 1:
        @pl.when(r == dilation - 1)
        def _():
            for h in range(HEADS_PER_GROUP):
                o_ref[0, :, h * HEAD_DIM:(h + 1) * HEAD_DIM] = o_slab[0][h].astype(BF16)


def _attention_group(qkv, group, dilation):
    if dilation == 1:
        b, s, _ = qkv.shape
    else:
        b, _, n_sub, _ = qkv.shape
        s = n_sub * dilation
    n_sub = s // dilation
    tq = min(ATTN_Q_ROWS[dilation], n_sub)
    halo_per_tile = tq // HALF_KEYS
    last_halo = n_sub // HALF_KEYS - 1

    def spec(rows, row_index, col):
        if dilation == 1:
            return pl.BlockSpec((1, rows, GROUP_WIDTH), lambda bi, qi, r: (bi, row_index(qi), col))
        return pl.BlockSpec((1, 1, rows, GROUP_WIDTH), lambda bi, qi, r: (bi, r, row_index(qi), col))

    def main_spec(col):
        return spec(tq, lambda qi: qi, col)

    def prev_spec(col):
        return spec(HALF_KEYS, lambda qi: jnp.maximum(qi * halo_per_tile - 1, 0), col)

    def next_spec(col):
        return spec(HALF_KEYS, lambda qi: jnp.minimum((qi + 1) * halo_per_tile, last_halo), col)

    scratch = [pltpu.VMEM((tq + 2 * HALF_KEYS, GROUP_WIDTH), BF16),
               pltpu.VMEM((tq + 2 * HALF_KEYS, GROUP_WIDTH), BF16)]
    if dilation > 1:
        scratch.append(pltpu.VMEM((HEADS_PER_GROUP, tq * dilation, LANES), F32))

    return pl.pallas_call(
        functools.partial(_attn_body, group=group, dilation=dilation, tq=tq, n_sub=n_sub),
        grid=(b, n_sub // tq, dilation),
        in_specs=[main_spec(Q_COL),
                  prev_spec(K_COL), main_spec(K_COL), next_spec(K_COL),
                  prev_spec(V_COL), main_spec(V_COL), next_spec(V_COL)],
        out_specs=[pl.BlockSpec((1, tq * dilation, GROUP_WIDTH), lambda bi, qi, r: (bi, qi, 0)),
                   pl.BlockSpec((1, tq * dilation, LANES), lambda bi, qi, r: (bi, qi, 0))],
        out_shape=[jax.ShapeDtypeStruct((b, s, GROUP_WIDTH), BF16),
                   jax.ShapeDtypeStruct((b, s, LANES), F32)],
        scratch_shapes=scratch,
        compiler_params=pltpu.CompilerParams(dimension_semantics=("parallel", "parallel", "arbitrary")),
        name=f"attn_d{dilation}",
    )(qkv, qkv, qkv, qkv, qkv, qkv, qkv)


def _merge_body(o0_ref, o1_ref, o2_ref, l0_ref, l1_ref, l2_ref, gatt_ref, hf_ref, hb_ref, glru_ref,
                za_ref, zl_ref, x_ref, bg_ref, woa_ref, wol_ref, wout_ref, fg_ref, out_ref, *, final):
    tm = x_ref.shape[0]
    dm = x_ref.shape[1]
    l0, l1, l2 = l0_ref[...], l1_ref[...], l2_ref[...]
    mx = jnp.maximum(jnp.maximum(l0, l1), l2)
    e0, e1, e2 = jnp.exp(l0 - mx), jnp.exp(l1 - mx), jnp.exp(l2 - mx)
    inv = 1.0 / (e0 + e1 + e2)
    alphas = (e0 * inv, e1 * inv, e2 * inv)
    o_refs = (o0_ref, o1_ref, o2_ref)

    heads = []
    for h in range(HEADS_PER_GROUP):
        c0 = h * HEAD_DIM
        acc = jnp.zeros((tm, HEAD_DIM), F32)
        for g in range(N_GROUPS):
            w = jnp.broadcast_to(alphas[g][:, h * LSE_REP:h * LSE_REP + 1], (tm, HEAD_DIM))
            acc = acc + w * o_refs[g][:, c0:c0 + HEAD_DIM].astype(F32)
        heads.append(acc)
    mixed = jnp.concatenate(heads, axis=-1)

    gatt = gatt_ref[...].astype(F32)
    y_att = (mixed * (gatt * _sigmoid(gatt))).astype(BF16)
    glru = glru_ref[...].astype(F32)
    h_sum = hf_ref[...].astype(F32) + hb_ref[...].astype(F32)
    y_lru = (h_sum * (glru * _sigmoid(glru))).astype(BF16)

    p_att = jnp.dot(y_att, woa_ref[...], preferred_element_type=F32)
    p_lru = jnp.dot(y_lru, wol_ref[...], preferred_element_type=F32)
    gate_att = _sigmoid(za_ref[...].astype(F32) + bg_ref[:, 0:dm])
    gate_lru = _sigmoid(zl_ref[...].astype(F32) + bg_ref[:, dm:2 * dm])
    merged = (gate_att * p_att + gate_lru * p_lru).astype(BF16)
    y = x_ref[...] + jnp.dot(merged, wout_ref[...], preferred_element_type=F32)
    if final:
        y = _rms_norm(y, fg_ref[...])
    out_ref[...] = y


def _merge(x2d, gates2d, outs, lses, h_f, h_b, b_gate, w_o_att, w_o_lru, w_out, final_g, final):
    t, dm = x2d.shape
    tm = MERGE_ROWS

    def rows(width, col=0):
        return pl.BlockSpec((tm, width), lambda i: (i, col))

    consts = (b_gate.reshape(1, 2 * dm), w_o_att, w_o_lru, w_out, final_g.reshape(1, dm))
    return pl.pallas_call(
        functools.partial(_merge_body, final=final),
        grid=(t // tm,),
        in_specs=[rows(GROUP_WIDTH)] * 3 + [rows(LANES)] * 3
                 + [rows(GROUP_WIDTH, GATT_COL), rows(dm), rows(dm), rows(dm, GLRU_COL),
                    rows(dm, LOGIT_ATT_COL), rows(dm, LOGIT_LRU_COL), rows(dm)]
                 + [_resident(a) for a in consts],
        out_specs=rows(dm),
        out_shape=jax.ShapeDtypeStruct((t, dm), F32),
        compiler_params=pltpu.CompilerParams(dimension_semantics=("parallel",)),
        name="merge_final" if final else "merge",
    )(*outs, *lses, gates2d, h_f, h_b, gates2d, gates2d, gates2d, x2d, *consts)


def _split_projection_weight(w):
    gw = GROUP_WIDTH
    qkv = [w[:, i * gw:(i + 1) * gw] for i in range(3 * N_GROUPS)]
    gate_att = w[:, 3 * N_GROUPS * gw:(3 * N_GROUPS + 1) * gw]
    rest = w[:, (3 * N_GROUPS + 1) * gw:]
    w_gates = jnp.concatenate([rest, gate_att], axis=1).astype(BF16)
    w_qkv = jnp.concatenate([qkv[g + N_GROUPS * i] for g in range(N_GROUPS) for i in range(3)], axis=1).astype(BF16)
    return w_gates, w_qkv


def kernel(x, norm_g, w_in, b_gate, conv_w, conv_b, rg_w_a, rg_b_a, rg_w_x, rg_b_x, rg_lam,
           w_o_att, w_o_lru, w_out, final_g):
    b, s, dm = x.shape
    depth = w_in.shape[0]
    t = b * s
    x2d = x.reshape(t, dm)
    for l in range(depth):
        g = norm_g[l].reshape(1, dm)
        w_gates, w_qkv = _split_projection_weight(w_in[l])
        gates2d = _gate_proj(x2d, g, w_gates)
        qkv1, qkv4, qkv16, h_f, h_b = _qkv_lru(
            x2d.reshape(b, s, dm), g, w_qkv, gates2d.reshape(b, s, -1), conv_w[l], conv_b[l],
            rg_w_a[l].astype(BF16), rg_b_a[l], rg_w_x[l].astype(BF16), rg_b_x[l], rg_lam[l])
        outs, lses = [], []
        for group, qkv in enumerate((qkv1, qkv4, qkv16)):
            o, lse = _attention_group(qkv, group, ATT_PATTERNS[group][1])
            outs.append(o.reshape(t, GROUP_WIDTH))
            lses.append(lse.reshape(t, LANES))
        x2d = _merge(x2d, gates2d, outs, lses, h_f.reshape(t, dm), h_b.reshape(t, dm), b_gate[l],
                     w_o_att[l].astype(BF16), w_o_lru[l].astype(BF16), w_out[l].astype(BF16),
                     final_g, final=(l == depth - 1))
    return x2d.reshape(b, s, dm)
```

```python
import functools

import jax
import jax.numpy as jnp
from jax import lax
from jax.experimental import pallas as pl
from jax.experimental.pallas import tpu as pltpu

F32 = jnp.float32
BF16 = jnp.bfloat16

HEAD_DIM = 128
HEADS_PER_GROUP = 4
ATT_PATTERNS = ((128, 1), (512, 4), (2048, 16))
N_GROUPS = len(ATT_PATTERNS)
N_ATT_HEADS = N_GROUPS * HEADS_PER_GROUP
GROUP_WIDTH = HEADS_PER_GROUP * HEAD_DIM
QKV_WIDTH = 3 * GROUP_WIDTH
NEG_INF = -1e30
LRU_BLOCKS = 4
RG_C = 8.0
NORM_EPS = 1e-6
CONV_LEFT = 2

Q_COL, K_COL, V_COL, GATT_COL = 0, 1, 2, 3
XLRU_COL, GLRU_COL, LOGIT_ATT_COL, LOGIT_LRU_COL = 2, 3, 4, 5

LANES = 128
F32_ROWS = 8
BF16_ROWS = 16
HALF_KEYS = 64
Q_SUB = 128
K_SUB = Q_SUB + 2 * HALF_KEYS
LSE_REP = LANES // HEADS_PER_GROUP
SEGMENTS = F32_ROWS

PROJ_ROWS = 512
PROJ_COLS = 768
ATTN_Q_ROWS = {1: 1024, 4: 512, 16: 256}
LRU_ROWS = 512
MERGE_ROWS = 512
MERGE_SUB_ROWS = 256

LOG2_E = 1.4426950408889634
LN_2 = 0.6931471805599453


def _sigmoid(x):
    return 1.0 / (1.0 + jnp.exp2(x * -LOG2_E))


def _rms_norm(x, g):
    ms = jnp.mean(x * x, axis=-1, keepdims=True)
    return x * lax.rsqrt(ms + NORM_EPS) * g


def _resident(a):
    return pl.BlockSpec(a.shape, lambda *_: (0,) * a.ndim, pipeline_mode=pl.Buffered(1))


def _norm_proj_body(x_ref, g_ref, w_ref, nat_ref, s4_ref, s16_ref, xn_ref, xn4_ref, xn16_ref, slab_ref):
    tm, dm = x_ref.shape
    n_nat = nat_ref.shape[1]

    y = _rms_norm(x_ref[...], g_ref[...])
    xn_ref[...] = y.astype(BF16)
    for c in range(dm // LANES):
        slab_ref[c] = y[:, c * LANES:(c + 1) * LANES]
    for ref, d in ((xn4_ref, 4), (xn16_ref, 16)):
        rows = tm // d
        for c in range(dm // LANES):
            for r in range(d):
                ref[r * rows:(r + 1) * rows, c * LANES:(c + 1) * LANES] = (
                    slab_ref[c, pl.ds(r, rows, stride=d), :].astype(BF16))

    def project(lhs_ref, col):
        return jnp.dot(lhs_ref[...], w_ref[:, col:col + PROJ_COLS], preferred_element_type=F32).astype(BF16)

    for n0 in range(0, n_nat, PROJ_COLS):
        nat_ref[:, n0:n0 + PROJ_COLS] = project(xn_ref, n0)
    for n0 in range(0, QKV_WIDTH, PROJ_COLS):
        s4_ref[0, :, :, n0:n0 + PROJ_COLS] = project(xn4_ref, n_nat + n0).reshape(4, tm // 4, PROJ_COLS)
        s16_ref[0, :, :, n0:n0 + PROJ_COLS] = (
            project(xn16_ref, n_nat + QKV_WIDTH + n0).reshape(16, tm // 16, PROJ_COLS))


def _norm_proj(x2d, g, w, batch):
    t, dm = x2d.shape
    s = t // batch
    n_nat = w.shape[1] - 2 * QKV_WIDTH
    tm = PROJ_ROWS
    tiles_per_b = s // tm

    def strided_spec(d):
        return pl.BlockSpec((1, d, tm // d, QKV_WIDTH), lambda i: (i // tiles_per_b, 0, i % tiles_per_b, 0))

    return pl.pallas_call(
        _norm_proj_body,
        grid=(t // tm,),
        in_specs=[pl.BlockSpec((tm, dm), lambda i: (i, 0)), _resident(g), _resident(w)],
        out_specs=[pl.BlockSpec((tm, n_nat), lambda i: (i, 0)), strided_spec(4), strided_spec(16)],
        out_shape=[jax.ShapeDtypeStruct((t, n_nat), BF16),
                   jax.ShapeDtypeStruct((batch, 4, s // 4, QKV_WIDTH), BF16),
                   jax.ShapeDtypeStruct((batch, 16, s // 16, QKV_WIDTH), BF16)],
        scratch_shapes=[pltpu.VMEM((tm, dm), BF16), pltpu.VMEM((tm, dm), BF16), pltpu.VMEM((tm, dm), BF16),
                        pltpu.VMEM((dm // LANES, tm, LANES), F32)],
        compiler_params=pltpu.CompilerParams(dimension_semantics=("parallel",)),
        name="norm_proj",
    )(x2d, g, w)


def _attn_body(q_ref, kp_ref, kc_ref, kn_ref, vp_ref, vc_ref, vn_ref, o_ref, lse_ref,
               kcat, vcat, bias, *o_slab, group, dilation, tq, n_sub):
    qi = pl.program_id(1)
    r = pl.program_id(2)
    n_q = pl.num_programs(1)
    blk = (0,) if dilation == 1 else (0, 0)
    kcol = lax.broadcasted_iota(jnp.int32, (Q_SUB, K_SUB), 1)
    slopes = [2.0 ** (-8.0 * (group * HEADS_PER_GROUP + h + 1) / N_ATT_HEADS) for h in range(HEADS_PER_GROUP)]

    @pl.when((pl.program_id(0) == 0) & (qi == 0) & (r == 0))
    def _():
        qrow = lax.broadcasted_iota(jnp.int32, (Q_SUB, K_SUB), 0)
        absd = jnp.abs(kcol - HALF_KEYS - qrow)
        dist = (absd * dilation).astype(F32)
        for h in range(HEADS_PER_GROUP):
            bias[h] = jnp.where(absd <= HALF_KEYS, (-slopes[h] * LOG2_E) * dist, NEG_INF)
            vcat[:, (2 * h + 1) * HEAD_DIM:(2 * h + 2) * HEAD_DIM] = jnp.ones((tq + 2 * HALF_KEYS, HEAD_DIM), BF16)

    kcat[0:HALF_KEYS] = kp_ref[blk]
    kcat[HALF_KEYS:HALF_KEYS + tq] = kc_ref[blk]
    kcat[HALF_KEYS + tq:] = kn_ref[blk]
    for h in range(HEADS_PER_GROUP):
        src = slice(h * HEAD_DIM, (h + 1) * HEAD_DIM)
        dst = slice(2 * h * HEAD_DIM, (2 * h + 1) * HEAD_DIM)
        vcat[0:HALF_KEYS, dst] = vp_ref[blk + (slice(None), src)]
        vcat[HALF_KEYS:HALF_KEYS + tq, dst] = vc_ref[blk + (slice(None), src)]
        vcat[HALF_KEYS + tq:, dst] = vn_ref[blk + (slice(None), src)]

    lane = lax.broadcasted_iota(jnp.int32, (Q_SUB, LANES), 1)
    n_sub_tiles = tq // Q_SUB
    for sub in range(n_sub_tiles):
        r0 = sub * Q_SUB
        lse_tile = jnp.zeros((Q_SUB, LANES), F32)
        for h in range(HEADS_PER_GROUP):
            c0 = h * HEAD_DIM
            q = q_ref[blk + (slice(r0, r0 + Q_SUB), slice(c0, c0 + HEAD_DIM))]
            k = kcat[r0:r0 + K_SUB, c0:c0 + HEAD_DIM]
            raw = lax.dot_general(q, k, (((1,), (1,)), ((), ())), preferred_element_type=F32)
            s2 = raw * (HEAD_DIM ** -0.5 * LOG2_E) + bias[h]
            if sub == 0:
                s2 = jnp.where((qi > 0) | (kcol >= HALF_KEYS), s2, NEG_INF)
            if sub == n_sub_tiles - 1:
                s2 = jnp.where((qi < n_q - 1) | (kcol < HALF_KEYS + Q_SUB), s2, NEG_INF)
            m2 = jnp.max(s2, axis=-1, keepdims=True)
            p = jnp.exp2(s2 - m2).astype(BF16)
            acc = jnp.dot(p, vcat[r0:r0 + K_SUB, 2 * c0:2 * c0 + 2 * HEAD_DIM], preferred_element_type=F32)
            den = acc[:, HEAD_DIM:]
            o = acc[:, :HEAD_DIM] / den
            if dilation == 1:
                o_ref[0, r0:r0 + Q_SUB, c0:c0 + HEAD_DIM] = o.astype(BF16)
            else:
                o_slab[0][h, pl.ds(r0 * dilation + r, Q_SUB, stride=dilation), :] = o
            lse = (m2 + jnp.log2(den)) * LN_2
            lse_tile = jnp.where((lane >= h * LSE_REP) & (lane < (h + 1) * LSE_REP), lse, lse_tile)
        if dilation == 1:
            lse_ref[0, r0:r0 + Q_SUB, :] = lse_tile
        else:
            lse_ref[0, pl.ds(r0 * dilation + r, Q_SUB, stride=dilation), :] = lse_tile

    if dilation > 1:
        @pl.when(r == dilation - 1)
        def _():
            for h in range(HEADS_PER_GROUP):
                o_ref[0, :, h * HEAD_DIM:(h + 1) * HEAD_DIM] = o_slab[0][h].astype(BF16)


def _attention_group(qkv, group, dilation):
    if dilation == 1:
        b, s, _ = qkv.shape
    else:
        b, _, n_sub, _ = qkv.shape
        s = n_sub * dilation
    n_sub = s // dilation
    tq = min(ATTN_Q_ROWS[dilation], n_sub)
    halo_per_tile = tq // HALF_KEYS
    last_halo = n_sub // HALF_KEYS - 1

    def spec(rows, row_index, col):
        if dilation == 1:
            return pl.BlockSpec((1, rows, GROUP_WIDTH), lambda bi, qi, r: (bi, row_index(qi), col))
        return pl.BlockSpec((1, 1, rows, GROUP_WIDTH), lambda bi, qi, r: (bi, r, row_index(qi), col))

    def main_spec(col):
        return spec(tq, lambda qi: qi, col)

    def prev_spec(col):
        return spec(HALF_KEYS, lambda qi: jnp.maximum(qi * halo_per_tile - 1, 0), col)

    def next_spec(col):
        return spec(HALF_KEYS, lambda qi: jnp.minimum((qi + 1) * halo_per_tile, last_halo), col)

    scratch = [pltpu.VMEM((tq + 2 * HALF_KEYS, GROUP_WIDTH), BF16),
               pltpu.VMEM((tq + 2 * HALF_KEYS, 2 * GROUP_WIDTH), BF16),
               pltpu.VMEM((HEADS_PER_GROUP, Q_SUB, K_SUB), F32)]
    if dilation > 1:
        scratch.append(pltpu.VMEM((HEADS_PER_GROUP, tq * dilation, LANES), F32))

    return pl.pallas_call(
        functools.partial(_attn_body, group=group, dilation=dilation, tq=tq, n_sub=n_sub),
        grid=(b, n_sub // tq, dilation),
        in_specs=[main_spec(Q_COL),
                  prev_spec(K_COL), main_spec(K_COL), next_spec(K_COL),
                  prev_spec(V_COL), main_spec(V_COL), next_spec(V_COL)],
        out_specs=[pl.BlockSpec((1, tq * dilation, GROUP_WIDTH), lambda bi, qi, r: (bi, qi, 0)),
                   pl.BlockSpec((1, tq * dilation, LANES), lambda bi, qi, r: (bi, qi, 0))],
        out_shape=[jax.ShapeDtypeStruct((b, s, GROUP_WIDTH), BF16),
                   jax.ShapeDtypeStruct((b, s, LANES), F32)],
        scratch_shapes=scratch,
        compiler_params=pltpu.CompilerParams(dimension_semantics=("arbitrary", "arbitrary", "arbitrary")),
        name=f"attn_d{dilation}",
    )(qkv, qkv, qkv, qkv, qkv, qkv, qkv)


def _lru_direction(xp_ref, xc_ref, xn_ref, h_ref, tile, n_tiles, rev, dirn,
                   cw_ref, cb_ref, wa_ref, ba_ref, wx_ref, bx_ref, lam_ref,
                   xpad, a_s, b_s, hpad, carry):
    tt, width = xc_ref.shape[1:]
    bw = width // LRU_BLOCKS
    n_chunks = width // LANES
    seg_len = tt // SEGMENTS
    pitch = seg_len + SEGMENTS
    sub = lax.broadcasted_iota(jnp.int32, (SEGMENTS, width), 0)

    xf = xc_ref[0].astype(F32)
    for c in range(n_chunks):
        for k in range(SEGMENTS):
            xpad[c, k * pitch:k * pitch + seg_len, :] = xf[k * seg_len:(k + 1) * seg_len, c * LANES:(c + 1) * LANES]
    xs = jnp.concatenate(
        [jnp.concatenate([xpad[c, pl.ds(j, SEGMENTS, stride=pitch), :] for j in range(seg_len)], axis=0)
         for c in range(n_chunks)], axis=1)

    halo_prev = jnp.where(tile == 0, 0.0, xp_ref[0, BF16_ROWS - CONV_LEFT:BF16_ROWS, :].astype(F32))
    halo_next = jnp.where(tile == n_tiles - 1, 0.0, xn_ref[0, 0:1, :].astype(F32))
    before1 = jnp.where(sub == 0, halo_prev[1:2], pltpu.roll(xs[tt - SEGMENTS:tt], 1, 0))
    before2 = jnp.where(sub == 0, halo_prev[0:1], pltpu.roll(xs[tt - 2 * SEGMENTS:tt - SEGMENTS], 1, 0))
    after1 = jnp.where(sub == SEGMENTS - 1, halo_next, pltpu.roll(xs[0:SEGMENTS], SEGMENTS - 1, 0))
    xm2 = jnp.concatenate([before2, before1, xs[:tt - 2 * SEGMENTS]], axis=0)
    xm1 = jnp.concatenate([before1, xs[:tt - SEGMENTS]], axis=0)
    xp1 = jnp.concatenate([xs[SEGMENTS:], after1], axis=0)
    xc = cb_ref[...] + xm2 * cw_ref[0:1] + xm1 * cw_ref[1:2] + xs * cw_ref[2:3] + xp1 * cw_ref[3:4]
    xcb = xc.astype(BF16)

    lam = lam_ref[dirn:dirn + 1]
    neg_lam = -lam
    softplus = jnp.maximum(neg_lam, 0.0) + jnp.log1p(jnp.exp(-jnp.abs(neg_lam)))
    coef = -RG_C * softplus

    sub_blk = lax.broadcasted_iota(jnp.int32, (SEGMENTS, bw), 0)
    if rev:
        start_rows = slice(tt - SEGMENTS, tt)
        first = (tile == n_tiles - 1) & (sub_blk == SEGMENTS - 1)
    else:
        start_rows = slice(0, SEGMENTS)
        first = (tile == 0) & (sub_blk == 0)

    for blk in range(LRU_BLOCKS):
        c0 = blk * bw
        xblk = xcb[:, c0:c0 + bw]
        pa = jnp.dot(xblk, wa_ref[dirn, blk], preferred_element_type=F32) + ba_ref[dirn:dirn + 1, c0:c0 + bw]
        px = jnp.dot(xblk, wx_ref[dirn, blk], preferred_element_type=F32) + bx_ref[dirn:dirn + 1, c0:c0 + bw]
        log_a = coef[:, c0:c0 + bw] * _sigmoid(pa)
        a = jnp.exp(log_a)
        u = jnp.tanh(log_a) * (-1.0 - a * a)
        mult = jnp.where(u > 0.0, u * lax.rsqrt(u), 0.0)
        gated = _sigmoid(px) * xc[:, c0:c0 + bw]
        b = mult * gated
        a_s[:, c0:c0 + bw] = a
        b_s[:, c0:c0 + bw] = b
        b_s[start_rows, c0:c0 + bw] = jnp.where(first, gated[start_rows], b[start_rows])

    steps = range(seg_len - 1, -1, -1) if rev else range(seg_len)
    segs = range(SEGMENTS - 1, -1, -1) if rev else range(SEGMENTS)
    sub1 = lax.broadcasted_iota(jnp.int32, (SEGMENTS, LANES), 0)
    for c in range(n_chunks):
        cs = slice(c * LANES, (c + 1) * LANES)
        prod = jnp.ones((SEGMENTS, LANES), F32)
        h = jnp.zeros((SEGMENTS, LANES), F32)
        for j in steps:
            aj = a_s[j * SEGMENTS:(j + 1) * SEGMENTS, cs]
            h = aj * h + b_s[j * SEGMENTS:(j + 1) * SEGMENTS, cs]
            prod = aj * prod
        state = carry[dirn:dirn + 1, cs]
        h0 = jnp.zeros((SEGMENTS, LANES), F32)
        for k in segs:
            h0 = jnp.where(sub1 == k, state, h0)
            state = prod[k:k + 1] * state + h[k:k + 1]
        carry[dirn:dirn + 1, cs] = state
        h = h0
        for j in steps:
            h = a_s[j * SEGMENTS:(j + 1) * SEGMENTS, cs] * h + b_s[j * SEGMENTS:(j + 1) * SEGMENTS, cs]
            hpad[c, pl.ds(j, SEGMENTS, stride=pitch), :] = h
        for k in range(SEGMENTS):
            h_ref[0, k * seg_len:(k + 1) * seg_len, cs] = hpad[c, k * pitch:k * pitch + seg_len, :].astype(h_ref.dtype)


def _lru_body(xfp_ref, xfc_ref, xfn_ref, xbp_ref, xbc_ref, xbn_ref,
              cw_ref, cb_ref, wa_ref, ba_ref, wx_ref, bx_ref, lam_ref,
              hf_ref, hb_ref, xpad_f, a_f, b_f, hpad_f, xpad_b, a_b, b_b, hpad_b, carry):
    i = pl.program_id(1)
    n_tiles = pl.num_programs(1)

    @pl.when(i == 0)
    def _():
        carry[...] = jnp.zeros_like(carry)

    params = (cw_ref, cb_ref, wa_ref, ba_ref, wx_ref, bx_ref, lam_ref)
    _lru_direction(xfp_ref, xfc_ref, xfn_ref, hf_ref, i, n_tiles, False, 0, *params,
                   xpad_f, a_f, b_f, hpad_f, carry)
    _lru_direction(xbp_ref, xbc_ref, xbn_ref, hb_ref, n_tiles - 1 - i, n_tiles, True, 1, *params,
                   xpad_b, a_b, b_b, hpad_b, carry)


def _lru(nat, conv_w, conv_b, w_a, b_a, w_x, b_x, lam):
    b, s, _ = nat.shape
    width = conv_w.shape[-1]
    tt = LRU_ROWS
    n_tiles = s // tt
    halo_per_tile = tt // BF16_ROWS
    last_halo = s // BF16_ROWS - 1

    def tile_of(i, rev):
        return n_tiles - 1 - i if rev else i

    def specs(rev):
        return [
            pl.BlockSpec((1, BF16_ROWS, width),
                         lambda bi, i: (bi, jnp.maximum(tile_of(i, rev) * halo_per_tile - 1, 0), XLRU_COL)),
            pl.BlockSpec((1, tt, width), lambda bi, i: (bi, tile_of(i, rev), XLRU_COL)),
            pl.BlockSpec((1, BF16_ROWS, width),
                         lambda bi, i: (bi, jnp.minimum((tile_of(i, rev) + 1) * halo_per_tile, last_halo), XLRU_COL)),
        ]

    consts = (conv_w, conv_b.reshape(1, width), w_a, b_a, w_x, b_x, lam)
    padded_rows = tt + SEGMENTS * SEGMENTS
    per_direction = [pltpu.VMEM((width // LANES, padded_rows, LANES), F32),
                     pltpu.VMEM((tt, width), F32),
                     pltpu.VMEM((tt, width), F32),
                     pltpu.VMEM((width // LANES, padded_rows, LANES), F32)]
    return pl.pallas_call(
        _lru_body,
        grid=(b, n_tiles),
        in_specs=specs(False) + specs(True) + [_resident(a) for a in consts],
        out_specs=[pl.BlockSpec((1, tt, width), lambda bi, i: (bi, i, 0)),
                   pl.BlockSpec((1, tt, width), lambda bi, i: (bi, n_tiles - 1 - i, 0))],
        out_shape=[jax.ShapeDtypeStruct((b, s, width), BF16)] * 2,
        scratch_shapes=per_direction + per_direction + [pltpu.VMEM((F32_ROWS, width), F32)],
        compiler_params=pltpu.CompilerParams(dimension_semantics=("parallel", "arbitrary")),
        name="lru",
    )(nat, nat, nat, nat, nat, nat, *consts)


def _merge_body(o0_ref, o1_ref, o2_ref, l0_ref, l1_ref, l2_ref, gatt_ref, hf_ref, hb_ref, glru_ref,
                za_ref, zl_ref, x_ref, bg_ref, woa_ref, wol_ref, wout_ref, fg_ref, out_ref, *, final):
    tm, dm = x_ref.shape
    o_refs = (o0_ref, o1_ref, o2_ref)
    for r0 in range(0, tm, MERGE_SUB_ROWS):
        rows = slice(r0, r0 + MERGE_SUB_ROWS)
        l0, l1, l2 = l0_ref[rows], l1_ref[rows], l2_ref[rows]
        mx = jnp.maximum(jnp.maximum(l0, l1), l2)
        e0, e1, e2 = jnp.exp(l0 - mx), jnp.exp(l1 - mx), jnp.exp(l2 - mx)
        inv = 1.0 / (e0 + e1 + e2)
        alphas = (e0 * inv, e1 * inv, e2 * inv)

        heads = []
        for h in range(HEADS_PER_GROUP):
            c0 = h * HEAD_DIM
            acc = jnp.zeros((MERGE_SUB_ROWS, HEAD_DIM), F32)
            for g in range(N_GROUPS):
                w = jnp.broadcast_to(alphas[g][:, h * LSE_REP:h * LSE_REP + 1], (MERGE_SUB_ROWS, HEAD_DIM))
                acc = acc + w * o_refs[g][rows, c0:c0 + HEAD_DIM].astype(F32)
            heads.append(acc)
        mixed = jnp.concatenate(heads, axis=-1)

        gatt = gatt_ref[rows].astype(F32)
        y_att = (mixed * (gatt * _sigmoid(gatt))).astype(BF16)
        glru = glru_ref[rows].astype(F32)
        h_sum = hf_ref[rows].astype(F32) + hb_ref[rows].astype(F32)
        y_lru = (h_sum * (glru * _sigmoid(glru))).astype(BF16)

        p_att = jnp.dot(y_att, woa_ref[...], preferred_element_type=F32)
        p_lru = jnp.dot(y_lru, wol_ref[...], preferred_element_type=F32)
        gate_att = _sigmoid(za_ref[rows].astype(F32) + bg_ref[:, 0:dm])
        gate_lru = _sigmoid(zl_ref[rows].astype(F32) + bg_ref[:, dm:2 * dm])
        merged = (gate_att * p_att + gate_lru * p_lru).astype(BF16)
        y = x_ref[rows] + jnp.dot(merged, wout_ref[...], preferred_element_type=F32)
        if final:
            y = _rms_norm(y, fg_ref[...])
        out_ref[rows] = y


def _merge(x2d, nat2d, outs, lses, h_f, h_b, b_gate, w_o_att, w_o_lru, w_out, final_g, final):
    t, dm = x2d.shape
    tm = MERGE_ROWS

    def rows(width, col=0):
        return pl.BlockSpec((tm, width), lambda i: (i, col))

    consts = (b_gate.reshape(1, 2 * dm), w_o_att, w_o_lru, w_out, final_g.reshape(1, dm))
    return pl.pallas_call(
        functools.partial(_merge_body, final=final),
        grid=(t // tm,),
        in_specs=[rows(GROUP_WIDTH)] * 3 + [rows(LANES)] * 3
                 + [rows(GROUP_WIDTH, GATT_COL), rows(dm), rows(dm), rows(dm, GLRU_COL),
                    rows(dm, LOGIT_ATT_COL), rows(dm, LOGIT_LRU_COL), rows(dm)]
                 + [_resident(a) for a in consts],
        out_specs=rows(dm),
        out_shape=jax.ShapeDtypeStruct((t, dm), F32),
        compiler_params=pltpu.CompilerParams(dimension_semantics=("parallel",)),
        name="merge_final" if final else "merge",
    )(*outs, *lses, nat2d, h_f, h_b, nat2d, nat2d, nat2d, x2d, *consts)


def _split_projection_weight(w):
    gw = GROUP_WIDTH
    qkv = [w[:, i * gw:(i + 1) * gw] for i in range(3 * N_GROUPS)]
    rest = w[:, 3 * N_GROUPS * gw:]
    ordered = [qkv[0], qkv[3], qkv[6], rest] + [qkv[g + N_GROUPS * i] for g in (1, 2) for i in range(3)]
    return jnp.concatenate(ordered, axis=1).astype(BF16)


def kernel(x, norm_g, w_in, b_gate, conv_w, conv_b, rg_w_a, rg_b_a, rg_w_x, rg_b_x, rg_lam,
           w_o_att, w_o_lru, w_out, final_g):
    b, s, dm = x.shape
    depth = w_in.shape[0]
    t = b * s
    x2d = x.reshape(t, dm)
    for l in range(depth):
        nat2d, qkv4, qkv16 = _norm_proj(x2d, norm_g[l].reshape(1, dm), _split_projection_weight(w_in[l]), b)
        nat = nat2d.reshape(b, s, -1)
        outs, lses = [], []
        for group, qkv in enumerate((nat, qkv4, qkv16)):
            o, lse = _attention_group(qkv, group, ATT_PATTERNS[group][1])
            outs.append(o.reshape(t, GROUP_WIDTH))
            lses.append(lse.reshape(t, LANES))
        h_f, h_b = _lru(nat, conv_w[l], conv_b[l], rg_w_a[l].astype(BF16), rg_b_a[l],
                        rg_w_x[l].astype(BF16), rg_b_x[l], rg_lam[l])
        x2d = _merge(x2d, nat2d, outs, lses, h_f.reshape(t, dm), h_b.reshape(t, dm), b_gate[l],
                     w_o_att[l].astype(BF16), w_o_lru[l].astype(BF16), w_out[l].astype(BF16),
                     final_g, final=(l == depth - 1))
    return x2d.reshape(b, s, dm)
```

```python
import functools

import jax
import jax.numpy as jnp
from jax import lax
from jax.experimental import pallas as pl
from jax.experimental.pallas import tpu as pltpu

F32 = jnp.float32
BF16 = jnp.bfloat16

HEAD_DIM = 128
HEADS_PER_GROUP = 4
ATT_PATTERNS = ((128, 1), (512, 4), (2048, 16))
N_GROUPS = len(ATT_PATTERNS)
N_ATT_HEADS = N_GROUPS * HEADS_PER_GROUP
GROUP_WIDTH = HEADS_PER_GROUP * HEAD_DIM
QKV_WIDTH = 3 * GROUP_WIDTH
NEG_INF = -1e30
LRU_BLOCKS = 4
RG_C = 8.0
NORM_EPS = 1e-6
CONV_LEFT = 2

XLRU_COL = 0
NAT_QKV_COL = 2

LANES = 128
F32_ROWS = 8
BF16_ROWS = 16
HALF_KEYS = 64
Q_SUB = 128
K_SUB = Q_SUB + 2 * HALF_KEYS
LSE_REP = LANES // HEADS_PER_GROUP
SEGMENTS = F32_ROWS

PROJ_ROWS = 512
PROJ_COLS = 512
ATTN_Q_ROWS = {1: 1024, 4: 512, 16: 256}
LRU_ROWS = 512
MERGE_ROWS = 512
MERGE_SUB_ROWS = 256

LOG2_E = 1.4426950408889634
LN_2 = 0.6931471805599453


def _sigmoid(x):
    return 1.0 / (1.0 + jnp.exp2(x * -LOG2_E))


def _rms_norm(x, g):
    ms = jnp.mean(x * x, axis=-1, keepdims=True)
    return x * lax.rsqrt(ms + NORM_EPS) * g


def _resident(a):
    return pl.BlockSpec(a.shape, lambda *_: (0,) * a.ndim, pipeline_mode=pl.Buffered(1))


def _norm_proj_body(x_ref, g_ref, w_ref, nat_ref, s4_ref, s16_ref, xn_ref, xn4_ref, xn16_ref, slab_ref):
    tm, dm = x_ref.shape
    n_nat = nat_ref.shape[1]

    y = _rms_norm(x_ref[...], g_ref[...])
    xn_ref[...] = y.astype(BF16)
    for c in range(dm // LANES):
        slab_ref[c] = y[:, c * LANES:(c + 1) * LANES]
    for ref, d in ((xn4_ref, 4), (xn16_ref, 16)):
        rows = tm // d
        for c in range(dm // LANES):
            for r in range(d):
                ref[r * rows:(r + 1) * rows, c * LANES:(c + 1) * LANES] = (
                    slab_ref[c, pl.ds(r, rows, stride=d), :].astype(BF16))

    def project(lhs_ref, col):
        return jnp.dot(lhs_ref[...], w_ref[:, col:col + PROJ_COLS], preferred_element_type=F32).astype(BF16)

    for n0 in range(0, n_nat, PROJ_COLS):
        nat_ref[:, n0:n0 + PROJ_COLS] = project(xn_ref, n0)
    for n0 in range(0, QKV_WIDTH, PROJ_COLS):
        s4_ref[0, :, :, n0:n0 + PROJ_COLS] = project(xn4_ref, n_nat + n0).reshape(4, tm // 4, PROJ_COLS)
        s16_ref[0, :, :, n0:n0 + PROJ_COLS] = (
            project(xn16_ref, n_nat + QKV_WIDTH + n0).reshape(16, tm // 16, PROJ_COLS))


def _norm_proj(x2d, g, w, batch):
    t, dm = x2d.shape
    s = t // batch
    n_nat = w.shape[1] - 2 * QKV_WIDTH
    tm = PROJ_ROWS
    tiles_per_b = s // tm

    def strided_spec(d):
        return pl.BlockSpec((1, d, tm // d, QKV_WIDTH), lambda i: (i // tiles_per_b, 0, i % tiles_per_b, 0))

    return pl.pallas_call(
        _norm_proj_body,
        grid=(t // tm,),
        in_specs=[pl.BlockSpec((tm, dm), lambda i: (i, 0)), _resident(g), _resident(w)],
        out_specs=[pl.BlockSpec((tm, n_nat), lambda i: (i, 0)), strided_spec(4), strided_spec(16)],
        out_shape=[jax.ShapeDtypeStruct((t, n_nat), BF16),
                   jax.ShapeDtypeStruct((batch, 4, s // 4, QKV_WIDTH), BF16),
                   jax.ShapeDtypeStruct((batch, 16, s // 16, QKV_WIDTH), BF16)],
        scratch_shapes=[pltpu.VMEM((tm, dm), BF16), pltpu.VMEM((tm, dm), BF16), pltpu.VMEM((tm, dm), BF16),
                        pltpu.VMEM((dm // LANES, tm, LANES), F32)],
        compiler_params=pltpu.CompilerParams(dimension_semantics=("parallel",)),
        name="norm_proj",
    )(x2d, g, w)


def _attn_body(q_ref, kp_ref, kc_ref, kn_ref, vp_ref, vc_ref, vn_ref, o_ref, lse_ref,
               kcat, vcat, bias, *o_slab, group, dilation, tq, n_sub):
    qi = pl.program_id(1)
    r = pl.program_id(2)
    n_q = pl.num_programs(1)
    blk = (0,) if dilation == 1 else (0, 0)
    kcol = lax.broadcasted_iota(jnp.int32, (Q_SUB, K_SUB), 1)
    slopes = [2.0 ** (-8.0 * (group * HEADS_PER_GROUP + h + 1) / N_ATT_HEADS) for h in range(HEADS_PER_GROUP)]

    @pl.when((pl.program_id(0) == 0) & (qi == 0) & (r == 0))
    def _():
        qrow = lax.broadcasted_iota(jnp.int32, (Q_SUB, K_SUB), 0)
        absd = jnp.abs(kcol - HALF_KEYS - qrow)
        dist = (absd * dilation).astype(F32)
        for h in range(HEADS_PER_GROUP):
            bias[h] = jnp.where(absd <= HALF_KEYS, (-slopes[h] * LOG2_E) * dist, NEG_INF)
            vcat[:, (2 * h + 1) * HEAD_DIM:(2 * h + 2) * HEAD_DIM] = jnp.ones((tq + 2 * HALF_KEYS, HEAD_DIM), BF16)

    kcat[0:HALF_KEYS] = kp_ref[blk]
    kcat[HALF_KEYS:HALF_KEYS + tq] = kc_ref[blk]
    kcat[HALF_KEYS + tq:] = kn_ref[blk]
    for h in range(HEADS_PER_GROUP):
        src = slice(h * HEAD_DIM, (h + 1) * HEAD_DIM)
        dst = slice(2 * h * HEAD_DIM, (2 * h + 1) * HEAD_DIM)
        vcat[0:HALF_KEYS, dst] = vp_ref[blk + (slice(None), src)]
        vcat[HALF_KEYS:HALF_KEYS + tq, dst] = vc_ref[blk + (slice(None), src)]
        vcat[HALF_KEYS + tq:, dst] = vn_ref[blk + (slice(None), src)]

    lane = lax.broadcasted_iota(jnp.int32, (Q_SUB, LANES), 1)
    n_sub_tiles = tq // Q_SUB
    for sub in range(n_sub_tiles):
        r0 = sub * Q_SUB
        lse_tile = jnp.zeros((Q_SUB, LANES), F32)
        for h in range(HEADS_PER_GROUP):
            c0 = h * HEAD_DIM
            q = q_ref[blk + (slice(r0, r0 + Q_SUB), slice(c0, c0 + HEAD_DIM))]
            k = kcat[r0:r0 + K_SUB, c0:c0 + HEAD_DIM]
            raw = lax.dot_general(q, k, (((1,), (1,)), ((), ())), preferred_element_type=F32)
            s2 = raw * (HEAD_DIM ** -0.5 * LOG2_E) + bias[h]
            if sub == 0:
                s2 = jnp.where((qi > 0) | (kcol >= HALF_KEYS), s2, NEG_INF)
            if sub == n_sub_tiles - 1:
                s2 = jnp.where((qi < n_q - 1) | (kcol < HALF_KEYS + Q_SUB), s2, NEG_INF)
            m2 = jnp.max(s2, axis=-1, keepdims=True)
            p = jnp.exp2(s2 - m2).astype(BF16)
            acc = jnp.dot(p, vcat[r0:r0 + K_SUB, 2 * c0:2 * c0 + 2 * HEAD_DIM], preferred_element_type=F32)
            den = acc[:, HEAD_DIM:]
            o = acc[:, :HEAD_DIM] / den
            if dilation == 1:
                o_ref[0, r0:r0 + Q_SUB, c0:c0 + HEAD_DIM] = o.astype(BF16)
            else:
                o_slab[0][h, pl.ds(r0 * dilation + r, Q_SUB, stride=dilation), :] = o
            lse = (m2 + jnp.log2(den)) * LN_2
            lse_tile = jnp.where((lane >= h * LSE_REP) & (lane < (h + 1) * LSE_REP), lse, lse_tile)
        if dilation == 1:
            lse_ref[0, r0:r0 + Q_SUB, :] = lse_tile
        else:
            lse_ref[0, pl.ds(r0 * dilation + r, Q_SUB, stride=dilation), :] = lse_tile

    if dilation > 1:
        @pl.when(r == dilation - 1)
        def _():
            for h in range(HEADS_PER_GROUP):
                o_ref[0, :, h * HEAD_DIM:(h + 1) * HEAD_DIM] = o_slab[0][h].astype(BF16)


def _attention_group(qkv, group, dilation):
    if dilation == 1:
        b, s, _ = qkv.shape
    else:
        b, _, n_sub, _ = qkv.shape
        s = n_sub * dilation
    n_sub = s // dilation
    tq = min(ATTN_Q_ROWS[dilation], n_sub)
    q_col = NAT_QKV_COL if dilation == 1 else 0
    halo_per_tile = tq // HALF_KEYS
    last_halo = n_sub // HALF_KEYS - 1

    def spec(rows, row_index, col):
        if dilation == 1:
            return pl.BlockSpec((1, rows, GROUP_WIDTH), lambda bi, qi, r: (bi, row_index(qi), col))
        return pl.BlockSpec((1, 1, rows, GROUP_WIDTH), lambda bi, qi, r: (bi, r, row_index(qi), col))

    def main_spec(col):
        return spec(tq, lambda qi: qi, col)

    def prev_spec(col):
        return spec(HALF_KEYS, lambda qi: jnp.maximum(qi * halo_per_tile - 1, 0), col)

    def next_spec(col):
        return spec(HALF_KEYS, lambda qi: jnp.minimum((qi + 1) * halo_per_tile, last_halo), col)

    scratch = [pltpu.VMEM((tq + 2 * HALF_KEYS, GROUP_WIDTH), BF16),
               pltpu.VMEM((tq + 2 * HALF_KEYS, 2 * GROUP_WIDTH), BF16),
               pltpu.VMEM((HEADS_PER_GROUP, Q_SUB, K_SUB), F32)]
    if dilation > 1:
        scratch.append(pltpu.VMEM((HEADS_PER_GROUP, tq * dilation, LANES), F32))

    return pl.pallas_call(
        functools.partial(_attn_body, group=group, dilation=dilation, tq=tq, n_sub=n_sub),
        grid=(b, n_sub // tq, dilation),
        in_specs=[main_spec(q_col),
                  prev_spec(q_col + 1), main_spec(q_col + 1), next_spec(q_col + 1),
                  prev_spec(q_col + 2), main_spec(q_col + 2), next_spec(q_col + 2)],
        out_specs=[pl.BlockSpec((1, tq * dilation, GROUP_WIDTH), lambda bi, qi, r: (bi, qi, 0)),
                   pl.BlockSpec((1, tq * dilation, LANES), lambda bi, qi, r: (bi, qi, 0))],
        out_shape=[jax.ShapeDtypeStruct((b, s, GROUP_WIDTH), BF16),
                   jax.ShapeDtypeStruct((b, s, LANES), F32)],
        scratch_shapes=scratch,
        compiler_params=pltpu.CompilerParams(dimension_semantics=("arbitrary", "arbitrary", "arbitrary")),
        name=f"attn_d{dilation}",
    )(qkv, qkv, qkv, qkv, qkv, qkv, qkv)


def _lru_direction(xp_ref, xc_ref, xn_ref, h_ref, tile, n_tiles, rev, dirn,
                   cw_ref, cb_ref, wa_ref, ba_ref, wx_ref, bx_ref, lam_ref,
                   xpad, a_s, b_s, hpad, carry):
    tt, width = xc_ref.shape[1:]
    bw = width // LRU_BLOCKS
    n_chunks = width // LANES
    seg_len = tt // SEGMENTS
    pitch = seg_len + SEGMENTS
    sub = lax.broadcasted_iota(jnp.int32, (SEGMENTS, width), 0)

    xf = xc_ref[0].astype(F32)
    for c in range(n_chunks):
        for k in range(SEGMENTS):
            xpad[c, k * pitch:k * pitch + seg_len, :] = xf[k * seg_len:(k + 1) * seg_len, c * LANES:(c + 1) * LANES]
    xs = jnp.concatenate(
        [jnp.concatenate([xpad[c, pl.ds(j, SEGMENTS, stride=pitch), :] for j in range(seg_len)], axis=0)
         for c in range(n_chunks)], axis=1)

    halo_prev = jnp.where(tile == 0, 0.0, xp_ref[0, BF16_ROWS - CONV_LEFT:BF16_ROWS, :].astype(F32))
    halo_next = jnp.where(tile == n_tiles - 1, 0.0, xn_ref[0, 0:1, :].astype(F32))
    before1 = jnp.where(sub == 0, halo_prev[1:2], pltpu.roll(xs[tt - SEGMENTS:tt], 1, 0))
    before2 = jnp.where(sub == 0, halo_prev[0:1], pltpu.roll(xs[tt - 2 * SEGMENTS:tt - SEGMENTS], 1, 0))
    after1 = jnp.where(sub == SEGMENTS - 1, halo_next, pltpu.roll(xs[0:SEGMENTS], SEGMENTS - 1, 0))
    xm2 = jnp.concatenate([before2, before1, xs[:tt - 2 * SEGMENTS]], axis=0)
    xm1 = jnp.concatenate([before1, xs[:tt - SEGMENTS]], axis=0)
    xp1 = jnp.concatenate([xs[SEGMENTS:], after1], axis=0)
    xc = cb_ref[...] + xm2 * cw_ref[0:1] + xm1 * cw_ref[1:2] + xs * cw_ref[2:3] + xp1 * cw_ref[3:4]
    xcb = xc.astype(BF16)

    lam = lam_ref[dirn:dirn + 1]
    neg_lam = -lam
    softplus = jnp.maximum(neg_lam, 0.0) + jnp.log1p(jnp.exp(-jnp.abs(neg_lam)))
    coef = -RG_C * softplus

    sub_blk = lax.broadcasted_iota(jnp.int32, (SEGMENTS, bw), 0)
    if rev:
        start_rows = slice(tt - SEGMENTS, tt)
        first = (tile == n_tiles - 1) & (sub_blk == SEGMENTS - 1)
    else:
        start_rows = slice(0, SEGMENTS)
        first = (tile == 0) & (sub_blk == 0)

    for blk in range(LRU_BLOCKS):
        c0 = blk * bw
        xblk = xcb[:, c0:c0 + bw]
        pa = jnp.dot(xblk, wa_ref[dirn, blk], preferred_element_type=F32) + ba_ref[dirn:dirn + 1, c0:c0 + bw]
        px = jnp.dot(xblk, wx_ref[dirn, blk], preferred_element_type=F32) + bx_ref[dirn:dirn + 1, c0:c0 + bw]
        log_a = coef[:, c0:c0 + bw] * _sigmoid(pa)
        a = jnp.exp(log_a)
        u = jnp.tanh(log_a) * (-1.0 - a * a)
        mult = jnp.where(u > 0.0, u * lax.rsqrt(u), 0.0)
        gated = _sigmoid(px) * xc[:, c0:c0 + bw]
        b = mult * gated
        a_s[:, c0:c0 + bw] = a
        b_s[:, c0:c0 + bw] = b
        b_s[start_rows, c0:c0 + bw] = jnp.where(first, gated[start_rows], b[start_rows])

    steps = range(seg_len - 1, -1, -1) if rev else range(seg_len)
    segs = range(SEGMENTS - 1, -1, -1) if rev else range(SEGMENTS)
    sub1 = lax.broadcasted_iota(jnp.int32, (SEGMENTS, LANES), 0)
    for c in range(n_chunks):
        cs = slice(c * LANES, (c + 1) * LANES)
        prod = jnp.ones((SEGMENTS, LANES), F32)
        h = jnp.zeros((SEGMENTS, LANES), F32)
        for j in steps:
            aj = a_s[j * SEGMENTS:(j + 1) * SEGMENTS, cs]
            h = aj * h + b_s[j * SEGMENTS:(j + 1) * SEGMENTS, cs]
            prod = aj * prod
        state = carry[dirn:dirn + 1, cs]
        h0 = jnp.zeros((SEGMENTS, LANES), F32)
        for k in segs:
            h0 = jnp.where(sub1 == k, state, h0)
            state = prod[k:k + 1] * state + h[k:k + 1]
        carry[dirn:dirn + 1, cs] = state
        h = h0
        for j in steps:
            h = a_s[j * SEGMENTS:(j + 1) * SEGMENTS, cs] * h + b_s[j * SEGMENTS:(j + 1) * SEGMENTS, cs]
            hpad[c, pl.ds(j, SEGMENTS, stride=pitch), :] = h
        for k in range(SEGMENTS):
            h_ref[0, k * seg_len:(k + 1) * seg_len, cs] = hpad[c, k * pitch:k * pitch + seg_len, :].astype(h_ref.dtype)


def _lru_body(xfp_ref, xfc_ref, xfn_ref, xbp_ref, xbc_ref, xbn_ref,
              cw_ref, cb_ref, wa_ref, ba_ref, wx_ref, bx_ref, lam_ref,
              hf_ref, hb_ref, xpad_f, a_f, b_f, hpad_f, xpad_b, a_b, b_b, hpad_b, carry):
    i = pl.program_id(1)
    n_tiles = pl.num_programs(1)

    @pl.when(i == 0)
    def _():
        carry[...] = jnp.zeros_like(carry)

    params = (cw_ref, cb_ref, wa_ref, ba_ref, wx_ref, bx_ref, lam_ref)
    _lru_direction(xfp_ref, xfc_ref, xfn_ref, hf_ref, i, n_tiles, False, 0, *params,
                   xpad_f, a_f, b_f, hpad_f, carry)
    _lru_direction(xbp_ref, xbc_ref, xbn_ref, hb_ref, n_tiles - 1 - i, n_tiles, True, 1, *params,
                   xpad_b, a_b, b_b, hpad_b, carry)


def _lru(nat, conv_w, conv_b, w_a, b_a, w_x, b_x, lam):
    b, s, _ = nat.shape
    width = conv_w.shape[-1]
    tt = LRU_ROWS
    n_tiles = s // tt
    halo_per_tile = tt // BF16_ROWS
    last_halo = s // BF16_ROWS - 1

    def tile_of(i, rev):
        return n_tiles - 1 - i if rev else i

    def specs(rev):
        return [
            pl.BlockSpec((1, BF16_ROWS, width),
                         lambda bi, i: (bi, jnp.maximum(tile_of(i, rev) * halo_per_tile - 1, 0), XLRU_COL)),
            pl.BlockSpec((1, tt, width), lambda bi, i: (bi, tile_of(i, rev), XLRU_COL)),
            pl.BlockSpec((1, BF16_ROWS, width),
                         lambda bi, i: (bi, jnp.minimum((tile_of(i, rev) + 1) * halo_per_tile, last_halo), XLRU_COL)),
        ]

    consts = (conv_w, conv_b.reshape(1, width), w_a, b_a, w_x, b_x, lam)
    padded_rows = tt + SEGMENTS * SEGMENTS
    per_direction = [pltpu.VMEM((width // LANES, padded_rows, LANES), F32),
                     pltpu.VMEM((tt, width), F32),
                     pltpu.VMEM((tt, width), F32),
                     pltpu.VMEM((width // LANES, padded_rows, LANES), F32)]
    return pl.pallas_call(
        _lru_body,
        grid=(b, n_tiles),
        in_specs=specs(False) + specs(True) + [_resident(a) for a in consts],
        out_specs=[pl.BlockSpec((1, tt, width), lambda bi, i: (bi, i, 0)),
                   pl.BlockSpec((1, tt, width), lambda bi, i: (bi, n_tiles - 1 - i, 0))],
        out_shape=[jax.ShapeDtypeStruct((b, s, width), BF16)] * 2,
        scratch_shapes=per_direction + per_direction + [pltpu.VMEM((F32_ROWS, width), F32)],
        compiler_params=pltpu.CompilerParams(dimension_semantics=("parallel", "arbitrary")),
        name="lru",
    )(nat, nat, nat, nat, nat, nat, *consts)


def _merge_body(o0_ref, o1_ref, o2_ref, l0_ref, l1_ref, l2_ref, hf_ref, hb_ref, x_ref,
                ng_ref, wg_ref, bg_ref, woa_ref, wol_ref, wout_ref, fg_ref, out_ref, *, final):
    tm, dm = x_ref.shape
    o_refs = (o0_ref, o1_ref, o2_ref)
    for r0 in range(0, tm, MERGE_SUB_ROWS):
        rows = slice(r0, r0 + MERGE_SUB_ROWS)
        xn = _rms_norm(x_ref[rows], ng_ref[...]).astype(BF16)

        def gate_proj(c0, width):
            return jnp.dot(xn, wg_ref[:, c0:c0 + width], preferred_element_type=F32)

        gatt = gate_proj(0, GROUP_WIDTH)
        glru = gate_proj(GROUP_WIDTH, dm)
        logit_att = gate_proj(GROUP_WIDTH + dm, dm)
        logit_lru = gate_proj(GROUP_WIDTH + 2 * dm, dm)

        l0, l1, l2 = l0_ref[rows], l1_ref[rows], l2_ref[rows]
        mx = jnp.maximum(jnp.maximum(l0, l1), l2)
        e0, e1, e2 = jnp.exp(l0 - mx), jnp.exp(l1 - mx), jnp.exp(l2 - mx)
        inv = 1.0 / (e0 + e1 + e2)
        alphas = (e0 * inv, e1 * inv, e2 * inv)

        heads = []
        for h in range(HEADS_PER_GROUP):
            c0 = h * HEAD_DIM
            acc = jnp.zeros((MERGE_SUB_ROWS, HEAD_DIM), F32)
            for g in range(N_GROUPS):
                w = jnp.broadcast_to(alphas[g][:, h * LSE_REP:h * LSE_REP + 1], (MERGE_SUB_ROWS, HEAD_DIM))
                acc = acc + w * o_refs[g][rows, c0:c0 + HEAD_DIM].astype(F32)
            heads.append(acc)
        mixed = jnp.concatenate(heads, axis=-1)

        y_att = (mixed * (gatt * _sigmoid(gatt))).astype(BF16)
        h_sum = hf_ref[rows].astype(F32) + hb_ref[rows].astype(F32)
        y_lru = (h_sum * (glru * _sigmoid(glru))).astype(BF16)

        p_att = jnp.dot(y_att, woa_ref[...], preferred_element_type=F32)
        p_lru = jnp.dot(y_lru, wol_ref[...], preferred_element_type=F32)
        gate_att = _sigmoid(logit_att + bg_ref[:, 0:dm])
        gate_lru = _sigmoid(logit_lru + bg_ref[:, dm:2 * dm])
        merged = (gate_att * p_att + gate_lru * p_lru).astype(BF16)
        y = x_ref[rows] + jnp.dot(merged, wout_ref[...], preferred_element_type=F32)
        if final:
            y = _rms_norm(y, fg_ref[...])
        out_ref[rows] = y


def _merge(x2d, outs, lses, h_f, h_b, norm_g, w_gates, b_gate, w_o_att, w_o_lru, w_out, final_g, final):
    t, dm = x2d.shape
    tm = MERGE_ROWS

    def rows(width):
        return pl.BlockSpec((tm, width), lambda i: (i, 0))

    consts = (norm_g, w_gates, b_gate.reshape(1, 2 * dm), w_o_att, w_o_lru, w_out, final_g.reshape(1, dm))
    return pl.pallas_call(
        functools.partial(_merge_body, final=final),
        grid=(t // tm,),
        in_specs=[rows(GROUP_WIDTH)] * 3 + [rows(LANES)] * 3 + [rows(dm)] * 3 + [_resident(a) for a in consts],
        out_specs=rows(dm),
        out_shape=jax.ShapeDtypeStruct((t, dm), F32),
        compiler_params=pltpu.CompilerParams(dimension_semantics=("parallel",)),
        name="merge_final" if final else "merge",
    )(*outs, *lses, h_f, h_b, x2d, *consts)


def _split_projection_weight(w):
    gw = GROUP_WIDTH
    dm = w.shape[0]
    qkv = [w[:, i * gw:(i + 1) * gw] for i in range(3 * N_GROUPS)]
    rest = w[:, 3 * N_GROUPS * gw:]
    att_gate, x_lru, others = rest[:, :gw], rest[:, gw:gw + dm], rest[:, gw + dm:]
    w_proj = jnp.concatenate([x_lru] + [qkv[g + N_GROUPS * i] for g in range(N_GROUPS) for i in range(3)], axis=1)
    w_gates = jnp.concatenate([att_gate, others], axis=1)
    return w_proj.astype(BF16), w_gates.astype(BF16)


def kernel(x, norm_g, w_in, b_gate, conv_w, conv_b, rg_w_a, rg_b_a, rg_w_x, rg_b_x, rg_lam,
           w_o_att, w_o_lru, w_out, final_g):
    b, s, dm = x.shape
    depth = w_in.shape[0]
    t = b * s
    x2d = x.reshape(t, dm)
    for l in range(depth):
        g = norm_g[l].reshape(1, dm)
        w_proj, w_gates = _split_projection_weight(w_in[l])
        nat2d, qkv4, qkv16 = _norm_proj(x2d, g, w_proj, b)
        nat = nat2d.reshape(b, s, -1)
        outs, lses = [], []
        for group, qkv in enumerate((nat, qkv4, qkv16)):
            o, lse = _attention_group(qkv, group, ATT_PATTERNS[group][1])
            outs.append(o.reshape(t, GROUP_WIDTH))
            lses.append(lse.reshape(t, LANES))
        h_f, h_b = _lru(nat, conv_w[l], conv_b[l], rg_w_a[l].astype(BF16), rg_b_a[l],
                        rg_w_x[l].astype(BF16), rg_b_x[l], rg_lam[l])
        x2d = _merge(x2d, outs, lses, h_f.reshape(t, dm), h_b.reshape(t, dm), g, w_gates, b_gate[l],
                     w_o_att[l].astype(BF16), w_o_lru[l].astype(BF16), w_out[l].astype(BF16),
                     final_g, final=(l == depth - 1))
    return x2d.reshape(b, s, dm)
```

```python
import functools

import jax
import jax.numpy as jnp
from jax import lax
from jax.experimental import pallas as pl
from jax.experimental.pallas import tpu as pltpu

F32 = jnp.float32
BF16 = jnp.bfloat16

HEAD_DIM = 128
HEADS_PER_GROUP = 4
ATT_PATTERNS = ((128, 1), (512, 4), (2048, 16))
N_GROUPS = len(ATT_PATTERNS)
N_ATT_HEADS = N_GROUPS * HEADS_PER_GROUP
GROUP_WIDTH = HEADS_PER_GROUP * HEAD_DIM
QKV_WIDTH = 3 * GROUP_WIDTH
NEG_INF = -1e30
LRU_BLOCKS = 4
RG_C = 8.0
NORM_EPS = 1e-6
CONV_LEFT = 2

XLRU_COL = 0
NAT_QKV_COL = 2

LANES = 128
F32_ROWS = 8
BF16_ROWS = 16
HALF_KEYS = 64
Q_SUB = 128
K_SUB = Q_SUB + 2 * HALF_KEYS
LSE_REP = LANES // HEADS_PER_GROUP
SEGMENTS = F32_ROWS

PROJ_ROWS = 512
PROJ_COLS = 512
ATTN_Q_ROWS = {1: 1024, 4: 512, 16: 256}
LRU_ROWS = 512
MERGE_ROWS = 512
MERGE_SUB_ROWS = 256

LOG2_E = 1.4426950408889634
LN_2 = 0.6931471805599453


def _sigmoid(x):
    return 1.0 / (1.0 + jnp.exp2(x * -LOG2_E))


def _rms_norm(x, g):
    ms = jnp.mean(x * x, axis=-1, keepdims=True)
    return x * lax.rsqrt(ms + NORM_EPS) * g


def _col_block(first_col, cols):
    assert first_col % cols == 0, (first_col, cols)
    return first_col // cols


def _layer(stacked, layer, cols=None, col_block=0):
    block = stacked.shape[1:] if cols is None else stacked.shape[1:-1] + (cols,)
    index = (layer,) + (0,) * (len(block) - 1) + (col_block,)
    return pl.BlockSpec((None,) + block, lambda *_: index, pipeline_mode=pl.Buffered(1))


def _norm_proj_body(x_ref, g_ref, wqkv_ref, wx_ref, nat_ref, s4_ref, s16_ref, xn_ref, xn4_ref, xn16_ref, slab_ref):
    tm, dm = x_ref.shape
    width = wx_ref.shape[1]

    y = _rms_norm(x_ref[...], g_ref[...])
    xn_ref[...] = y.astype(BF16)
    for c in range(dm // LANES):
        slab_ref[c] = y[:, c * LANES:(c + 1) * LANES]
    for ref, d in ((xn4_ref, 4), (xn16_ref, 16)):
        rows = tm // d
        for c in range(dm // LANES):
            for r in range(d):
                ref[r * rows:(r + 1) * rows, c * LANES:(c + 1) * LANES] = (
                    slab_ref[c, pl.ds(r, rows, stride=d), :].astype(BF16))

    def project(lhs_ref, w_ref, col, cols):
        return jnp.dot(lhs_ref[...], w_ref[:, col:col + cols], preferred_element_type=F32).astype(BF16)

    for n0 in range(0, width, PROJ_COLS):
        nat_ref[:, n0:n0 + PROJ_COLS] = project(xn_ref, wx_ref, n0, PROJ_COLS)
    gw = GROUP_WIDTH
    for part in range(3):
        src = part * N_GROUPS * gw
        dst = part * gw
        nat_ref[:, width + dst:width + dst + gw] = project(xn_ref, wqkv_ref, src, gw)
        s4_ref[0, :, :, dst:dst + gw] = project(xn4_ref, wqkv_ref, src + gw, gw).reshape(4, tm // 4, gw)
        s16_ref[0, :, :, dst:dst + gw] = project(xn16_ref, wqkv_ref, src + 2 * gw, gw).reshape(16, tm // 16, gw)


def _norm_proj(x2d, norm_g, w_in, layer, batch):
    t, dm = x2d.shape
    s = t // batch
    n_nat = dm + QKV_WIDTH
    tm = PROJ_ROWS
    tiles_per_b = s // tm
    x_lru_block = _col_block(N_GROUPS * QKV_WIDTH + GROUP_WIDTH, dm)

    def strided_spec(d):
        return pl.BlockSpec((1, d, tm // d, QKV_WIDTH), lambda i: (i // tiles_per_b, 0, i % tiles_per_b, 0))

    return pl.pallas_call(
        _norm_proj_body,
        grid=(t // tm,),
        in_specs=[pl.BlockSpec((tm, dm), lambda i: (i, 0)), _layer(norm_g, layer),
                  _layer(w_in, layer, cols=N_GROUPS * QKV_WIDTH), _layer(w_in, layer, cols=dm, col_block=x_lru_block)],
        out_specs=[pl.BlockSpec((tm, n_nat), lambda i: (i, 0)), strided_spec(4), strided_spec(16)],
        out_shape=[jax.ShapeDtypeStruct((t, n_nat), BF16),
                   jax.ShapeDtypeStruct((batch, 4, s // 4, QKV_WIDTH), BF16),
                   jax.ShapeDtypeStruct((batch, 16, s // 16, QKV_WIDTH), BF16)],
        scratch_shapes=[pltpu.VMEM((tm, dm), BF16), pltpu.VMEM((tm, dm), BF16), pltpu.VMEM((tm, dm), BF16),
                        pltpu.VMEM((dm // LANES, tm, LANES), F32)],
        compiler_params=pltpu.CompilerParams(dimension_semantics=("parallel",)),
        name="norm_proj",
    )(x2d, norm_g, w_in, w_in)


def _attn_body(q_ref, kp_ref, kc_ref, kn_ref, vp_ref, vc_ref, vn_ref, o_ref, lse_ref,
               kcat, vcat, bias, *o_slab, group, dilation, tq, n_sub):
    qi = pl.program_id(1)
    r = pl.program_id(2)
    n_q = pl.num_programs(1)
    blk = (0,) if dilation == 1 else (0, 0)
    kcol = lax.broadcasted_iota(jnp.int32, (Q_SUB, K_SUB), 1)
    slopes = [2.0 ** (-8.0 * (group * HEADS_PER_GROUP + h + 1) / N_ATT_HEADS) for h in range(HEADS_PER_GROUP)]

    @pl.when((pl.program_id(0) == 0) & (qi == 0) & (r == 0))
    def _():
        qrow = lax.broadcasted_iota(jnp.int32, (Q_SUB, K_SUB), 0)
        absd = jnp.abs(kcol - HALF_KEYS - qrow)
        dist = (absd * dilation).astype(F32)
        for h in range(HEADS_PER_GROUP):
            bias[h] = jnp.where(absd <= HALF_KEYS, (-slopes[h] * LOG2_E) * dist, NEG_INF)
            vcat[:, (2 * h + 1) * HEAD_DIM:(2 * h + 2) * HEAD_DIM] = jnp.ones((tq + 2 * HALF_KEYS, HEAD_DIM), BF16)

    kcat[0:HALF_KEYS] = kp_ref[blk]
    kcat[HALF_KEYS:HALF_KEYS + tq] = kc_ref[blk]
    kcat[HALF_KEYS + tq:] = kn_ref[blk]
    for h in range(HEADS_PER_GROUP):
        src = slice(h * HEAD_DIM, (h + 1) * HEAD_DIM)
        dst = slice(2 * h * HEAD_DIM, (2 * h + 1) * HEAD_DIM)
        vcat[0:HALF_KEYS, dst] = vp_ref[blk + (slice(None), src)]
        vcat[HALF_KEYS:HALF_KEYS + tq, dst] = vc_ref[blk + (slice(None), src)]
        vcat[HALF_KEYS + tq:, dst] = vn_ref[blk + (slice(None), src)]

    lane = lax.broadcasted_iota(jnp.int32, (Q_SUB, LANES), 1)
    n_sub_tiles = tq // Q_SUB
    for sub in range(n_sub_tiles):
        r0 = sub * Q_SUB
        lse_tile = jnp.zeros((Q_SUB, LANES), F32)
        for h in range(HEADS_PER_GROUP):
            c0 = h * HEAD_DIM
            q = q_ref[blk + (slice(r0, r0 + Q_SUB), slice(c0, c0 + HEAD_DIM))]
            k = kcat[r0:r0 + K_SUB, c0:c0 + HEAD_DIM]
            raw = lax.dot_general(q, k, (((1,), (1,)), ((), ())), preferred_element_type=F32)
            s2 = raw * (HEAD_DIM ** -0.5 * LOG2_E) + bias[h]
            if sub == 0:
                s2 = jnp.where((qi > 0) | (kcol >= HALF_KEYS), s2, NEG_INF)
            if sub == n_sub_tiles - 1:
                s2 = jnp.where((qi < n_q - 1) | (kcol < HALF_KEYS + Q_SUB), s2, NEG_INF)
            m2 = jnp.max(s2, axis=-1, keepdims=True)
            p = jnp.exp2(s2 - m2).astype(BF16)
            acc = jnp.dot(p, vcat[r0:r0 + K_SUB, 2 * c0:2 * c0 + 2 * HEAD_DIM], preferred_element_type=F32)
            den = acc[:, HEAD_DIM:]
            o = acc[:, :HEAD_DIM] / den
            if dilation == 1:
                o_ref[0, r0:r0 + Q_SUB, c0:c0 + HEAD_DIM] = o.astype(BF16)
            else:
                o_slab[0][h, pl.ds(r0 * dilation + r, Q_SUB, stride=dilation), :] = o
            lse = (m2 + jnp.log2(den)) * LN_2
            lse_tile = jnp.where((lane >= h * LSE_REP) & (lane < (h + 1) * LSE_REP), lse, lse_tile)
        if dilation == 1:
            lse_ref[0, r0:r0 + Q_SUB, :] = lse_tile
        else:
            lse_ref[0, pl.ds(r0 * dilation + r, Q_SUB, stride=dilation), :] = lse_tile

    if dilation > 1:
        @pl.when(r == dilation - 1)
        def _():
            for h in range(HEADS_PER_GROUP):
                o_ref[0, :, h * HEAD_DIM:(h + 1) * HEAD_DIM] = o_slab[0][h].astype(BF16)


def _attention_group(qkv, group, dilation):
    if dilation == 1:
        b, s, _ = qkv.shape
    else:
        b, _, n_sub, _ = qkv.shape
        s = n_sub * dilation
    n_sub = s // dilation
    tq = min(ATTN_Q_ROWS[dilation], n_sub)
    q_col = NAT_QKV_COL if dilation == 1 else 0
    halo_per_tile = tq // HALF_KEYS
    last_halo = n_sub // HALF_KEYS - 1

    def spec(rows, row_index, col):
        if dilation == 1:
            return pl.BlockSpec((1, rows, GROUP_WIDTH), lambda bi, qi, r: (bi, row_index(qi), col))
        return pl.BlockSpec((1, 1, rows, GROUP_WIDTH), lambda bi, qi, r: (bi, r, row_index(qi), col))

    def main_spec(col):
        return spec(tq, lambda qi: qi, col)

    def prev_spec(col):
        return spec(HALF_KEYS, lambda qi: jnp.maximum(qi * halo_per_tile - 1, 0), col)

    def next_spec(col):
        return spec(HALF_KEYS, lambda qi: jnp.minimum((qi + 1) * halo_per_tile, last_halo), col)

    scratch = [pltpu.VMEM((tq + 2 * HALF_KEYS, GROUP_WIDTH), BF16),
               pltpu.VMEM((tq + 2 * HALF_KEYS, 2 * GROUP_WIDTH), BF16),
               pltpu.VMEM((HEADS_PER_GROUP, Q_SUB, K_SUB), F32)]
    if dilation > 1:
        scratch.append(pltpu.VMEM((HEADS_PER_GROUP, tq * dilation, LANES), F32))

    return pl.pallas_call(
        functools.partial(_attn_body, group=group, dilation=dilation, tq=tq, n_sub=n_sub),
        grid=(b, n_sub // tq, dilation),
        in_specs=[main_spec(q_col),
                  prev_spec(q_col + 1), main_spec(q_col + 1), next_spec(q_col + 1),
                  prev_spec(q_col + 2), main_spec(q_col + 2), next_spec(q_col + 2)],
        out_specs=[pl.BlockSpec((1, tq * dilation, GROUP_WIDTH), lambda bi, qi, r: (bi, qi, 0)),
                   pl.BlockSpec((1, tq * dilation, LANES), lambda bi, qi, r: (bi, qi, 0))],
        out_shape=[jax.ShapeDtypeStruct((b, s, GROUP_WIDTH), BF16),
                   jax.ShapeDtypeStruct((b, s, LANES), F32)],
        scratch_shapes=scratch,
        compiler_params=pltpu.CompilerParams(dimension_semantics=("arbitrary", "arbitrary", "arbitrary")),
        name=f"attn_d{dilation}",
    )(qkv, qkv, qkv, qkv, qkv, qkv, qkv)


def _lru_direction(xp_ref, xc_ref, xn_ref, h_ref, tile, n_tiles, rev, dirn,
                   cw_ref, cb_ref, wa_ref, ba_ref, wx_ref, bx_ref, lam_ref,
                   xpad, a_s, b_s, hpad, carry):
    tt, width = xc_ref.shape[1:]
    bw = width // LRU_BLOCKS
    n_chunks = width // LANES
    seg_len = tt // SEGMENTS
    pitch = seg_len + SEGMENTS
    sub = lax.broadcasted_iota(jnp.int32, (SEGMENTS, width), 0)

    xf = xc_ref[0].astype(F32)
    for c in range(n_chunks):
        for k in range(SEGMENTS):
            xpad[c, k * pitch:k * pitch + seg_len, :] = xf[k * seg_len:(k + 1) * seg_len, c * LANES:(c + 1) * LANES]
    xs = jnp.concatenate(
        [jnp.concatenate([xpad[c, pl.ds(j, SEGMENTS, stride=pitch), :] for j in range(seg_len)], axis=0)
         for c in range(n_chunks)], axis=1)

    halo_prev = jnp.where(tile == 0, 0.0, xp_ref[0, BF16_ROWS - CONV_LEFT:BF16_ROWS, :].astype(F32))
    halo_next = jnp.where(tile == n_tiles - 1, 0.0, xn_ref[0, 0:1, :].astype(F32))
    before1 = jnp.where(sub == 0, halo_prev[1:2], pltpu.roll(xs[tt - SEGMENTS:tt], 1, 0))
    before2 = jnp.where(sub == 0, halo_prev[0:1], pltpu.roll(xs[tt - 2 * SEGMENTS:tt - SEGMENTS], 1, 0))
    after1 = jnp.where(sub == SEGMENTS - 1, halo_next, pltpu.roll(xs[0:SEGMENTS], SEGMENTS - 1, 0))
    xm2 = jnp.concatenate([before2, before1, xs[:tt - 2 * SEGMENTS]], axis=0)
    xm1 = jnp.concatenate([before1, xs[:tt - SEGMENTS]], axis=0)
    xp1 = jnp.concatenate([xs[SEGMENTS:], after1], axis=0)
    xc = cb_ref[...] + xm2 * cw_ref[0:1] + xm1 * cw_ref[1:2] + xs * cw_ref[2:3] + xp1 * cw_ref[3:4]
    xcb = xc.astype(BF16)

    lam = lam_ref[dirn:dirn + 1]
    neg_lam = -lam
    softplus = jnp.maximum(neg_lam, 0.0) + jnp.log1p(jnp.exp(-jnp.abs(neg_lam)))
    coef = -RG_C * softplus

    sub_blk = lax.broadcasted_iota(jnp.int32, (SEGMENTS, bw), 0)
    if rev:
        start_rows = slice(tt - SEGMENTS, tt)
        first = (tile == n_tiles - 1) & (sub_blk == SEGMENTS - 1)
    else:
        start_rows = slice(0, SEGMENTS)
        first = (tile == 0) & (sub_blk == 0)

    for blk in range(LRU_BLOCKS):
        c0 = blk * bw
        xblk = xcb[:, c0:c0 + bw]
        pa = jnp.dot(xblk, wa_ref[dirn, blk], preferred_element_type=F32) + ba_ref[dirn:dirn + 1, c0:c0 + bw]
        px = jnp.dot(xblk, wx_ref[dirn, blk], preferred_element_type=F32) + bx_ref[dirn:dirn + 1, c0:c0 + bw]
        log_a = coef[:, c0:c0 + bw] * _sigmoid(pa)
        a = jnp.exp(log_a)
        u = jnp.tanh(log_a) * (-1.0 - a * a)
        mult = jnp.where(u > 0.0, u * lax.rsqrt(u), 0.0)
        gated = _sigmoid(px) * xc[:, c0:c0 + bw]
        b = mult * gated
        a_s[:, c0:c0 + bw] = a
        b_s[:, c0:c0 + bw] = b
        b_s[start_rows, c0:c0 + bw] = jnp.where(first, gated[start_rows], b[start_rows])

    steps = range(seg_len - 1, -1, -1) if rev else range(seg_len)
    segs = range(SEGMENTS - 1, -1, -1) if rev else range(SEGMENTS)
    sub1 = lax.broadcasted_iota(jnp.int32, (SEGMENTS, LANES), 0)
    for c in range(n_chunks):
        cs = slice(c * LANES, (c + 1) * LANES)
        prod = jnp.ones((SEGMENTS, LANES), F32)
        h = jnp.zeros((SEGMENTS, LANES), F32)
        for j in steps:
            aj = a_s[j * SEGMENTS:(j + 1) * SEGMENTS, cs]
            h = aj * h + b_s[j * SEGMENTS:(j + 1) * SEGMENTS, cs]
            prod = aj * prod
        state = carry[dirn:dirn + 1, cs]
        h0 = jnp.zeros((SEGMENTS, LANES), F32)
        for k in segs:
            h0 = jnp.where(sub1 == k, state, h0)
            state = prod[k:k + 1] * state + h[k:k + 1]
        carry[dirn:dirn + 1, cs] = state
        h = h0
        for j in steps:
            h = a_s[j * SEGMENTS:(j + 1) * SEGMENTS, cs] * h + b_s[j * SEGMENTS:(j + 1) * SEGMENTS, cs]
            hpad[c, pl.ds(j, SEGMENTS, stride=pitch), :] = h
        for k in range(SEGMENTS):
            h_ref[0, k * seg_len:(k + 1) * seg_len, cs] = hpad[c, k * pitch:k * pitch + seg_len, :].astype(h_ref.dtype)


def _lru_body(xfp_ref, xfc_ref, xfn_ref, xbp_ref, xbc_ref, xbn_ref,
              cw_ref, cb_ref, wa_ref, ba_ref, wx_ref, bx_ref, lam_ref,
              hf_ref, hb_ref, xpad_f, a_f, b_f, hpad_f, xpad_b, a_b, b_b, hpad_b, carry):
    i = pl.program_id(1)
    n_tiles = pl.num_programs(1)

    @pl.when(i == 0)
    def _():
        carry[...] = jnp.zeros_like(carry)

    params = (cw_ref, cb_ref, wa_ref, ba_ref, wx_ref, bx_ref, lam_ref)
    _lru_direction(xfp_ref, xfc_ref, xfn_ref, hf_ref, i, n_tiles, False, 0, *params,
                   xpad_f, a_f, b_f, hpad_f, carry)
    _lru_direction(xbp_ref, xbc_ref, xbn_ref, hb_ref, n_tiles - 1 - i, n_tiles, True, 1, *params,
                   xpad_b, a_b, b_b, hpad_b, carry)


def _lru(nat, conv_w, conv_b, w_a, b_a, w_x, b_x, lam, layer):
    b, s, _ = nat.shape
    width = conv_w.shape[-1]
    tt = LRU_ROWS
    n_tiles = s // tt
    halo_per_tile = tt // BF16_ROWS
    last_halo = s // BF16_ROWS - 1

    def tile_of(i, rev):
        return n_tiles - 1 - i if rev else i

    def specs(rev):
        return [
            pl.BlockSpec((1, BF16_ROWS, width),
                         lambda bi, i: (bi, jnp.maximum(tile_of(i, rev) * halo_per_tile - 1, 0), XLRU_COL)),
            pl.BlockSpec((1, tt, width), lambda bi, i: (bi, tile_of(i, rev), XLRU_COL)),
            pl.BlockSpec((1, BF16_ROWS, width),
                         lambda bi, i: (bi, jnp.minimum((tile_of(i, rev) + 1) * halo_per_tile, last_halo), XLRU_COL)),
        ]

    consts = (conv_w, conv_b, w_a, b_a, w_x, b_x, lam)
    padded_rows = tt + SEGMENTS * SEGMENTS
    per_direction = [pltpu.VMEM((width // LANES, padded_rows, LANES), F32),
                     pltpu.VMEM((tt, width), F32),
                     pltpu.VMEM((tt, width), F32),
                     pltpu.VMEM((width // LANES, padded_rows, LANES), F32)]
    return pl.pallas_call(
        _lru_body,
        grid=(b, n_tiles),
        in_specs=specs(False) + specs(True) + [_layer(a, layer) for a in consts],
        out_specs=[pl.BlockSpec((1, tt, width), lambda bi, i: (bi, i, 0)),
                   pl.BlockSpec((1, tt, width), lambda bi, i: (bi, n_tiles - 1 - i, 0))],
        out_shape=[jax.ShapeDtypeStruct((b, s, width), BF16)] * 2,
        scratch_shapes=per_direction + per_direction + [pltpu.VMEM((F32_ROWS, width), F32)],
        compiler_params=pltpu.CompilerParams(dimension_semantics=("parallel", "arbitrary")),
        name="lru",
    )(nat, nat, nat, nat, nat, nat, *consts)


def _merge_body(o0_ref, o1_ref, o2_ref, l0_ref, l1_ref, l2_ref, hf_ref, hb_ref, x_ref,
                ng_ref, wga_ref, wgo_ref, bg_ref, woa_ref, wol_ref, wout_ref, fg_ref, out_ref, *, final):
    tm, dm = x_ref.shape
    o_refs = (o0_ref, o1_ref, o2_ref)
    for r0 in range(0, tm, MERGE_SUB_ROWS):
        rows = slice(r0, r0 + MERGE_SUB_ROWS)
        xn = _rms_norm(x_ref[rows], ng_ref[...]).astype(BF16)

        gatt = jnp.dot(xn, wga_ref[...], preferred_element_type=F32)
        glru, logit_att, logit_lru = (
            jnp.dot(xn, wgo_ref[:, c0:c0 + dm], preferred_element_type=F32) for c0 in (0, dm, 2 * dm))

        l0, l1, l2 = l0_ref[rows], l1_ref[rows], l2_ref[rows]
        mx = jnp.maximum(jnp.maximum(l0, l1), l2)
        e0, e1, e2 = jnp.exp(l0 - mx), jnp.exp(l1 - mx), jnp.exp(l2 - mx)
        inv = 1.0 / (e0 + e1 + e2)
        alphas = (e0 * inv, e1 * inv, e2 * inv)

        heads = []
        for h in range(HEADS_PER_GROUP):
            c0 = h * HEAD_DIM
            acc = jnp.zeros((MERGE_SUB_ROWS, HEAD_DIM), F32)
            for g in range(N_GROUPS):
                w = jnp.broadcast_to(alphas[g][:, h * LSE_REP:h * LSE_REP + 1], (MERGE_SUB_ROWS, HEAD_DIM))
                acc = acc + w * o_refs[g][rows, c0:c0 + HEAD_DIM].astype(F32)
            heads.append(acc)
        mixed = jnp.concatenate(heads, axis=-1)

        y_att = (mixed * (gatt * _sigmoid(gatt))).astype(BF16)
        h_sum = hf_ref[rows].astype(F32) + hb_ref[rows].astype(F32)
        y_lru = (h_sum * (glru * _sigmoid(glru))).astype(BF16)

        p_att = jnp.dot(y_att, woa_ref[...], preferred_element_type=F32)
        p_lru = jnp.dot(y_lru, wol_ref[...], preferred_element_type=F32)
        gate_att = _sigmoid(logit_att + bg_ref[:, 0:dm])
        gate_lru = _sigmoid(logit_lru + bg_ref[:, dm:2 * dm])
        merged = (gate_att * p_att + gate_lru * p_lru).astype(BF16)
        y = x_ref[rows] + jnp.dot(merged, wout_ref[...], preferred_element_type=F32)
        if final:
            y = _rms_norm(y, fg_ref[...])
        out_ref[rows] = y


def _merge(x2d, outs, lses, h_f, h_b, norm_g, w_in, b_gate, w_o_att, w_o_lru, w_out, final_g, layer, final):
    t, dm = x2d.shape
    tm = MERGE_ROWS
    att_gate_block = _col_block(N_GROUPS * QKV_WIDTH, GROUP_WIDTH)
    others_block = _col_block(N_GROUPS * QKV_WIDTH + GROUP_WIDTH + dm, 3 * dm)

    def rows(width):
        return pl.BlockSpec((tm, width), lambda i: (i, 0))

    return pl.pallas_call(
        functools.partial(_merge_body, final=final),
        grid=(t // tm,),
        in_specs=[rows(GROUP_WIDTH)] * 3 + [rows(LANES)] * 3 + [rows(dm)] * 3
                 + [_layer(norm_g, layer), _layer(w_in, layer, cols=GROUP_WIDTH, col_block=att_gate_block),
                    _layer(w_in, layer, cols=3 * dm, col_block=others_block), _layer(b_gate, layer),
                    _layer(w_o_att, layer), _layer(w_o_lru, layer), _layer(w_out, layer),
                    pl.BlockSpec(final_g.shape, lambda i: (0, 0), pipeline_mode=pl.Buffered(1))],
        out_specs=rows(dm),
        out_shape=jax.ShapeDtypeStruct((t, dm), F32),
        compiler_params=pltpu.CompilerParams(dimension_semantics=("parallel",)),
        name="merge_final" if final else "merge",
    )(*outs, *lses, h_f, h_b, x2d, norm_g, w_in, w_in, b_gate, w_o_att, w_o_lru, w_out, final_g)


def kernel(x, norm_g, w_in, b_gate, conv_w, conv_b, rg_w_a, rg_b_a, rg_w_x, rg_b_x, rg_lam,
           w_o_att, w_o_lru, w_out, final_g):
    b, s, dm = x.shape
    depth = w_in.shape[0]
    t = b * s
    w_in, rg_w_a, rg_w_x, w_o_att, w_o_lru, w_out = (
        w.astype(BF16) for w in (w_in, rg_w_a, rg_w_x, w_o_att, w_o_lru, w_out))
    norm_g, b_gate, conv_b = (v.reshape(depth, 1, -1) for v in (norm_g, b_gate, conv_b))
    final_g = final_g.reshape(1, dm)

    x2d = x.reshape(t, dm)
    for layer in range(depth):
        nat2d, qkv4, qkv16 = _norm_proj(x2d, norm_g, w_in, layer, b)
        nat = nat2d.reshape(b, s, -1)
        outs, lses = [], []
        for group, qkv in enumerate((nat, qkv4, qkv16)):
            o, lse = _attention_group(qkv, group, ATT_PATTERNS[group][1])
            outs.append(o.reshape(t, GROUP_WIDTH))
            lses.append(lse.reshape(t, LANES))
        h_f, h_b = _lru(nat, conv_w, conv_b, rg_w_a, rg_b_a, rg_w_x, rg_b_x, rg_lam, layer)
        x2d = _merge(x2d, outs, lses, h_f.reshape(t, dm), h_b.reshape(t, dm), norm_g, w_in, b_gate,
                     w_o_att, w_o_lru, w_out, final_g, layer, final=(layer == depth - 1))
    return x2d.reshape(b, s, dm)
```

```python
import functools

import jax
import jax.numpy as jnp
from jax import lax
from jax.experimental import pallas as pl
from jax.experimental.pallas import tpu as pltpu

F32 = jnp.float32
BF16 = jnp.bfloat16

HEAD_DIM = 128
HEADS_PER_GROUP = 4
ATT_PATTERNS = ((128, 1), (512, 4), (2048, 16))
N_GROUPS = len(ATT_PATTERNS)
N_ATT_HEADS = N_GROUPS * HEADS_PER_GROUP
GROUP_WIDTH = HEADS_PER_GROUP * HEAD_DIM
QKV_WIDTH = 3 * GROUP_WIDTH
NEG_INF = -1e30
LRU_BLOCKS = 4
RG_C = 8.0
NORM_EPS = 1e-6
CONV_LEFT = 2

XLRU_COL = 0
NAT_QKV_COL = 2

LANES = 128
F32_ROWS = 8
BF16_ROWS = 16
HALF_KEYS = 64
Q_SUB = 128
K_SUB = Q_SUB + 2 * HALF_KEYS
LSE_REP = LANES // HEADS_PER_GROUP
SEGMENTS = F32_ROWS

PROJ_ROWS = 512
PROJ_COLS = 512
ATTN_Q_ROWS = {1: 1024, 4: 1024, 16: 512}
LRU_ROWS = 512
MERGE_ROWS = 512
MERGE_SUB_ROWS = 256

LOG2_E = 1.4426950408889634
LN_2 = 0.6931471805599453


def _sigmoid(x):
    return 1.0 / (1.0 + jnp.exp2(x * -LOG2_E))


def _rms_norm(x, g):
    ms = jnp.mean(x * x, axis=-1, keepdims=True)
    return x * lax.rsqrt(ms + NORM_EPS) * g


def _col_block(first_col, cols):
    assert first_col % cols == 0, (first_col, cols)
    return first_col // cols


def _layer(stacked, layer, cols=None, col_block=0):
    block = stacked.shape[1:] if cols is None else stacked.shape[1:-1] + (cols,)
    index = (layer,) + (0,) * (len(block) - 1) + (col_block,)
    return pl.BlockSpec((None,) + block, lambda *_: index, pipeline_mode=pl.Buffered(1))


def _norm_proj_body(x_ref, g_ref, wqkv_ref, wx_ref, nat_ref, s4_ref, s16_ref, xn_ref, xn4_ref, xn16_ref, slab_ref):
    tm, dm = x_ref.shape
    width = wx_ref.shape[1]

    y = _rms_norm(x_ref[...], g_ref[...])
    xn_ref[...] = y.astype(BF16)
    for c in range(dm // LANES):
        slab_ref[c] = y[:, c * LANES:(c + 1) * LANES]
    for ref, d in ((xn4_ref, 4), (xn16_ref, 16)):
        rows = tm // d
        for c in range(dm // LANES):
            for r in range(d):
                ref[r * rows:(r + 1) * rows, c * LANES:(c + 1) * LANES] = (
                    slab_ref[c, pl.ds(r, rows, stride=d), :].astype(BF16))

    def project(lhs_ref, w_ref, col, cols):
        return jnp.dot(lhs_ref[...], w_ref[:, col:col + cols], preferred_element_type=F32).astype(BF16)

    for n0 in range(0, width, PROJ_COLS):
        nat_ref[:, n0:n0 + PROJ_COLS] = project(xn_ref, wx_ref, n0, PROJ_COLS)
    gw = GROUP_WIDTH
    for part in range(3):
        src = part * N_GROUPS * gw
        dst = part * gw
        nat_ref[:, width + dst:width + dst + gw] = project(xn_ref, wqkv_ref, src, gw)
        s4_ref[0, :, :, dst:dst + gw] = project(xn4_ref, wqkv_ref, src + gw, gw).reshape(4, tm // 4, gw)
        s16_ref[0, :, :, dst:dst + gw] = project(xn16_ref, wqkv_ref, src + 2 * gw, gw).reshape(16, tm // 16, gw)


def _norm_proj(x2d, norm_g, w_in, layer, batch):
    t, dm = x2d.shape
    s = t // batch
    n_nat = dm + QKV_WIDTH
    tm = PROJ_ROWS
    tiles_per_b = s // tm
    x_lru_block = _col_block(N_GROUPS * QKV_WIDTH + GROUP_WIDTH, dm)

    def strided_spec(d):
        return pl.BlockSpec((1, d, tm // d, QKV_WIDTH), lambda i: (i // tiles_per_b, 0, i % tiles_per_b, 0))

    return pl.pallas_call(
        _norm_proj_body,
        grid=(t // tm,),
        in_specs=[pl.BlockSpec((tm, dm), lambda i: (i, 0)), _layer(norm_g, layer),
                  _layer(w_in, layer, cols=N_GROUPS * QKV_WIDTH), _layer(w_in, layer, cols=dm, col_block=x_lru_block)],
        out_specs=[pl.BlockSpec((tm, n_nat), lambda i: (i, 0)), strided_spec(4), strided_spec(16)],
        out_shape=[jax.ShapeDtypeStruct((t, n_nat), BF16),
                   jax.ShapeDtypeStruct((batch, 4, s // 4, QKV_WIDTH), BF16),
                   jax.ShapeDtypeStruct((batch, 16, s // 16, QKV_WIDTH), BF16)],
        scratch_shapes=[pltpu.VMEM((tm, dm), BF16), pltpu.VMEM((tm, dm), BF16), pltpu.VMEM((tm, dm), BF16),
                        pltpu.VMEM((dm // LANES, tm, LANES), F32)],
        compiler_params=pltpu.CompilerParams(dimension_semantics=("parallel",)),
        name="norm_proj",
    )(x2d, norm_g, w_in, w_in)


def _attn_body(q_ref, kp_ref, kc_ref, kn_ref, vp_ref, vc_ref, vn_ref, o_ref, lse_ref,
               kcat, vcat, bias, *o_slab, group, dilation, tq, n_sub):
    qi = pl.program_id(1)
    r = pl.program_id(2)
    n_q = pl.num_programs(1)
    blk = (0,) if dilation == 1 else (0, 0)
    kcol = lax.broadcasted_iota(jnp.int32, (Q_SUB, K_SUB), 1)
    slopes = [2.0 ** (-8.0 * (group * HEADS_PER_GROUP + h + 1) / N_ATT_HEADS) for h in range(HEADS_PER_GROUP)]

    @pl.when((pl.program_id(0) == 0) & (qi == 0) & (r == 0))
    def _():
        qrow = lax.broadcasted_iota(jnp.int32, (Q_SUB, K_SUB), 0)
        absd = jnp.abs(kcol - HALF_KEYS - qrow)
        dist = (absd * dilation).astype(F32)
        for h in range(HEADS_PER_GROUP):
            bias[h] = jnp.where(absd <= HALF_KEYS, (-slopes[h] * LOG2_E) * dist, NEG_INF)
            vcat[:, (2 * h + 1) * HEAD_DIM:(2 * h + 2) * HEAD_DIM] = jnp.ones((tq + 2 * HALF_KEYS, HEAD_DIM), BF16)

    kcat[0:HALF_KEYS] = kp_ref[blk]
    kcat[HALF_KEYS:HALF_KEYS + tq] = kc_ref[blk]
    kcat[HALF_KEYS + tq:] = kn_ref[blk]
    for h in range(HEADS_PER_GROUP):
        src = slice(h * HEAD_DIM, (h + 1) * HEAD_DIM)
        dst = slice(2 * h * HEAD_DIM, (2 * h + 1) * HEAD_DIM)
        vcat[0:HALF_KEYS, dst] = vp_ref[blk + (slice(None), src)]
        vcat[HALF_KEYS:HALF_KEYS + tq, dst] = vc_ref[blk + (slice(None), src)]
        vcat[HALF_KEYS + tq:, dst] = vn_ref[blk + (slice(None), src)]

    lane = lax.broadcasted_iota(jnp.int32, (Q_SUB, LANES), 1)
    n_sub_tiles = tq // Q_SUB
    for sub in range(n_sub_tiles):
        r0 = sub * Q_SUB
        lse_tile = jnp.zeros((Q_SUB, LANES), F32)
        for h in range(HEADS_PER_GROUP):
            c0 = h * HEAD_DIM
            q = q_ref[blk + (slice(r0, r0 + Q_SUB), slice(c0, c0 + HEAD_DIM))]
            k = kcat[r0:r0 + K_SUB, c0:c0 + HEAD_DIM]
            raw = lax.dot_general(q, k, (((1,), (1,)), ((), ())), preferred_element_type=F32)
            s2 = raw * (HEAD_DIM ** -0.5 * LOG2_E) + bias[h]
            if sub == 0:
                s2 = jnp.where((qi > 0) | (kcol >= HALF_KEYS), s2, NEG_INF)
            if sub == n_sub_tiles - 1:
                s2 = jnp.where((qi < n_q - 1) | (kcol < HALF_KEYS + Q_SUB), s2, NEG_INF)
            m2 = jnp.max(s2, axis=-1, keepdims=True)
            p = jnp.exp2(s2 - m2).astype(BF16)
            acc = jnp.dot(p, vcat[r0:r0 + K_SUB, 2 * c0:2 * c0 + 2 * HEAD_DIM], preferred_element_type=F32)
            den = acc[:, HEAD_DIM:]
            o = acc[:, :HEAD_DIM] / den
            if dilation == 1:
                o_ref[0, r0:r0 + Q_SUB, c0:c0 + HEAD_DIM] = o.astype(BF16)
            else:
                o_slab[0][h, pl.ds(r0 * dilation + r, Q_SUB, stride=dilation), :] = o
            lse = (m2 + jnp.log2(den)) * LN_2
            lse_tile = jnp.where((lane >= h * LSE_REP) & (lane < (h + 1) * LSE_REP), lse, lse_tile)
        if dilation == 1:
            lse_ref[0, r0:r0 + Q_SUB, :] = lse_tile
        else:
            lse_ref[0, pl.ds(r0 * dilation + r, Q_SUB, stride=dilation), :] = lse_tile

    if dilation > 1:
        @pl.when(r == dilation - 1)
        def _():
            for h in range(HEADS_PER_GROUP):
                o_ref[0, :, h * HEAD_DIM:(h + 1) * HEAD_DIM] = o_slab[0][h].astype(BF16)


def _attention_group(qkv, group, dilation):
    if dilation == 1:
        b, s, _ = qkv.shape
    else:
        b, _, n_sub, _ = qkv.shape
        s = n_sub * dilation
    n_sub = s // dilation
    tq = min(ATTN_Q_ROWS[dilation], n_sub)
    q_col = NAT_QKV_COL if dilation == 1 else 0
    halo_per_tile = tq // HALF_KEYS
    last_halo = n_sub // HALF_KEYS - 1

    def spec(rows, row_index, col):
        if dilation == 1:
            return pl.BlockSpec((1, rows, GROUP_WIDTH), lambda bi, qi, r: (bi, row_index(qi), col))
        return pl.BlockSpec((1, 1, rows, GROUP_WIDTH), lambda bi, qi, r: (bi, r, row_index(qi), col))

    def main_spec(col):
        return spec(tq, lambda qi: qi, col)

    def prev_spec(col):
        return spec(HALF_KEYS, lambda qi: jnp.maximum(qi * halo_per_tile - 1, 0), col)

    def next_spec(col):
        return spec(HALF_KEYS, lambda qi: jnp.minimum((qi + 1) * halo_per_tile, last_halo), col)

    scratch = [pltpu.VMEM((tq + 2 * HALF_KEYS, GROUP_WIDTH), BF16),
               pltpu.VMEM((tq + 2 * HALF_KEYS, 2 * GROUP_WIDTH), BF16),
               pltpu.VMEM((HEADS_PER_GROUP, Q_SUB, K_SUB), F32)]
    if dilation > 1:
        scratch.append(pltpu.VMEM((HEADS_PER_GROUP, tq * dilation, LANES), F32))

    return pl.pallas_call(
        functools.partial(_attn_body, group=group, dilation=dilation, tq=tq, n_sub=n_sub),
        grid=(b, n_sub // tq, dilation),
        in_specs=[main_spec(q_col),
                  prev_spec(q_col + 1), main_spec(q_col + 1), next_spec(q_col + 1),
                  prev_spec(q_col + 2), main_spec(q_col + 2), next_spec(q_col + 2)],
        out_specs=[pl.BlockSpec((1, tq * dilation, GROUP_WIDTH), lambda bi, qi, r: (bi, qi, 0)),
                   pl.BlockSpec((1, tq * dilation, LANES), lambda bi, qi, r: (bi, qi, 0))],
        out_shape=[jax.ShapeDtypeStruct((b, s, GROUP_WIDTH), BF16),
                   jax.ShapeDtypeStruct((b, s, LANES), F32)],
        scratch_shapes=scratch,
        compiler_params=pltpu.CompilerParams(dimension_semantics=("arbitrary", "arbitrary", "arbitrary")),
        name=f"attn_d{dilation}",
    )(qkv, qkv, qkv, qkv, qkv, qkv, qkv)


def _lru_direction(xp_ref, xc_ref, xn_ref, h_ref, tile, n_tiles, rev, dirn,
                   cw_ref, cb_ref, wa_ref, ba_ref, wx_ref, bx_ref, lam_ref,
                   xpad, a_s, b_s, hpad, carry):
    tt, width = xc_ref.shape[1:]
    bw = width // LRU_BLOCKS
    n_chunks = width // LANES
    seg_len = tt // SEGMENTS
    pitch = seg_len + SEGMENTS
    sub = lax.broadcasted_iota(jnp.int32, (SEGMENTS, width), 0)

    xf = xc_ref[0].astype(F32)
    for c in range(n_chunks):
        for k in range(SEGMENTS):
            xpad[c, k * pitch:k * pitch + seg_len, :] = xf[k * seg_len:(k + 1) * seg_len, c * LANES:(c + 1) * LANES]
    xs = jnp.concatenate(
        [jnp.concatenate([xpad[c, pl.ds(j, SEGMENTS, stride=pitch), :] for j in range(seg_len)], axis=0)
         for c in range(n_chunks)], axis=1)

    halo_prev = jnp.where(tile == 0, 0.0, xp_ref[0, BF16_ROWS - CONV_LEFT:BF16_ROWS, :].astype(F32))
    halo_next = jnp.where(tile == n_tiles - 1, 0.0, xn_ref[0, 0:1, :].astype(F32))
    before1 = jnp.where(sub == 0, halo_prev[1:2], pltpu.roll(xs[tt - SEGMENTS:tt], 1, 0))
    before2 = jnp.where(sub == 0, halo_prev[0:1], pltpu.roll(xs[tt - 2 * SEGMENTS:tt - SEGMENTS], 1, 0))
    after1 = jnp.where(sub == SEGMENTS - 1, halo_next, pltpu.roll(xs[0:SEGMENTS], SEGMENTS - 1, 0))
    xm2 = jnp.concatenate([before2, before1, xs[:tt - 2 * SEGMENTS]], axis=0)
    xm1 = jnp.concatenate([before1, xs[:tt - SEGMENTS]], axis=0)
    xp1 = jnp.concatenate([xs[SEGMENTS:], after1], axis=0)
    xc = cb_ref[...] + xm2 * cw_ref[0:1] + xm1 * cw_ref[1:2] + xs * cw_ref[2:3] + xp1 * cw_ref[3:4]
    xcb = xc.astype(BF16)

    lam = lam_ref[dirn:dirn + 1]
    neg_lam = -lam
    softplus = jnp.maximum(neg_lam, 0.0) + jnp.log1p(jnp.exp(-jnp.abs(neg_lam)))
    coef = -RG_C * softplus

    sub_blk = lax.broadcasted_iota(jnp.int32, (SEGMENTS, bw), 0)
    if rev:
        start_rows = slice(tt - SEGMENTS, tt)
        first = (tile == n_tiles - 1) & (sub_blk == SEGMENTS - 1)
    else:
        start_rows = slice(0, SEGMENTS)
        first = (tile == 0) & (sub_blk == 0)

    for blk in range(LRU_BLOCKS):
        c0 = blk * bw
        xblk = xcb[:, c0:c0 + bw]
        pa = jnp.dot(xblk, wa_ref[dirn, blk], preferred_element_type=F32) + ba_ref[dirn:dirn + 1, c0:c0 + bw]
        px = jnp.dot(xblk, wx_ref[dirn, blk], preferred_element_type=F32) + bx_ref[dirn:dirn + 1, c0:c0 + bw]
        log_a = coef[:, c0:c0 + bw] * _sigmoid(pa)
        a = jnp.exp(log_a)
        u = jnp.tanh(log_a) * (-1.0 - a * a)
        mult = jnp.where(u > 0.0, u * lax.rsqrt(u), 0.0)
        gated = _sigmoid(px) * xc[:, c0:c0 + bw]
        b = mult * gated
        a_s[:, c0:c0 + bw] = a
        b_s[:, c0:c0 + bw] = b
        b_s[start_rows, c0:c0 + bw] = jnp.where(first, gated[start_rows], b[start_rows])

    steps = range(seg_len - 1, -1, -1) if rev else range(seg_len)
    segs = range(SEGMENTS - 1, -1, -1) if rev else range(SEGMENTS)
    sub1 = lax.broadcasted_iota(jnp.int32, (SEGMENTS, LANES), 0)
    for c in range(n_chunks):
        cs = slice(c * LANES, (c + 1) * LANES)
        prod = jnp.ones((SEGMENTS, LANES), F32)
        h = jnp.zeros((SEGMENTS, LANES), F32)
        for j in steps:
            aj = a_s[j * SEGMENTS:(j + 1) * SEGMENTS, cs]
            h = aj * h + b_s[j * SEGMENTS:(j + 1) * SEGMENTS, cs]
            prod = aj * prod
        state = carry[dirn:dirn + 1, cs]
        h0 = jnp.zeros((SEGMENTS, LANES), F32)
        for k in segs:
            h0 = jnp.where(sub1 == k, state, h0)
            state = prod[k:k + 1] * state + h[k:k + 1]
        carry[dirn:dirn + 1, cs] = state
        h = h0
        for j in steps:
            h = a_s[j * SEGMENTS:(j + 1) * SEGMENTS, cs] * h + b_s[j * SEGMENTS:(j + 1) * SEGMENTS, cs]
            hpad[c, pl.ds(j, SEGMENTS, stride=pitch), :] = h
        for k in range(SEGMENTS):
            h_ref[0, k * seg_len:(k + 1) * seg_len, cs] = hpad[c, k * pitch:k * pitch + seg_len, :].astype(h_ref.dtype)


def _lru_body(xfp_ref, xfc_ref, xfn_ref, xbp_ref, xbc_ref, xbn_ref,
              cw_ref, cb_ref, wa_ref, ba_ref, wx_ref, bx_ref, lam_ref,
              hf_ref, hb_ref, xpad_f, a_f, b_f, hpad_f, xpad_b, a_b, b_b, hpad_b, carry):
    i = pl.program_id(1)
    n_tiles = pl.num_programs(1)

    @pl.when(i == 0)
    def _():
        carry[...] = jnp.zeros_like(carry)

    params = (cw_ref, cb_ref, wa_ref, ba_ref, wx_ref, bx_ref, lam_ref)
    _lru_direction(xfp_ref, xfc_ref, xfn_ref, hf_ref, i, n_tiles, False, 0, *params,
                   xpad_f, a_f, b_f, hpad_f, carry)
    _lru_direction(xbp_ref, xbc_ref, xbn_ref, hb_ref, n_tiles - 1 - i, n_tiles, True, 1, *params,
                   xpad_b, a_b, b_b, hpad_b, carry)


def _lru(nat, conv_w, conv_b, w_a, b_a, w_x, b_x, lam, layer):
    b, s, _ = nat.shape
    width = conv_w.shape[-1]
    tt = LRU_ROWS
    n_tiles = s // tt
    halo_per_tile = tt // BF16_ROWS
    last_halo = s // BF16_ROWS - 1

    def tile_of(i, rev):
        return n_tiles - 1 - i if rev else i

    def specs(rev):
        return [
            pl.BlockSpec((1, BF16_ROWS, width),
                         lambda bi, i: (bi, jnp.maximum(tile_of(i, rev) * halo_per_tile - 1, 0), XLRU_COL)),
            pl.BlockSpec((1, tt, width), lambda bi, i: (bi, tile_of(i, rev), XLRU_COL)),
            pl.BlockSpec((1, BF16_ROWS, width),
                         lambda bi, i: (bi, jnp.minimum((tile_of(i, rev) + 1) * halo_per_tile, last_halo), XLRU_COL)),
        ]

    consts = (conv_w, conv_b, w_a, b_a, w_x, b_x, lam)
    padded_rows = tt + SEGMENTS * SEGMENTS
    per_direction = [pltpu.VMEM((width // LANES, padded_rows, LANES), F32),
                     pltpu.VMEM((tt, width), F32),
                     pltpu.VMEM((tt, width), F32),
                     pltpu.VMEM((width // LANES, padded_rows, LANES), F32)]
    return pl.pallas_call(
        _lru_body,
        grid=(b, n_tiles),
        in_specs=specs(False) + specs(True) + [_layer(a, layer) for a in consts],
        out_specs=[pl.BlockSpec((1, tt, width), lambda bi, i: (bi, i, 0)),
                   pl.BlockSpec((1, tt, width), lambda bi, i: (bi, n_tiles - 1 - i, 0))],
        out_shape=[jax.ShapeDtypeStruct((b, s, width), BF16)] * 2,
        scratch_shapes=per_direction + per_direction + [pltpu.VMEM((F32_ROWS, width), F32)],
        compiler_params=pltpu.CompilerParams(dimension_semantics=("parallel", "arbitrary")),
        name="lru",
    )(nat, nat, nat, nat, nat, nat, *consts)


def _merge_body(o0_ref, o1_ref, o2_ref, l0_ref, l1_ref, l2_ref, hf_ref, hb_ref, x_ref,
                ng_ref, wga_ref, wgo_ref, bg_ref, woa_ref, wol_ref, wout_ref, fg_ref, out_ref, *, final):
    tm, dm = x_ref.shape
    o_refs = (o0_ref, o1_ref, o2_ref)
    for r0 in range(0, tm, MERGE_SUB_ROWS):
        rows = slice(r0, r0 + MERGE_SUB_ROWS)
        xn = _rms_norm(x_ref[rows], ng_ref[...]).astype(BF16)

        gatt = jnp.dot(xn, wga_ref[...], preferred_element_type=F32)
        glru, logit_att, logit_lru = (
            jnp.dot(xn, wgo_ref[:, c0:c0 + dm], preferred_element_type=F32) for c0 in (0, dm, 2 * dm))

        l0, l1, l2 = l0_ref[rows], l1_ref[rows], l2_ref[rows]
        mx = jnp.maximum(jnp.maximum(l0, l1), l2)
        e0, e1, e2 = jnp.exp(l0 - mx), jnp.exp(l1 - mx), jnp.exp(l2 - mx)
        inv = 1.0 / (e0 + e1 + e2)
        alphas = (e0 * inv, e1 * inv, e2 * inv)

        heads = []
        for h in range(HEADS_PER_GROUP):
            c0 = h * HEAD_DIM
            acc = jnp.zeros((MERGE_SUB_ROWS, HEAD_DIM), F32)
            for g in range(N_GROUPS):
                w = jnp.broadcast_to(alphas[g][:, h * LSE_REP:h * LSE_REP + 1], (MERGE_SUB_ROWS, HEAD_DIM))
                acc = acc + w * o_refs[g][rows, c0:c0 + HEAD_DIM].astype(F32)
            heads.append(acc)
        mixed = jnp.concatenate(heads, axis=-1)

        y_att = (mixed * (gatt * _sigmoid(gatt))).astype(BF16)
        h_sum = hf_ref[rows].astype(F32) + hb_ref[rows].astype(F32)
        y_lru = (h_sum * (glru * _sigmoid(glru))).astype(BF16)

        p_att = jnp.dot(y_att, woa_ref[...], preferred_element_type=F32)
        p_lru = jnp.dot(y_lru, wol_ref[...], preferred_element_type=F32)
        gate_att = _sigmoid(logit_att + bg_ref[:, 0:dm])
        gate_lru = _sigmoid(logit_lru + bg_ref[:, dm:2 * dm])
        merged = (gate_att * p_att + gate_lru * p_lru).astype(BF16)
        y = x_ref[rows] + jnp.dot(merged, wout_ref[...], preferred_element_type=F32)
        if final:
            y = _rms_norm(y, fg_ref[...])
        out_ref[rows] = y


def _merge(x2d, outs, lses, h_f, h_b, norm_g, w_in, b_gate, w_o_att, w_o_lru, w_out, final_g, layer, final):
    t, dm = x2d.shape
    tm = MERGE_ROWS
    att_gate_block = _col_block(N_GROUPS * QKV_WIDTH, GROUP_WIDTH)
    others_block = _col_block(N_GROUPS * QKV_WIDTH + GROUP_WIDTH + dm, 3 * dm)

    def rows(width):
        return pl.BlockSpec((tm, width), lambda i: (i, 0))

    return pl.pallas_call(
        functools.partial(_merge_body, final=final),
        grid=(t // tm,),
        in_specs=[rows(GROUP_WIDTH)] * 3 + [rows(LANES)] * 3 + [rows(dm)] * 3
                 + [_layer(norm_g, layer), _layer(w_in, layer, cols=GROUP_WIDTH, col_block=att_gate_block),
                    _layer(w_in, layer, cols=3 * dm, col_block=others_block), _layer(b_gate, layer),
                    _layer(w_o_att, layer), _layer(w_o_lru, layer), _layer(w_out, layer),
                    pl.BlockSpec(final_g.shape, lambda i: (0, 0), pipeline_mode=pl.Buffered(1))],
        out_specs=rows(dm),
        out_shape=jax.ShapeDtypeStruct((t, dm), F32),
        compiler_params=pltpu.CompilerParams(dimension_semantics=("parallel",)),
        name="merge_final" if final else "merge",
    )(*outs, *lses, h_f, h_b, x2d, norm_g, w_in, w_in, b_gate, w_o_att, w_o_lru, w_out, final_g)


def kernel(x, norm_g, w_in, b_gate, conv_w, conv_b, rg_w_a, rg_b_a, rg_w_x, rg_b_x, rg_lam,
           w_o_att, w_o_lru, w_out, final_g):
    b, s, dm = x.shape
    depth = w_in.shape[0]
    t = b * s
    w_in, rg_w_a, rg_w_x, w_o_att, w_o_lru, w_out = (
        w.astype(BF16) for w in (w_in, rg_w_a, rg_w_x, w_o_att, w_o_lru, w_out))
    norm_g, b_gate, conv_b = (v.reshape(depth, 1, -1) for v in (norm_g, b_gate, conv_b))
    final_g = final_g.reshape(1, dm)

    x2d = x.reshape(t, dm)
    for layer in range(depth):
        nat2d, qkv4, qkv16 = _norm_proj(x2d, norm_g, w_in, layer, b)
        nat = nat2d.reshape(b, s, -1)
        outs, lses = [], []
        for group, qkv in enumerate((nat, qkv4, qkv16)):
            o, lse = _attention_group(qkv, group, ATT_PATTERNS[group][1])
            outs.append(o.reshape(t, GROUP_WIDTH))
            lses.append(lse.reshape(t, LANES))
        h_f, h_b = _lru(nat, conv_w, conv_b, rg_w_a, rg_b_a, rg_w_x, rg_b_x, rg_lam, layer)
        x2d = _merge(x2d, outs, lses, h_f.reshape(t, dm), h_b.reshape(t, dm), norm_g, w_in, b_gate,
                     w_o_att, w_o_lru, w_out, final_g, layer, final=(layer == depth - 1))
    return x2d.reshape(b, s, dm)
```

```python
import functools

import jax
import jax.numpy as jnp
from jax import lax
from jax.experimental import pallas as pl
from jax.experimental.pallas import tpu as pltpu

F32 = jnp.float32
BF16 = jnp.bfloat16

HEAD_DIM = 128
HEADS_PER_GROUP = 4
ATT_PATTERNS = ((128, 1), (512, 4), (2048, 16))
N_GROUPS = len(ATT_PATTERNS)
N_ATT_HEADS = N_GROUPS * HEADS_PER_GROUP
GROUP_WIDTH = HEADS_PER_GROUP * HEAD_DIM
QKV_WIDTH = 3 * GROUP_WIDTH
NEG_INF = -1e30
LRU_BLOCKS = 4
RG_C = 8.0
NORM_EPS = 1e-6
CONV_LEFT = 2

XLRU_COL = 0
NAT_QKV_COL = 2

LANES = 128
F32_ROWS = 8
BF16_ROWS = 16
HALF_KEYS = 64
Q_SUB = 128
K_SUB = Q_SUB + 2 * HALF_KEYS
LSE_REP = LANES // HEADS_PER_GROUP
SEGMENTS = F32_ROWS

PROJ_ROWS = 1024
PROJ_COLS = 512
ATTN_Q_ROWS = {1: 2048, 4: 1024, 16: 512}
LRU_ROWS = 1024
MERGE_ROWS = 1024
MERGE_SUB_ROWS = 256

LOG2_E = 1.4426950408889634
LN_2 = 0.6931471805599453


def _sigmoid(x):
    return 1.0 / (1.0 + jnp.exp2(x * -LOG2_E))


def _rms_norm(x, g):
    ms = jnp.mean(x * x, axis=-1, keepdims=True)
    return x * lax.rsqrt(ms + NORM_EPS) * g


def _col_block(first_col, cols):
    assert first_col % cols == 0, (first_col, cols)
    return first_col // cols


def _layer(stacked, layer, cols=None, col_block=0):
    block = stacked.shape[1:] if cols is None else stacked.shape[1:-1] + (cols,)
    index = (layer,) + (0,) * (len(block) - 1) + (col_block,)
    return pl.BlockSpec((None,) + block, lambda *_: index, pipeline_mode=pl.Buffered(1))


def _norm_proj_body(x_ref, g_ref, wqkv_ref, wx_ref, nat_ref, s4_ref, s16_ref, xn_ref, xn4_ref, xn16_ref, slab_ref):
    tm, dm = x_ref.shape
    width = wx_ref.shape[1]

    y = _rms_norm(x_ref[...], g_ref[...])
    xn_ref[...] = y.astype(BF16)
    for c in range(dm // LANES):
        slab_ref[c] = y[:, c * LANES:(c + 1) * LANES]
    for ref, d in ((xn4_ref, 4), (xn16_ref, 16)):
        rows = tm // d
        for c in range(dm // LANES):
            for r in range(d):
                ref[r * rows:(r + 1) * rows, c * LANES:(c + 1) * LANES] = (
                    slab_ref[c, pl.ds(r, rows, stride=d), :].astype(BF16))

    def project(lhs_ref, w_ref, col, cols):
        return jnp.dot(lhs_ref[...], w_ref[:, col:col + cols], preferred_element_type=F32).astype(BF16)

    for n0 in range(0, width, PROJ_COLS):
        nat_ref[:, n0:n0 + PROJ_COLS] = project(xn_ref, wx_ref, n0, PROJ_COLS)
    gw = GROUP_WIDTH
    for part in range(3):
        src = part * N_GROUPS * gw
        dst = part * gw
        nat_ref[:, width + dst:width + dst + gw] = project(xn_ref, wqkv_ref, src, gw)
        s4_ref[0, :, :, dst:dst + gw] = project(xn4_ref, wqkv_ref, src + gw, gw).reshape(4, tm // 4, gw)
        s16_ref[0, :, :, dst:dst + gw] = project(xn16_ref, wqkv_ref, src + 2 * gw, gw).reshape(16, tm // 16, gw)


def _norm_proj(x2d, norm_g, w_in, layer, batch):
    t, dm = x2d.shape
    s = t // batch
    n_nat = dm + QKV_WIDTH
    tm = PROJ_ROWS
    tiles_per_b = s // tm
    x_lru_block = _col_block(N_GROUPS * QKV_WIDTH + GROUP_WIDTH, dm)

    def strided_spec(d):
        return pl.BlockSpec((1, d, tm // d, QKV_WIDTH), lambda i: (i // tiles_per_b, 0, i % tiles_per_b, 0))

    return pl.pallas_call(
        _norm_proj_body,
        grid=(t // tm,),
        in_specs=[pl.BlockSpec((tm, dm), lambda i: (i, 0)), _layer(norm_g, layer),
                  _layer(w_in, layer, cols=N_GROUPS * QKV_WIDTH), _layer(w_in, layer, cols=dm, col_block=x_lru_block)],
        out_specs=[pl.BlockSpec((tm, n_nat), lambda i: (i, 0)), strided_spec(4), strided_spec(16)],
        out_shape=[jax.ShapeDtypeStruct((t, n_nat), BF16),
                   jax.ShapeDtypeStruct((batch, 4, s // 4, QKV_WIDTH), BF16),
                   jax.ShapeDtypeStruct((batch, 16, s // 16, QKV_WIDTH), BF16)],
        scratch_shapes=[pltpu.VMEM((tm, dm), BF16), pltpu.VMEM((tm, dm), BF16), pltpu.VMEM((tm, dm), BF16),
                        pltpu.VMEM((dm // LANES, tm, LANES), F32)],
        compiler_params=pltpu.CompilerParams(dimension_semantics=("parallel",)),
        name="norm_proj",
    )(x2d, norm_g, w_in, w_in)


def _attn_body(q_ref, kp_ref, kc_ref, kn_ref, vp_ref, vc_ref, vn_ref, o_ref, lse_ref,
               kcat, vcat, bias, *o_slab, group, dilation, tq, n_sub):
    qi = pl.program_id(1)
    r = pl.program_id(2)
    n_q = pl.num_programs(1)
    blk = (0,) if dilation == 1 else (0, 0)
    kcol = lax.broadcasted_iota(jnp.int32, (Q_SUB, K_SUB), 1)
    slopes = [2.0 ** (-8.0 * (group * HEADS_PER_GROUP + h + 1) / N_ATT_HEADS) for h in range(HEADS_PER_GROUP)]

    @pl.when((pl.program_id(0) == 0) & (qi == 0) & (r == 0))
    def _():
        qrow = lax.broadcasted_iota(jnp.int32, (Q_SUB, K_SUB), 0)
        absd = jnp.abs(kcol - HALF_KEYS - qrow)
        dist = (absd * dilation).astype(F32)
        for h in range(HEADS_PER_GROUP):
            bias[h] = jnp.where(absd <= HALF_KEYS, (-slopes[h] * LOG2_E) * dist, NEG_INF)
            vcat[:, (2 * h + 1) * HEAD_DIM:(2 * h + 2) * HEAD_DIM] = jnp.ones((tq + 2 * HALF_KEYS, HEAD_DIM), BF16)

    kcat[0:HALF_KEYS] = kp_ref[blk]
    kcat[HALF_KEYS:HALF_KEYS + tq] = kc_ref[blk]
    kcat[HALF_KEYS + tq:] = kn_ref[blk]
    for h in range(HEADS_PER_GROUP):
        src = slice(h * HEAD_DIM, (h + 1) * HEAD_DIM)
        dst = slice(2 * h * HEAD_DIM, (2 * h + 1) * HEAD_DIM)
        vcat[0:HALF_KEYS, dst] = vp_ref[blk + (slice(None), src)]
        vcat[HALF_KEYS:HALF_KEYS + tq, dst] = vc_ref[blk + (slice(None), src)]
        vcat[HALF_KEYS + tq:, dst] = vn_ref[blk + (slice(None), src)]

    lane = lax.broadcasted_iota(jnp.int32, (Q_SUB, LANES), 1)
    n_sub_tiles = tq // Q_SUB
    for sub in range(n_sub_tiles):
        r0 = sub * Q_SUB
        lse_tile = jnp.zeros((Q_SUB, LANES), F32)
        for h in range(HEADS_PER_GROUP):
            c0 = h * HEAD_DIM
            q = q_ref[blk + (slice(r0, r0 + Q_SUB), slice(c0, c0 + HEAD_DIM))]
            k = kcat[r0:r0 + K_SUB, c0:c0 + HEAD_DIM]
            raw = lax.dot_general(q, k, (((1,), (1,)), ((), ())), preferred_element_type=F32)
            s2 = raw * (HEAD_DIM ** -0.5 * LOG2_E) + bias[h]
            if sub == 0:
                s2 = jnp.where((qi > 0) | (kcol >= HALF_KEYS), s2, NEG_INF)
            if sub == n_sub_tiles - 1:
                s2 = jnp.where((qi < n_q - 1) | (kcol < HALF_KEYS + Q_SUB), s2, NEG_INF)
            m2 = jnp.max(s2, axis=-1, keepdims=True)
            p = jnp.exp2(s2 - m2).astype(BF16)
            acc = jnp.dot(p, vcat[r0:r0 + K_SUB, 2 * c0:2 * c0 + 2 * HEAD_DIM], preferred_element_type=F32)
            den = acc[:, HEAD_DIM:]
            o = acc[:, :HEAD_DIM] / den
            if dilation == 1:
                o_ref[0, r0:r0 + Q_SUB, c0:c0 + HEAD_DIM] = o.astype(BF16)
            else:
                o_slab[0][h, pl.ds(r0 * dilation + r, Q_SUB, stride=dilation), :] = o
            lse = (m2 + jnp.log2(den)) * LN_2
            lse_tile = jnp.where((lane >= h * LSE_REP) & (lane < (h + 1) * LSE_REP), lse, lse_tile)
        if dilation == 1:
            lse_ref[0, r0:r0 + Q_SUB, :] = lse_tile
        else:
            lse_ref[0, pl.ds(r0 * dilation + r, Q_SUB, stride=dilation), :] = lse_tile

    if dilation > 1:
        @pl.when(r == dilation - 1)
        def _():
            for h in range(HEADS_PER_GROUP):
                o_ref[0, :, h * HEAD_DIM:(h + 1) * HEAD_DIM] = o_slab[0][h].astype(BF16)


def _attention_group(qkv, group, dilation):
    if dilation == 1:
        b, s, _ = qkv.shape
    else:
        b, _, n_sub, _ = qkv.shape
        s = n_sub * dilation
    n_sub = s // dilation
    tq = min(ATTN_Q_ROWS[dilation], n_sub)
    q_col = NAT_QKV_COL if dilation == 1 else 0
    halo_per_tile = tq // HALF_KEYS
    last_halo = n_sub // HALF_KEYS - 1

    def spec(rows, row_index, col):
        if dilation == 1:
            return pl.BlockSpec((1, rows, GROUP_WIDTH), lambda bi, qi, r: (bi, row_index(qi), col))
        return pl.BlockSpec((1, 1, rows, GROUP_WIDTH), lambda bi, qi, r: (bi, r, row_index(qi), col))

    def main_spec(col):
        return spec(tq, lambda qi: qi, col)

    def prev_spec(col):
        return spec(HALF_KEYS, lambda qi: jnp.maximum(qi * halo_per_tile - 1, 0), col)

    def next_spec(col):
        return spec(HALF_KEYS, lambda qi: jnp.minimum((qi + 1) * halo_per_tile, last_halo), col)

    scratch = [pltpu.VMEM((tq + 2 * HALF_KEYS, GROUP_WIDTH), BF16),
               pltpu.VMEM((tq + 2 * HALF_KEYS, 2 * GROUP_WIDTH), BF16),
               pltpu.VMEM((HEADS_PER_GROUP, Q_SUB, K_SUB), F32)]
    if dilation > 1:
        scratch.append(pltpu.VMEM((HEADS_PER_GROUP, tq * dilation, LANES), F32))

    return pl.pallas_call(
        functools.partial(_attn_body, group=group, dilation=dilation, tq=tq, n_sub=n_sub),
        grid=(b, n_sub // tq, dilation),
        in_specs=[main_spec(q_col),
                  prev_spec(q_col + 1), main_spec(q_col + 1), next_spec(q_col + 1),
                  prev_spec(q_col + 2), main_spec(q_col + 2), next_spec(q_col + 2)],
        out_specs=[pl.BlockSpec((1, tq * dilation, GROUP_WIDTH), lambda bi, qi, r: (bi, qi, 0)),
                   pl.BlockSpec((1, tq * dilation, LANES), lambda bi, qi, r: (bi, qi, 0))],
        out_shape=[jax.ShapeDtypeStruct((b, s, GROUP_WIDTH), BF16),
                   jax.ShapeDtypeStruct((b, s, LANES), F32)],
        scratch_shapes=scratch,
        compiler_params=pltpu.CompilerParams(dimension_semantics=("arbitrary", "arbitrary", "arbitrary")),
        name=f"attn_d{dilation}",
    )(qkv, qkv, qkv, qkv, qkv, qkv, qkv)


def _lru_direction(xp_ref, xc_ref, xn_ref, h_ref, tile, n_tiles, rev, dirn,
                   cw_ref, cb_ref, wa_ref, ba_ref, wx_ref, bx_ref, lam_ref,
                   xpad, a_s, b_s, hpad, carry):
    tt, width = xc_ref.shape[1:]
    bw = width // LRU_BLOCKS
    n_chunks = width // LANES
    seg_len = tt // SEGMENTS
    pitch = seg_len + SEGMENTS
    sub = lax.broadcasted_iota(jnp.int32, (SEGMENTS, width), 0)

    xf = xc_ref[0].astype(F32)
    for c in range(n_chunks):
        for k in range(SEGMENTS):
            xpad[c, k * pitch:k * pitch + seg_len, :] = xf[k * seg_len:(k + 1) * seg_len, c * LANES:(c + 1) * LANES]
    xs = jnp.concatenate(
        [jnp.concatenate([xpad[c, pl.ds(j, SEGMENTS, stride=pitch), :] for j in range(seg_len)], axis=0)
         for c in range(n_chunks)], axis=1)

    halo_prev = jnp.where(tile == 0, 0.0, xp_ref[0, BF16_ROWS - CONV_LEFT:BF16_ROWS, :].astype(F32))
    halo_next = jnp.where(tile == n_tiles - 1, 0.0, xn_ref[0, 0:1, :].astype(F32))
    before1 = jnp.where(sub == 0, halo_prev[1:2], pltpu.roll(xs[tt - SEGMENTS:tt], 1, 0))
    before2 = jnp.where(sub == 0, halo_prev[0:1], pltpu.roll(xs[tt - 2 * SEGMENTS:tt - SEGMENTS], 1, 0))
    after1 = jnp.where(sub == SEGMENTS - 1, halo_next, pltpu.roll(xs[0:SEGMENTS], SEGMENTS - 1, 0))
    xm2 = jnp.concatenate([before2, before1, xs[:tt - 2 * SEGMENTS]], axis=0)
    xm1 = jnp.concatenate([before1, xs[:tt - SEGMENTS]], axis=0)
    xp1 = jnp.concatenate([xs[SEGMENTS:], after1], axis=0)
    xc = cb_ref[...] + xm2 * cw_ref[0:1] + xm1 * cw_ref[1:2] + xs * cw_ref[2:3] + xp1 * cw_ref[3:4]
    xcb = xc.astype(BF16)

    lam = lam_ref[dirn:dirn + 1]
    neg_lam = -lam
    softplus = jnp.maximum(neg_lam, 0.0) + jnp.log1p(jnp.exp(-jnp.abs(neg_lam)))
    coef = -RG_C * softplus

    sub_blk = lax.broadcasted_iota(jnp.int32, (SEGMENTS, bw), 0)
    if rev:
        start_rows = slice(tt - SEGMENTS, tt)
        first = (tile == n_tiles - 1) & (sub_blk == SEGMENTS - 1)
    else:
        start_rows = slice(0, SEGMENTS)
        first = (tile == 0) & (sub_blk == 0)

    for blk in range(LRU_BLOCKS):
        c0 = blk * bw
        xblk = xcb[:, c0:c0 + bw]
        pa = jnp.dot(xblk, wa_ref[dirn, blk], preferred_element_type=F32) + ba_ref[dirn:dirn + 1, c0:c0 + bw]
        px = jnp.dot(xblk, wx_ref[dirn, blk], preferred_element_type=F32) + bx_ref[dirn:dirn + 1, c0:c0 + bw]
        log_a = coef[:, c0:c0 + bw] * _sigmoid(pa)
        a = jnp.exp(log_a)
        u = jnp.tanh(log_a) * (-1.0 - a * a)
        mult = jnp.where(u > 0.0, u * lax.rsqrt(u), 0.0)
        gated = _sigmoid(px) * xc[:, c0:c0 + bw]
        b = mult * gated
        a_s[:, c0:c0 + bw] = a
        b_s[:, c0:c0 + bw] = b
        b_s[start_rows, c0:c0 + bw] = jnp.where(first, gated[start_rows], b[start_rows])

    steps = range(seg_len - 1, -1, -1) if rev else range(seg_len)
    segs = range(SEGMENTS - 1, -1, -1) if rev else range(SEGMENTS)
    sub1 = lax.broadcasted_iota(jnp.int32, (SEGMENTS, LANES), 0)
    for c in range(n_chunks):
        cs = slice(c * LANES, (c + 1) * LANES)
        prod = jnp.ones((SEGMENTS, LANES), F32)
        h = jnp.zeros((SEGMENTS, LANES), F32)
        for j in steps:
            aj = a_s[j * SEGMENTS:(j + 1) * SEGMENTS, cs]
            h = aj * h + b_s[j * SEGMENTS:(j + 1) * SEGMENTS, cs]
            prod = aj * prod
        state = carry[dirn:dirn + 1, cs]
        h0 = jnp.zeros((SEGMENTS, LANES), F32)
        for k in segs:
            h0 = jnp.where(sub1 == k, state, h0)
            state = prod[k:k + 1] * state + h[k:k + 1]
        carry[dirn:dirn + 1, cs] = state
        h = h0
        for j in steps:
            h = a_s[j * SEGMENTS:(j + 1) * SEGMENTS, cs] * h + b_s[j * SEGMENTS:(j + 1) * SEGMENTS, cs]
            hpad[c, pl.ds(j, SEGMENTS, stride=pitch), :] = h
        for k in range(SEGMENTS):
            h_ref[0, k * seg_len:(k + 1) * seg_len, cs] = hpad[c, k * pitch:k * pitch + seg_len, :].astype(h_ref.dtype)


def _lru_body(xfp_ref, xfc_ref, xfn_ref, xbp_ref, xbc_ref, xbn_ref,
              cw_ref, cb_ref, wa_ref, ba_ref, wx_ref, bx_ref, lam_ref,
              hf_ref, hb_ref, xpad_f, a_f, b_f, hpad_f, xpad_b, a_b, b_b, hpad_b, carry):
    i = pl.program_id(1)
    n_tiles = pl.num_programs(1)

    @pl.when(i == 0)
    def _():
        carry[...] = jnp.zeros_like(carry)

    params = (cw_ref, cb_ref, wa_ref, ba_ref, wx_ref, bx_ref, lam_ref)
    _lru_direction(xfp_ref, xfc_ref, xfn_ref, hf_ref, i, n_tiles, False, 0, *params,
                   xpad_f, a_f, b_f, hpad_f, carry)
    _lru_direction(xbp_ref, xbc_ref, xbn_ref, hb_ref, n_tiles - 1 - i, n_tiles, True, 1, *params,
                   xpad_b, a_b, b_b, hpad_b, carry)


def _lru(nat, conv_w, conv_b, w_a, b_a, w_x, b_x, lam, layer):
    b, s, _ = nat.shape
    width = conv_w.shape[-1]
    tt = LRU_ROWS
    n_tiles = s // tt
    halo_per_tile = tt // BF16_ROWS
    last_halo = s // BF16_ROWS - 1

    def tile_of(i, rev):
        return n_tiles - 1 - i if rev else i

    def specs(rev):
        return [
            pl.BlockSpec((1, BF16_ROWS, width),
                         lambda bi, i: (bi, jnp.maximum(tile_of(i, rev) * halo_per_tile - 1, 0), XLRU_COL)),
            pl.BlockSpec((1, tt, width), lambda bi, i: (bi, tile_of(i, rev), XLRU_COL)),
            pl.BlockSpec((1, BF16_ROWS, width),
                         lambda bi, i: (bi, jnp.minimum((tile_of(i, rev) + 1) * halo_per_tile, last_halo), XLRU_COL)),
        ]

    consts = (conv_w, conv_b, w_a, b_a, w_x, b_x, lam)
    padded_rows = tt + SEGMENTS * SEGMENTS
    per_direction = [pltpu.VMEM((width // LANES, padded_rows, LANES), F32),
                     pltpu.VMEM((tt, width), F32),
                     pltpu.VMEM((tt, width), F32),
                     pltpu.VMEM((width // LANES, padded_rows, LANES), F32)]
    return pl.pallas_call(
        _lru_body,
        grid=(b, n_tiles),
        in_specs=specs(False) + specs(True) + [_layer(a, layer) for a in consts],
        out_specs=[pl.BlockSpec((1, tt, width), lambda bi, i: (bi, i, 0)),
                   pl.BlockSpec((1, tt, width), lambda bi, i: (bi, n_tiles - 1 - i, 0))],
        out_shape=[jax.ShapeDtypeStruct((b, s, width), BF16)] * 2,
        scratch_shapes=per_direction + per_direction + [pltpu.VMEM((F32_ROWS, width), F32)],
        compiler_params=pltpu.CompilerParams(dimension_semantics=("parallel", "arbitrary")),
        name="lru",
    )(nat, nat, nat, nat, nat, nat, *consts)


def _merge_body(o0_ref, o1_ref, o2_ref, l0_ref, l1_ref, l2_ref, hf_ref, hb_ref, x_ref,
                ng_ref, wga_ref, wgo_ref, bg_ref, woa_ref, wol_ref, wout_ref, fg_ref, out_ref, *, final):
    tm, dm = x_ref.shape
    o_refs = (o0_ref, o1_ref, o2_ref)
    for r0 in range(0, tm, MERGE_SUB_ROWS):
        rows = slice(r0, r0 + MERGE_SUB_ROWS)
        xn = _rms_norm(x_ref[rows], ng_ref[...]).astype(BF16)

        gatt = jnp.dot(xn, wga_ref[...], preferred_element_type=F32)
        glru, logit_att, logit_lru = (
            jnp.dot(xn, wgo_ref[:, c0:c0 + dm], preferred_element_type=F32) for c0 in (0, dm, 2 * dm))

        l0, l1, l2 = l0_ref[rows], l1_ref[rows], l2_ref[rows]
        mx = jnp.maximum(jnp.maximum(l0, l1), l2)
        e0, e1, e2 = jnp.exp(l0 - mx), jnp.exp(l1 - mx), jnp.exp(l2 - mx)
        inv = 1.0 / (e0 + e1 + e2)
        alphas = (e0 * inv, e1 * inv, e2 * inv)

        heads = []
        for h in range(HEADS_PER_GROUP):
            c0 = h * HEAD_DIM
            acc = jnp.zeros((MERGE_SUB_ROWS, HEAD_DIM), F32)
            for g in range(N_GROUPS):
                w = jnp.broadcast_to(alphas[g][:, h * LSE_REP:h * LSE_REP + 1], (MERGE_SUB_ROWS, HEAD_DIM))
                acc = acc + w * o_refs[g][rows, c0:c0 + HEAD_DIM].astype(F32)
            heads.append(acc)
        mixed = jnp.concatenate(heads, axis=-1)

        y_att = (mixed * (gatt * _sigmoid(gatt))).astype(BF16)
        h_sum = hf_ref[rows].astype(F32) + hb_ref[rows].astype(F32)
        y_lru = (h_sum * (glru * _sigmoid(glru))).astype(BF16)

        p_att = jnp.dot(y_att, woa_ref[...], preferred_element_type=F32)
        p_lru = jnp.dot(y_lru, wol_ref[...], preferred_element_type=F32)
        gate_att = _sigmoid(logit_att + bg_ref[:, 0:dm])
        gate_lru = _sigmoid(logit_lru + bg_ref[:, dm:2 * dm])
        merged = (gate_att * p_att + gate_lru * p_lru).astype(BF16)
        y = x_ref[rows] + jnp.dot(merged, wout_ref[...], preferred_element_type=F32)
        if final:
            y = _rms_norm(y, fg_ref[...])
        out_ref[rows] = y


def _merge(x2d, outs, lses, h_f, h_b, norm_g, w_in, b_gate, w_o_att, w_o_lru, w_out, final_g, layer, final):
    t, dm = x2d.shape
    tm = MERGE_ROWS
    att_gate_block = _col_block(N_GROUPS * QKV_WIDTH, GROUP_WIDTH)
    others_block = _col_block(N_GROUPS * QKV_WIDTH + GROUP_WIDTH + dm, 3 * dm)

    def rows(width):
        return pl.BlockSpec((tm, width), lambda i: (i, 0))

    return pl.pallas_call(
        functools.partial(_merge_body, final=final),
        grid=(t // tm,),
        in_specs=[rows(GROUP_WIDTH)] * 3 + [rows(LANES)] * 3 + [rows(dm)] * 3
                 + [_layer(norm_g, layer), _layer(w_in, layer, cols=GROUP_WIDTH, col_block=att_gate_block),
                    _layer(w_in, layer, cols=3 * dm, col_block=others_block), _layer(b_gate, layer),
                    _layer(w_o_att, layer), _layer(w_o_lru, layer), _layer(w_out, layer),
                    pl.BlockSpec(final_g.shape, lambda i: (0, 0), pipeline_mode=pl.Buffered(1))],
        out_specs=rows(dm),
        out_shape=jax.ShapeDtypeStruct((t, dm), F32),
        compiler_params=pltpu.CompilerParams(dimension_semantics=("parallel",)),
        name="merge_final" if final else "merge",
    )(*outs, *lses, h_f, h_b, x2d, norm_g, w_in, w_in, b_gate, w_o_att, w_o_lru, w_out, final_g)


def kernel(x, norm_g, w_in, b_gate, conv_w, conv_b, rg_w_a, rg_b_a, rg_w_x, rg_b_x, rg_lam,
           w_o_att, w_o_lru, w_out, final_g):
    b, s, dm = x.shape
    depth = w_in.shape[0]
    t = b * s
    w_in, rg_w_a, rg_w_x, w_o_att, w_o_lru, w_out = (
        w.astype(BF16) for w in (w_in, rg_w_a, rg_w_x, w_o_att, w_o_lru, w_out))
    norm_g, b_gate, conv_b = (v.reshape(depth, 1, -1) for v in (norm_g, b_gate, conv_b))
    final_g = final_g.reshape(1, dm)

    x2d = x.reshape(t, dm)
    for layer in range(depth):
        nat2d, qkv4, qkv16 = _norm_proj(x2d, norm_g, w_in, layer, b)
        nat = nat2d.reshape(b, s, -1)
        outs, lses = [], []
        for group, qkv in enumerate((nat, qkv4, qkv16)):
            o, lse = _attention_group(qkv, group, ATT_PATTERNS[group][1])
            outs.append(o.reshape(t, GROUP_WIDTH))
            lses.append(lse.reshape(t, LANES))
        h_f, h_b = _lru(nat, conv_w, conv_b, rg_w_a, rg_b_a, rg_w_x, rg_b_x, rg_lam, layer)
        x2d = _merge(x2d, outs, lses, h_f.reshape(t, dm), h_b.reshape(t, dm), norm_g, w_in, b_gate,
                     w_o_att, w_o_lru, w_out, final_g, layer, final=(layer == depth - 1))
    return x2d.reshape(b, s, dm)
```

```python
import functools

import jax
import jax.numpy as jnp
from jax import lax
from jax.experimental import pallas as pl
from jax.experimental.pallas import tpu as pltpu

F32 = jnp.float32
BF16 = jnp.bfloat16

HEAD_DIM = 128
HEADS_PER_GROUP = 4
ATT_PATTERNS = ((128, 1), (512, 4), (2048, 16))
N_GROUPS = len(ATT_PATTERNS)
N_ATT_HEADS = N_GROUPS * HEADS_PER_GROUP
GROUP_WIDTH = HEADS_PER_GROUP * HEAD_DIM
QKV_WIDTH = 3 * GROUP_WIDTH
NEG_INF = -1e30
LRU_BLOCKS = 4
RG_C = 8.0
NORM_EPS = 1e-6
CONV_LEFT = 2

XLRU_COL = 0
NAT_QKV_COL = 2

LANES = 128
F32_ROWS = 8
BF16_ROWS = 16
HALF_KEYS = 64
Q_SUB = 128
K_SUB = Q_SUB + 2 * HALF_KEYS
LSE_REP = LANES // HEADS_PER_GROUP
SEGMENTS = F32_ROWS

PROJ_ROWS = 1024
PROJ_COLS = 512
ATTN_Q_ROWS = {1: 2048, 4: 2048, 16: 512}
LRU_ROWS = 1024
MERGE_ROWS = 1024
MERGE_SUB_ROWS = 256

LOG2_E = 1.4426950408889634
LN_2 = 0.6931471805599453


def _sigmoid(x):
    return 1.0 / (1.0 + jnp.exp2(x * -LOG2_E))


def _rms_norm(x, g):
    ms = jnp.mean(x * x, axis=-1, keepdims=True)
    return x * lax.rsqrt(ms + NORM_EPS) * g


def _col_block(first_col, cols):
    assert first_col % cols == 0, (first_col, cols)
    return first_col // cols


def _layer(stacked, layer, cols=None, col_block=0):
    block = stacked.shape[1:] if cols is None else stacked.shape[1:-1] + (cols,)
    index = (layer,) + (0,) * (len(block) - 1) + (col_block,)
    return pl.BlockSpec((None,) + block, lambda *_: index, pipeline_mode=pl.Buffered(1))


def _norm_proj_body(x_ref, g_ref, wqkv_ref, wx_ref, nat_ref, s4_ref, s16_ref, xn_ref, xn4_ref, xn16_ref, slab_ref):
    tm, dm = x_ref.shape
    width = wx_ref.shape[1]

    y = _rms_norm(x_ref[...], g_ref[...])
    xn_ref[...] = y.astype(BF16)
    for c in range(dm // LANES):
        slab_ref[c] = y[:, c * LANES:(c + 1) * LANES]
    for ref, d in ((xn4_ref, 4), (xn16_ref, 16)):
        rows = tm // d
        for c in range(dm // LANES):
            for r in range(d):
                ref[r * rows:(r + 1) * rows, c * LANES:(c + 1) * LANES] = (
                    slab_ref[c, pl.ds(r, rows, stride=d), :].astype(BF16))

    def project(lhs_ref, w_ref, col, cols):
        return jnp.dot(lhs_ref[...], w_ref[:, col:col + cols], preferred_element_type=F32).astype(BF16)

    for n0 in range(0, width, PROJ_COLS):
        nat_ref[:, n0:n0 + PROJ_COLS] = project(xn_ref, wx_ref, n0, PROJ_COLS)
    gw = GROUP_WIDTH
    for part in range(3):
        src = part * N_GROUPS * gw
        dst = part * gw
        nat_ref[:, width + dst:width + dst + gw] = project(xn_ref, wqkv_ref, src, gw)
        s4_ref[0, :, :, dst:dst + gw] = project(xn4_ref, wqkv_ref, src + gw, gw).reshape(4, tm // 4, gw)
        s16_ref[0, :, :, dst:dst + gw] = project(xn16_ref, wqkv_ref, src + 2 * gw, gw).reshape(16, tm // 16, gw)


def _norm_proj(x2d, norm_g, w_in, layer, batch):
    t, dm = x2d.shape
    s = t // batch
    n_nat = dm + QKV_WIDTH
    tm = PROJ_ROWS
    tiles_per_b = s // tm
    x_lru_block = _col_block(N_GROUPS * QKV_WIDTH + GROUP_WIDTH, dm)

    def strided_spec(d):
        return pl.BlockSpec((1, d, tm // d, QKV_WIDTH), lambda i: (i // tiles_per_b, 0, i % tiles_per_b, 0))

    return pl.pallas_call(
        _norm_proj_body,
        grid=(t // tm,),
        in_specs=[pl.BlockSpec((tm, dm), lambda i: (i, 0)), _layer(norm_g, layer),
                  _layer(w_in, layer, cols=N_GROUPS * QKV_WIDTH), _layer(w_in, layer, cols=dm, col_block=x_lru_block)],
        out_specs=[pl.BlockSpec((tm, n_nat), lambda i: (i, 0)), strided_spec(4), strided_spec(16)],
        out_shape=[jax.ShapeDtypeStruct((t, n_nat), BF16),
                   jax.ShapeDtypeStruct((batch, 4, s // 4, QKV_WIDTH), BF16),
                   jax.ShapeDtypeStruct((batch, 16, s // 16, QKV_WIDTH), BF16)],
        scratch_shapes=[pltpu.VMEM((tm, dm), BF16), pltpu.VMEM((tm, dm), BF16), pltpu.VMEM((tm, dm), BF16),
                        pltpu.VMEM((dm // LANES, tm, LANES), F32)],
        compiler_params=pltpu.CompilerParams(dimension_semantics=("parallel",)),
        name="norm_proj",
    )(x2d, norm_g, w_in, w_in)


def _attn_body(q_ref, kp_ref, kc_ref, kn_ref, vp_ref, vc_ref, vn_ref, o_ref, lse_ref,
               kcat, vcat, bias, *, group, dilation, tq):
    qi = pl.program_id(1)
    r = pl.program_id(2)
    n_q = pl.num_programs(1)
    blk = (0,) if dilation == 1 else (0, 0)
    kcol = lax.broadcasted_iota(jnp.int32, (Q_SUB, K_SUB), 1)
    slopes = [2.0 ** (-8.0 * (group * HEADS_PER_GROUP + h + 1) / N_ATT_HEADS) for h in range(HEADS_PER_GROUP)]

    @pl.when((pl.program_id(0) == 0) & (qi == 0) & (r == 0))
    def _():
        qrow = lax.broadcasted_iota(jnp.int32, (Q_SUB, K_SUB), 0)
        absd = jnp.abs(kcol - HALF_KEYS - qrow)
        dist = (absd * dilation).astype(F32)
        for h in range(HEADS_PER_GROUP):
            bias[h] = jnp.where(absd <= HALF_KEYS, (-slopes[h] * LOG2_E) * dist, NEG_INF)
            vcat[:, (2 * h + 1) * HEAD_DIM:(2 * h + 2) * HEAD_DIM] = jnp.ones((tq + 2 * HALF_KEYS, HEAD_DIM), BF16)

    kcat[0:HALF_KEYS] = kp_ref[blk]
    kcat[HALF_KEYS:HALF_KEYS + tq] = kc_ref[blk]
    kcat[HALF_KEYS + tq:] = kn_ref[blk]
    for h in range(HEADS_PER_GROUP):
        src = slice(h * HEAD_DIM, (h + 1) * HEAD_DIM)
        dst = slice(2 * h * HEAD_DIM, (2 * h + 1) * HEAD_DIM)
        vcat[0:HALF_KEYS, dst] = vp_ref[blk + (slice(None), src)]
        vcat[HALF_KEYS:HALF_KEYS + tq, dst] = vc_ref[blk + (slice(None), src)]
        vcat[HALF_KEYS + tq:, dst] = vn_ref[blk + (slice(None), src)]

    lane = lax.broadcasted_iota(jnp.int32, (Q_SUB, LANES), 1)
    n_sub_tiles = tq // Q_SUB
    for sub in range(n_sub_tiles):
        r0 = sub * Q_SUB
        lse_tile = jnp.zeros((Q_SUB, LANES), F32)
        for h in range(HEADS_PER_GROUP):
            c0 = h * HEAD_DIM
            q = q_ref[blk + (slice(r0, r0 + Q_SUB), slice(c0, c0 + HEAD_DIM))]
            k = kcat[r0:r0 + K_SUB, c0:c0 + HEAD_DIM]
            raw = lax.dot_general(q, k, (((1,), (1,)), ((), ())), preferred_element_type=F32)
            s2 = raw * (HEAD_DIM ** -0.5 * LOG2_E) + bias[h]
            if sub == 0:
                s2 = jnp.where((qi > 0) | (kcol >= HALF_KEYS), s2, NEG_INF)
            if sub == n_sub_tiles - 1:
                s2 = jnp.where((qi < n_q - 1) | (kcol < HALF_KEYS + Q_SUB), s2, NEG_INF)
            m2 = jnp.max(s2, axis=-1, keepdims=True)
            p = jnp.exp2(s2 - m2).astype(BF16)
            acc = jnp.dot(p, vcat[r0:r0 + K_SUB, 2 * c0:2 * c0 + 2 * HEAD_DIM], preferred_element_type=F32)
            den = acc[:, HEAD_DIM:]
            o = acc[:, :HEAD_DIM] / den
            o_ref[blk + (slice(r0, r0 + Q_SUB), slice(c0, c0 + HEAD_DIM))] = o.astype(BF16)
            lse = (m2 + jnp.log2(den)) * LN_2
            lse_tile = jnp.where((lane >= h * LSE_REP) & (lane < (h + 1) * LSE_REP), lse, lse_tile)
        lse_ref[blk + (slice(r0, r0 + Q_SUB), slice(None))] = lse_tile


def _attention_group(qkv, group, dilation):
    if dilation == 1:
        b, s, _ = qkv.shape
    else:
        b, _, n_sub, _ = qkv.shape
        s = n_sub * dilation
    n_sub = s // dilation
    tq = min(ATTN_Q_ROWS[dilation], n_sub)
    q_col = NAT_QKV_COL if dilation == 1 else 0
    halo_per_tile = tq // HALF_KEYS
    last_halo = n_sub // HALF_KEYS - 1

    def spec(rows, row_index, col, width=GROUP_WIDTH):
        if dilation == 1:
            return pl.BlockSpec((1, rows, width), lambda bi, qi, r: (bi, row_index(qi), col))
        return pl.BlockSpec((1, 1, rows, width), lambda bi, qi, r: (bi, r, row_index(qi), col))

    def main_spec(col):
        return spec(tq, lambda qi: qi, col)

    def prev_spec(col):
        return spec(HALF_KEYS, lambda qi: jnp.maximum(qi * halo_per_tile - 1, 0), col)

    def next_spec(col):
        return spec(HALF_KEYS, lambda qi: jnp.minimum((qi + 1) * halo_per_tile, last_halo), col)

    rows_shape = (b, s) if dilation == 1 else (b, dilation, n_sub)
    return pl.pallas_call(
        functools.partial(_attn_body, group=group, dilation=dilation, tq=tq),
        grid=(b, n_sub // tq, dilation),
        in_specs=[main_spec(q_col),
                  prev_spec(q_col + 1), main_spec(q_col + 1), next_spec(q_col + 1),
                  prev_spec(q_col + 2), main_spec(q_col + 2), next_spec(q_col + 2)],
        out_specs=[main_spec(0), spec(tq, lambda qi: qi, 0, LANES)],
        out_shape=[jax.ShapeDtypeStruct(rows_shape + (GROUP_WIDTH,), BF16),
                   jax.ShapeDtypeStruct(rows_shape + (LANES,), F32)],
        scratch_shapes=[pltpu.VMEM((tq + 2 * HALF_KEYS, GROUP_WIDTH), BF16),
                        pltpu.VMEM((tq + 2 * HALF_KEYS, 2 * GROUP_WIDTH), BF16),
                        pltpu.VMEM((HEADS_PER_GROUP, Q_SUB, K_SUB), F32)],
        compiler_params=pltpu.CompilerParams(dimension_semantics=("arbitrary", "arbitrary", "arbitrary")),
        name=f"attn_d{dilation}",
    )(qkv, qkv, qkv, qkv, qkv, qkv, qkv)


def _lru_direction(xp_ref, xc_ref, xn_ref, h_ref, tile, n_tiles, rev, dirn,
                   cw_ref, cb_ref, wa_ref, ba_ref, wx_ref, bx_ref, lam_ref,
                   xpad, a_s, b_s, hpad, carry):
    tt, width = xc_ref.shape[1:]
    bw = width // LRU_BLOCKS
    n_chunks = width // LANES
    seg_len = tt // SEGMENTS
    pitch = seg_len + SEGMENTS
    sub = lax.broadcasted_iota(jnp.int32, (SEGMENTS, width), 0)

    xf = xc_ref[0].astype(F32)
    for c in range(n_chunks):
        for k in range(SEGMENTS):
            xpad[c, k * pitch:k * pitch + seg_len, :] = xf[k * seg_len:(k + 1) * seg_len, c * LANES:(c + 1) * LANES]
    xs = jnp.concatenate(
        [jnp.concatenate([xpad[c, pl.ds(j, SEGMENTS, stride=pitch), :] for j in range(seg_len)], axis=0)
         for c in range(n_chunks)], axis=1)

    halo_prev = jnp.where(tile == 0, 0.0, xp_ref[0, BF16_ROWS - CONV_LEFT:BF16_ROWS, :].astype(F32))
    halo_next = jnp.where(tile == n_tiles - 1, 0.0, xn_ref[0, 0:1, :].astype(F32))
    before1 = jnp.where(sub == 0, halo_prev[1:2], pltpu.roll(xs[tt - SEGMENTS:tt], 1, 0))
    before2 = jnp.where(sub == 0, halo_prev[0:1], pltpu.roll(xs[tt - 2 * SEGMENTS:tt - SEGMENTS], 1, 0))
    after1 = jnp.where(sub == SEGMENTS - 1, halo_next, pltpu.roll(xs[0:SEGMENTS], SEGMENTS - 1, 0))
    xm2 = jnp.concatenate([before2, before1, xs[:tt - 2 * SEGMENTS]], axis=0)
    xm1 = jnp.concatenate([before1, xs[:tt - SEGMENTS]], axis=0)
    xp1 = jnp.concatenate([xs[SEGMENTS:], after1], axis=0)
    xc = cb_ref[...] + xm2 * cw_ref[0:1] + xm1 * cw_ref[1:2] + xs * cw_ref[2:3] + xp1 * cw_ref[3:4]
    xcb = xc.astype(BF16)

    lam = lam_ref[dirn:dirn + 1]
    neg_lam = -lam
    softplus = jnp.maximum(neg_lam, 0.0) + jnp.log1p(jnp.exp(-jnp.abs(neg_lam)))
    coef = -RG_C * softplus

    sub_blk = lax.broadcasted_iota(jnp.int32, (SEGMENTS, bw), 0)
    if rev:
        start_rows = slice(tt - SEGMENTS, tt)
        first = (tile == n_tiles - 1) & (sub_blk == SEGMENTS - 1)
    else:
        start_rows = slice(0, SEGMENTS)
        first = (tile == 0) & (sub_blk == 0)

    for blk in range(LRU_BLOCKS):
        c0 = blk * bw
        xblk = xcb[:, c0:c0 + bw]
        pa = jnp.dot(xblk, wa_ref[dirn, blk], preferred_element_type=F32) + ba_ref[dirn:dirn + 1, c0:c0 + bw]
        px = jnp.dot(xblk, wx_ref[dirn, blk], preferred_element_type=F32) + bx_ref[dirn:dirn + 1, c0:c0 + bw]
        log_a = coef[:, c0:c0 + bw] * _sigmoid(pa)
        a = jnp.exp(log_a)
        u = jnp.tanh(log_a) * (-1.0 - a * a)
        mult = jnp.where(u > 0.0, u * lax.rsqrt(u), 0.0)
        gated = _sigmoid(px) * xc[:, c0:c0 + bw]
        b = mult * gated
        a_s[:, c0:c0 + bw] = a
        b_s[:, c0:c0 + bw] = b
        b_s[start_rows, c0:c0 + bw] = jnp.where(first, gated[start_rows], b[start_rows])

    steps = range(seg_len - 1, -1, -1) if rev else range(seg_len)
    segs = range(SEGMENTS - 1, -1, -1) if rev else range(SEGMENTS)
    sub1 = lax.broadcasted_iota(jnp.int32, (SEGMENTS, LANES), 0)
    for c in range(n_chunks):
        cs = slice(c * LANES, (c + 1) * LANES)
        prod = jnp.ones((SEGMENTS, LANES), F32)
        h = jnp.zeros((SEGMENTS, LANES), F32)
        for j in steps:
            aj = a_s[j * SEGMENTS:(j + 1) * SEGMENTS, cs]
            h = aj * h + b_s[j * SEGMENTS:(j + 1) * SEGMENTS, cs]
            prod = aj * prod
        state = carry[dirn:dirn + 1, cs]
        h0 = jnp.zeros((SEGMENTS, LANES), F32)
        for k in segs:
            h0 = jnp.where(sub1 == k, state, h0)
            state = prod[k:k + 1] * state + h[k:k + 1]
        carry[dirn:dirn + 1, cs] = state
        h = h0
        for j in steps:
            h = a_s[j * SEGMENTS:(j + 1) * SEGMENTS, cs] * h + b_s[j * SEGMENTS:(j + 1) * SEGMENTS, cs]
            hpad[c, pl.ds(j, SEGMENTS, stride=pitch), :] = h
        for k in range(SEGMENTS):
            h_ref[0, k * seg_len:(k + 1) * seg_len, cs] = hpad[c, k * pitch:k * pitch + seg_len, :].astype(h_ref.dtype)


def _lru_body(xfp_ref, xfc_ref, xfn_ref, xbp_ref, xbc_ref, xbn_ref,
              cw_ref, cb_ref, wa_ref, ba_ref, wx_ref, bx_ref, lam_ref,
              hf_ref, hb_ref, xpad_f, a_f, b_f, hpad_f, xpad_b, a_b, b_b, hpad_b, carry):
    i = pl.program_id(1)
    n_tiles = pl.num_programs(1)

    @pl.when(i == 0)
    def _():
        carry[...] = jnp.zeros_like(carry)

    params = (cw_ref, cb_ref, wa_ref, ba_ref, wx_ref, bx_ref, lam_ref)
    _lru_direction(xfp_ref, xfc_ref, xfn_ref, hf_ref, i, n_tiles, False, 0, *params,
                   xpad_f, a_f, b_f, hpad_f, carry)
    _lru_direction(xbp_ref, xbc_ref, xbn_ref, hb_ref, n_tiles - 1 - i, n_tiles, True, 1, *params,
                   xpad_b, a_b, b_b, hpad_b, carry)


def _lru(nat, conv_w, conv_b, w_a, b_a, w_x, b_x, lam, layer):
    b, s, _ = nat.shape
    width = conv_w.shape[-1]
    tt = LRU_ROWS
    n_tiles = s // tt
    halo_per_tile = tt // BF16_ROWS
    last_halo = s // BF16_ROWS - 1

    def tile_of(i, rev):
        return n_tiles - 1 - i if rev else i

    def specs(rev):
        return [
            pl.BlockSpec((1, BF16_ROWS, width),
                         lambda bi, i: (bi, jnp.maximum(tile_of(i, rev) * halo_per_tile - 1, 0), XLRU_COL)),
            pl.BlockSpec((1, tt, width), lambda bi, i: (bi, tile_of(i, rev), XLRU_COL)),
            pl.BlockSpec((1, BF16_ROWS, width),
                         lambda bi, i: (bi, jnp.minimum((tile_of(i, rev) + 1) * halo_per_tile, last_halo), XLRU_COL)),
        ]

    consts = (conv_w, conv_b, w_a, b_a, w_x, b_x, lam)
    padded_rows = tt + SEGMENTS * SEGMENTS
    per_direction = [pltpu.VMEM((width // LANES, padded_rows, LANES), F32),
                     pltpu.VMEM((tt, width), F32),
                     pltpu.VMEM((tt, width), F32),
                     pltpu.VMEM((width // LANES, padded_rows, LANES), F32)]
    return pl.pallas_call(
        _lru_body,
        grid=(b, n_tiles),
        in_specs=specs(False) + specs(True) + [_layer(a, layer) for a in consts],
        out_specs=[pl.BlockSpec((1, tt, width), lambda bi, i: (bi, i, 0)),
                   pl.BlockSpec((1, tt, width), lambda bi, i: (bi, n_tiles - 1 - i, 0))],
        out_shape=[jax.ShapeDtypeStruct((b, s, width), BF16)] * 2,
        scratch_shapes=per_direction + per_direction + [pltpu.VMEM((F32_ROWS, width), F32)],
        compiler_params=pltpu.CompilerParams(dimension_semantics=("parallel", "arbitrary")),
        name="lru",
    )(nat, nat, nat, nat, nat, nat, *consts)


def _token_order_rows(ref, r0, n_rows, slab):
    d, width = ref.shape[1], ref.shape[3]
    per = n_rows // d
    for r in range(d):
        piece = ref[0, r, r0 // d:r0 // d + per, :].astype(F32)
        for c in range(width // LANES):
            slab[c, pl.ds(r, per, stride=d), :] = piece[:, c * LANES:(c + 1) * LANES]
    return [slab[c] for c in range(width // LANES)]


def _merge_body(o0_ref, o1_ref, o2_ref, l0_ref, l1_ref, l2_ref, hf_ref, hb_ref, x_ref,
                ng_ref, wga_ref, wgo_ref, bg_ref, woa_ref, wol_ref, wout_ref, fg_ref, out_ref,
                o_slabs, l_slabs, *, final):
    tm, dm = x_ref.shape
    for blk, r0 in enumerate(range(0, tm, MERGE_SUB_ROWS)):
        rows = slice(r0, r0 + MERGE_SUB_ROWS)
        xn = _rms_norm(x_ref[rows], ng_ref[...]).astype(BF16)

        gatt = jnp.dot(xn, wga_ref[...], preferred_element_type=F32)
        glru, logit_att, logit_lru = (
            jnp.dot(xn, wgo_ref[:, c0:c0 + dm], preferred_element_type=F32) for c0 in (0, dm, 2 * dm))

        o_dilated = [_token_order_rows(ref, r0, MERGE_SUB_ROWS, o_slabs.at[blk, g])
                     for g, ref in enumerate((o1_ref, o2_ref))]
        l1, l2 = (_token_order_rows(ref, r0, MERGE_SUB_ROWS, l_slabs.at[blk, g])[0]
                  for g, ref in enumerate((l1_ref, l2_ref)))
        l0 = l0_ref[rows]
        mx = jnp.maximum(jnp.maximum(l0, l1), l2)
        e0, e1, e2 = jnp.exp(l0 - mx), jnp.exp(l1 - mx), jnp.exp(l2 - mx)
        inv = 1.0 / (e0 + e1 + e2)
        alphas = (e0 * inv, e1 * inv, e2 * inv)

        heads = []
        for h in range(HEADS_PER_GROUP):
            group_outs = (o0_ref[rows, h * HEAD_DIM:(h + 1) * HEAD_DIM].astype(F32), o_dilated[0][h], o_dilated[1][h])
            acc = jnp.zeros((MERGE_SUB_ROWS, HEAD_DIM), F32)
            for g in range(N_GROUPS):
                w = jnp.broadcast_to(alphas[g][:, h * LSE_REP:h * LSE_REP + 1], (MERGE_SUB_ROWS, HEAD_DIM))
                acc = acc + w * group_outs[g]
            heads.append(acc)
        mixed = jnp.concatenate(heads, axis=-1)

        y_att = (mixed * (gatt * _sigmoid(gatt))).astype(BF16)
        h_sum = hf_ref[rows].astype(F32) + hb_ref[rows].astype(F32)
        y_lru = (h_sum * (glru * _sigmoid(glru))).astype(BF16)

        p_att = jnp.dot(y_att, woa_ref[...], preferred_element_type=F32)
        p_lru = jnp.dot(y_lru, wol_ref[...], preferred_element_type=F32)
        gate_att = _sigmoid(logit_att + bg_ref[:, 0:dm])
        gate_lru = _sigmoid(logit_lru + bg_ref[:, dm:2 * dm])
        merged = (gate_att * p_att + gate_lru * p_lru).astype(BF16)
        y = x_ref[rows] + jnp.dot(merged, wout_ref[...], preferred_element_type=F32)
        if final:
            y = _rms_norm(y, fg_ref[...])
        out_ref[rows] = y


def _merge(x2d, outs, lses, h_f, h_b, norm_g, w_in, b_gate, w_o_att, w_o_lru, w_out, final_g, layer, final):
    t, dm = x2d.shape
    tm = MERGE_ROWS
    tiles_per_b = t // outs[1].shape[0] // tm
    att_gate_block = _col_block(N_GROUPS * QKV_WIDTH, GROUP_WIDTH)
    others_block = _col_block(N_GROUPS * QKV_WIDTH + GROUP_WIDTH + dm, 3 * dm)

    def rows(width):
        return pl.BlockSpec((tm, width), lambda i: (i, 0))

    def dilated_rows(a):
        d, width = a.shape[1], a.shape[3]
        return pl.BlockSpec((1, d, tm // d, width), lambda i: (i // tiles_per_b, 0, i % tiles_per_b, 0))

    n_blocks = tm // MERGE_SUB_ROWS
    return pl.pallas_call(
        functools.partial(_merge_body, final=final),
        grid=(t // tm,),
        in_specs=[rows(GROUP_WIDTH), dilated_rows(outs[1]), dilated_rows(outs[2]),
                  rows(LANES), dilated_rows(lses[1]), dilated_rows(lses[2])] + [rows(dm)] * 3
                 + [_layer(norm_g, layer), _layer(w_in, layer, cols=GROUP_WIDTH, col_block=att_gate_block),
                    _layer(w_in, layer, cols=3 * dm, col_block=others_block), _layer(b_gate, layer),
                    _layer(w_o_att, layer), _layer(w_o_lru, layer), _layer(w_out, layer),
                    pl.BlockSpec(final_g.shape, lambda i: (0, 0), pipeline_mode=pl.Buffered(1))],
        out_specs=rows(dm),
        out_shape=jax.ShapeDtypeStruct((t, dm), F32),
        scratch_shapes=[pltpu.VMEM((n_blocks, N_GROUPS - 1, GROUP_WIDTH // LANES, MERGE_SUB_ROWS, LANES), F32),
                        pltpu.VMEM((n_blocks, N_GROUPS - 1, 1, MERGE_SUB_ROWS, LANES), F32)],
        compiler_params=pltpu.CompilerParams(dimension_semantics=("parallel",)),
        name="merge_final" if final else "merge",
    )(*outs, *lses, h_f, h_b, x2d, norm_g, w_in, w_in, b_gate, w_o_att, w_o_lru, w_out, final_g)


def kernel(x, norm_g, w_in, b_gate, conv_w, conv_b, rg_w_a, rg_b_a, rg_w_x, rg_b_x, rg_lam,
           w_o_att, w_o_lru, w_out, final_g):
    b, s, dm = x.shape
    depth = w_in.shape[0]
    t = b * s
    w_in, rg_w_a, rg_w_x, w_o_att, w_o_lru, w_out = (
        w.astype(BF16) for w in (w_in, rg_w_a, rg_w_x, w_o_att, w_o_lru, w_out))
    norm_g, b_gate, conv_b = (v.reshape(depth, 1, -1) for v in (norm_g, b_gate, conv_b))
    final_g = final_g.reshape(1, dm)

    x2d = x.reshape(t, dm)
    for layer in range(depth):
        nat2d, qkv4, qkv16 = _norm_proj(x2d, norm_g, w_in, layer, b)
        nat = nat2d.reshape(b, s, -1)
        outs, lses = [], []
        for group, qkv in enumerate((nat, qkv4, qkv16)):
            o, lse = _attention_group(qkv, group, ATT_PATTERNS[group][1])
            outs.append(o.reshape(t, GROUP_WIDTH) if group == 0 else o)
            lses.append(lse.reshape(t, LANES) if group == 0 else lse)
        h_f, h_b = _lru(nat, conv_w, conv_b, rg_w_a, rg_b_a, rg_w_x, rg_b_x, rg_lam, layer)
        x2d = _merge(x2d, outs, lses, h_f.reshape(t, dm), h_b.reshape(t, dm), norm_g, w_in, b_gate,
                     w_o_att, w_o_lru, w_out, final_g, layer, final=(layer == depth - 1))
    return x2d.reshape(b, s, dm)
```

```python
import functools

import jax
import jax.numpy as jnp
from jax import lax
from jax.experimental import pallas as pl
from jax.experimental.pallas import tpu as pltpu

F32 = jnp.float32
BF16 = jnp.bfloat16

HEAD_DIM = 128
HEADS_PER_GROUP = 4
ATT_PATTERNS = ((128, 1), (512, 4), (2048, 16))
N_GROUPS = len(ATT_PATTERNS)
N_ATT_HEADS = N_GROUPS * HEADS_PER_GROUP
GROUP_WIDTH = HEADS_PER_GROUP * HEAD_DIM
QKV_WIDTH = 3 * GROUP_WIDTH
NEG_INF = -1e30
LRU_BLOCKS = 4
RG_C = 8.0
NORM_EPS = 1e-6
CONV_LEFT = 2

XLRU_COL = 0
NAT_QKV_COL = 2

LANES = 128
F32_ROWS = 8
BF16_ROWS = 16
HALF_KEYS = 64
Q_SUB = 128
K_SUB = Q_SUB + 2 * HALF_KEYS
LSE_REP = LANES // HEADS_PER_GROUP
SEGMENTS = F32_ROWS

PROJ_ROWS = 1024
PROJ_COLS = 512
ATTN_Q_ROWS = {1: 2048, 4: 2048, 16: 512}
ATTN_RESIDUES = {1: 1, 4: 1, 16: 4}
LRU_ROWS = 1024
MERGE_ROWS = 1024
MERGE_SUB_ROWS = 256

LOG2_E = 1.4426950408889634
LN_2 = 0.6931471805599453


def _sigmoid(x):
    return 1.0 / (1.0 + jnp.exp2(x * -LOG2_E))


def _rms_norm(x, g):
    ms = jnp.mean(x * x, axis=-1, keepdims=True)
    return x * lax.rsqrt(ms + NORM_EPS) * g


def _col_block(first_col, cols):
    assert first_col % cols == 0, (first_col, cols)
    return first_col // cols


def _layer(stacked, layer, cols=None, col_block=0):
    block = stacked.shape[1:] if cols is None else stacked.shape[1:-1] + (cols,)
    index = (layer,) + (0,) * (len(block) - 1) + (col_block,)
    return pl.BlockSpec((None,) + block, lambda *_: index, pipeline_mode=pl.Buffered(1))


def _norm_proj_body(x_ref, g_ref, wqkv_ref, wx_ref, nat_ref, s4_ref, s16_ref, xn_ref, xn4_ref, xn16_ref, slab_ref):
    tm, dm = x_ref.shape
    width = wx_ref.shape[1]

    y = _rms_norm(x_ref[...], g_ref[...])
    xn_ref[...] = y.astype(BF16)
    for c in range(dm // LANES):
        slab_ref[c] = y[:, c * LANES:(c + 1) * LANES]
    for ref, d in ((xn4_ref, 4), (xn16_ref, 16)):
        rows = tm // d
        for c in range(dm // LANES):
            for r in range(d):
                ref[r * rows:(r + 1) * rows, c * LANES:(c + 1) * LANES] = (
                    slab_ref[c, pl.ds(r, rows, stride=d), :].astype(BF16))

    def project(lhs_ref, w_ref, col, cols):
        return jnp.dot(lhs_ref[...], w_ref[:, col:col + cols], preferred_element_type=F32).astype(BF16)

    for n0 in range(0, width, PROJ_COLS):
        nat_ref[:, n0:n0 + PROJ_COLS] = project(xn_ref, wx_ref, n0, PROJ_COLS)
    gw = GROUP_WIDTH
    for part in range(3):
        src = part * N_GROUPS * gw
        dst = part * gw
        nat_ref[:, width + dst:width + dst + gw] = project(xn_ref, wqkv_ref, src, gw)
        s4_ref[0, :, :, dst:dst + gw] = project(xn4_ref, wqkv_ref, src + gw, gw).reshape(4, tm // 4, gw)
        s16_ref[0, :, :, dst:dst + gw] = project(xn16_ref, wqkv_ref, src + 2 * gw, gw).reshape(16, tm // 16, gw)


def _norm_proj(x2d, norm_g, w_in, layer, batch):
    t, dm = x2d.shape
    s = t // batch
    n_nat = dm + QKV_WIDTH
    tm = PROJ_ROWS
    tiles_per_b = s // tm
    x_lru_block = _col_block(N_GROUPS * QKV_WIDTH + GROUP_WIDTH, dm)

    def strided_spec(d):
        return pl.BlockSpec((1, d, tm // d, QKV_WIDTH), lambda i: (i // tiles_per_b, 0, i % tiles_per_b, 0))

    return pl.pallas_call(
        _norm_proj_body,
        grid=(t // tm,),
        in_specs=[pl.BlockSpec((tm, dm), lambda i: (i, 0)), _layer(norm_g, layer),
                  _layer(w_in, layer, cols=N_GROUPS * QKV_WIDTH), _layer(w_in, layer, cols=dm, col_block=x_lru_block)],
        out_specs=[pl.BlockSpec((tm, n_nat), lambda i: (i, 0)), strided_spec(4), strided_spec(16)],
        out_shape=[jax.ShapeDtypeStruct((t, n_nat), BF16),
                   jax.ShapeDtypeStruct((batch, 4, s // 4, QKV_WIDTH), BF16),
                   jax.ShapeDtypeStruct((batch, 16, s // 16, QKV_WIDTH), BF16)],
        scratch_shapes=[pltpu.VMEM((tm, dm), BF16), pltpu.VMEM((tm, dm), BF16), pltpu.VMEM((tm, dm), BF16),
                        pltpu.VMEM((dm // LANES, tm, LANES), F32)],
        compiler_params=pltpu.CompilerParams(dimension_semantics=("parallel",)),
        name="norm_proj",
    )(x2d, norm_g, w_in, w_in)


def _attn_body(q_ref, kp_ref, kc_ref, kn_ref, vp_ref, vc_ref, vn_ref, o_ref, lse_ref,
               kcat, vcat, bias, *, group, dilation, tq, residues):
    qi = pl.program_id(1)
    r = pl.program_id(2)
    n_q = pl.num_programs(1)
    kcol = lax.broadcasted_iota(jnp.int32, (Q_SUB, K_SUB), 1)
    slopes = [2.0 ** (-8.0 * (group * HEADS_PER_GROUP + h + 1) / N_ATT_HEADS) for h in range(HEADS_PER_GROUP)]

    @pl.when((pl.program_id(0) == 0) & (qi == 0) & (r == 0))
    def _():
        qrow = lax.broadcasted_iota(jnp.int32, (Q_SUB, K_SUB), 0)
        absd = jnp.abs(kcol - HALF_KEYS - qrow)
        dist = (absd * dilation).astype(F32)
        for h in range(HEADS_PER_GROUP):
            bias[h] = jnp.where(absd <= HALF_KEYS, (-slopes[h] * LOG2_E) * dist, NEG_INF)
            vcat[:, :, (2 * h + 1) * HEAD_DIM:(2 * h + 2) * HEAD_DIM] = jnp.ones(
                (residues, tq + 2 * HALF_KEYS, HEAD_DIM), BF16)

    lane = lax.broadcasted_iota(jnp.int32, (Q_SUB, LANES), 1)
    for res in range(residues):
        blk = (0,) if dilation == 1 else (0, res)
        _attn_residue(q_ref, kp_ref, kc_ref, kn_ref, vp_ref, vc_ref, vn_ref, o_ref, lse_ref,
                      kcat.at[res], vcat.at[res], bias, blk, qi, n_q, kcol, lane, tq)


def _attn_residue(q_ref, kp_ref, kc_ref, kn_ref, vp_ref, vc_ref, vn_ref, o_ref, lse_ref,
                  kcat, vcat, bias, blk, qi, n_q, kcol, lane, tq):
    kcat[0:HALF_KEYS] = kp_ref[blk]
    kcat[HALF_KEYS:HALF_KEYS + tq] = kc_ref[blk]
    kcat[HALF_KEYS + tq:] = kn_ref[blk]
    for h in range(HEADS_PER_GROUP):
        src = slice(h * HEAD_DIM, (h + 1) * HEAD_DIM)
        dst = slice(2 * h * HEAD_DIM, (2 * h + 1) * HEAD_DIM)
        vcat[0:HALF_KEYS, dst] = vp_ref[blk + (slice(None), src)]
        vcat[HALF_KEYS:HALF_KEYS + tq, dst] = vc_ref[blk + (slice(None), src)]
        vcat[HALF_KEYS + tq:, dst] = vn_ref[blk + (slice(None), src)]

    n_sub_tiles = tq // Q_SUB
    for sub in range(n_sub_tiles):
        r0 = sub * Q_SUB
        lse_tile = jnp.zeros((Q_SUB, LANES), F32)
        for h in range(HEADS_PER_GROUP):
            c0 = h * HEAD_DIM
            q = q_ref[blk + (slice(r0, r0 + Q_SUB), slice(c0, c0 + HEAD_DIM))]
            k = kcat[r0:r0 + K_SUB, c0:c0 + HEAD_DIM]
            raw = lax.dot_general(q, k, (((1,), (1,)), ((), ())), preferred_element_type=F32)
            s2 = raw * (HEAD_DIM ** -0.5 * LOG2_E) + bias[h]
            if sub == 0:
                s2 = jnp.where((qi > 0) | (kcol >= HALF_KEYS), s2, NEG_INF)
            if sub == n_sub_tiles - 1:
                s2 = jnp.where((qi < n_q - 1) | (kcol < HALF_KEYS + Q_SUB), s2, NEG_INF)
            m2 = jnp.max(s2, axis=-1, keepdims=True)
            p = jnp.exp2(s2 - m2).astype(BF16)
            acc = jnp.dot(p, vcat[r0:r0 + K_SUB, 2 * c0:2 * c0 + 2 * HEAD_DIM], preferred_element_type=F32)
            den = acc[:, HEAD_DIM:]
            o = acc[:, :HEAD_DIM] / den
            o_ref[blk + (slice(r0, r0 + Q_SUB), slice(c0, c0 + HEAD_DIM))] = o.astype(BF16)
            lse = (m2 + jnp.log2(den)) * LN_2
            lse_tile = jnp.where((lane >= h * LSE_REP) & (lane < (h + 1) * LSE_REP), lse, lse_tile)
        lse_ref[blk + (slice(r0, r0 + Q_SUB), slice(None))] = lse_tile


def _attention_group(qkv, group, dilation):
    if dilation == 1:
        b, s, _ = qkv.shape
    else:
        b, _, n_sub, _ = qkv.shape
        s = n_sub * dilation
    n_sub = s // dilation
    tq = min(ATTN_Q_ROWS[dilation], n_sub)
    residues = ATTN_RESIDUES[dilation]
    q_col = NAT_QKV_COL if dilation == 1 else 0
    halo_per_tile = tq // HALF_KEYS
    last_halo = n_sub // HALF_KEYS - 1

    def spec(rows, row_index, col, width=GROUP_WIDTH):
        if dilation == 1:
            return pl.BlockSpec((1, rows, width), lambda bi, qi, r: (bi, row_index(qi), col))
        return pl.BlockSpec((1, residues, rows, width), lambda bi, qi, r: (bi, r, row_index(qi), col))

    def main_spec(col):
        return spec(tq, lambda qi: qi, col)

    def prev_spec(col):
        return spec(HALF_KEYS, lambda qi: jnp.maximum(qi * halo_per_tile - 1, 0), col)

    def next_spec(col):
        return spec(HALF_KEYS, lambda qi: jnp.minimum((qi + 1) * halo_per_tile, last_halo), col)

    rows_shape = (b, s) if dilation == 1 else (b, dilation, n_sub)
    return pl.pallas_call(
        functools.partial(_attn_body, group=group, dilation=dilation, tq=tq, residues=residues),
        grid=(b, n_sub // tq, dilation // residues),
        in_specs=[main_spec(q_col),
                  prev_spec(q_col + 1), main_spec(q_col + 1), next_spec(q_col + 1),
                  prev_spec(q_col + 2), main_spec(q_col + 2), next_spec(q_col + 2)],
        out_specs=[main_spec(0), spec(tq, lambda qi: qi, 0, LANES)],
        out_shape=[jax.ShapeDtypeStruct(rows_shape + (GROUP_WIDTH,), BF16),
                   jax.ShapeDtypeStruct(rows_shape + (LANES,), F32)],
        scratch_shapes=[pltpu.VMEM((residues, tq + 2 * HALF_KEYS, GROUP_WIDTH), BF16),
                        pltpu.VMEM((residues, tq + 2 * HALF_KEYS, 2 * GROUP_WIDTH), BF16),
                        pltpu.VMEM((HEADS_PER_GROUP, Q_SUB, K_SUB), F32)],
        compiler_params=pltpu.CompilerParams(dimension_semantics=("arbitrary", "arbitrary", "arbitrary")),
        name=f"attn_d{dilation}",
    )(qkv, qkv, qkv, qkv, qkv, qkv, qkv)


def _lru_direction(xp_ref, xc_ref, xn_ref, h_ref, tile, n_tiles, rev, dirn,
                   cw_ref, cb_ref, wa_ref, ba_ref, wx_ref, bx_ref, lam_ref,
                   xpad, a_s, b_s, hpad, carry):
    tt, width = xc_ref.shape[1:]
    bw = width // LRU_BLOCKS
    n_chunks = width // LANES
    seg_len = tt // SEGMENTS
    pitch = seg_len + SEGMENTS
    sub = lax.broadcasted_iota(jnp.int32, (SEGMENTS, width), 0)

    xf = xc_ref[0].astype(F32)
    for c in range(n_chunks):
        for k in range(SEGMENTS):
            xpad[c, k * pitch:k * pitch + seg_len, :] = xf[k * seg_len:(k + 1) * seg_len, c * LANES:(c + 1) * LANES]
    xs = jnp.concatenate(
        [jnp.concatenate([xpad[c, pl.ds(j, SEGMENTS, stride=pitch), :] for j in range(seg_len)], axis=0)
         for c in range(n_chunks)], axis=1)

    halo_prev = jnp.where(tile == 0, 0.0, xp_ref[0, BF16_ROWS - CONV_LEFT:BF16_ROWS, :].astype(F32))
    halo_next = jnp.where(tile == n_tiles - 1, 0.0, xn_ref[0, 0:1, :].astype(F32))
    before1 = jnp.where(sub == 0, halo_prev[1:2], pltpu.roll(xs[tt - SEGMENTS:tt], 1, 0))
    before2 = jnp.where(sub == 0, halo_prev[0:1], pltpu.roll(xs[tt - 2 * SEGMENTS:tt - SEGMENTS], 1, 0))
    after1 = jnp.where(sub == SEGMENTS - 1, halo_next, pltpu.roll(xs[0:SEGMENTS], SEGMENTS - 1, 0))
    xm2 = jnp.concatenate([before2, before1, xs[:tt - 2 * SEGMENTS]], axis=0)
    xm1 = jnp.concatenate([before1, xs[:tt - SEGMENTS]], axis=0)
    xp1 = jnp.concatenate([xs[SEGMENTS:], after1], axis=0)
    xc = cb_ref[...] + xm2 * cw_ref[0:1] + xm1 * cw_ref[1:2] + xs * cw_ref[2:3] + xp1 * cw_ref[3:4]
    xcb = xc.astype(BF16)

    lam = lam_ref[dirn:dirn + 1]
    neg_lam = -lam
    softplus = jnp.maximum(neg_lam, 0.0) + jnp.log1p(jnp.exp(-jnp.abs(neg_lam)))
    coef = -RG_C * softplus

    sub_blk = lax.broadcasted_iota(jnp.int32, (SEGMENTS, bw), 0)
    if rev:
        start_rows = slice(tt - SEGMENTS, tt)
        first = (tile == n_tiles - 1) & (sub_blk == SEGMENTS - 1)
    else:
        start_rows = slice(0, SEGMENTS)
        first = (tile == 0) & (sub_blk == 0)

    for blk in range(LRU_BLOCKS):
        c0 = blk * bw
        xblk = xcb[:, c0:c0 + bw]
        pa = jnp.dot(xblk, wa_ref[dirn, blk], preferred_element_type=F32) + ba_ref[dirn:dirn + 1, c0:c0 + bw]
        px = jnp.dot(xblk, wx_ref[dirn, blk], preferred_element_type=F32) + bx_ref[dirn:dirn + 1, c0:c0 + bw]
        log_a = coef[:, c0:c0 + bw] * _sigmoid(pa)
        a = jnp.exp(log_a)
        u = jnp.tanh(log_a) * (-1.0 - a * a)
        mult = jnp.where(u > 0.0, u * lax.rsqrt(u), 0.0)
        gated = _sigmoid(px) * xc[:, c0:c0 + bw]
        b = mult * gated
        a_s[:, c0:c0 + bw] = a
        b_s[:, c0:c0 + bw] = b
        b_s[start_rows, c0:c0 + bw] = jnp.where(first, gated[start_rows], b[start_rows])

    steps = range(seg_len - 1, -1, -1) if rev else range(seg_len)
    segs = range(SEGMENTS - 1, -1, -1) if rev else range(SEGMENTS)
    sub1 = lax.broadcasted_iota(jnp.int32, (SEGMENTS, LANES), 0)
    for c in range(n_chunks):
        cs = slice(c * LANES, (c + 1) * LANES)
        prod = jnp.ones((SEGMENTS, LANES), F32)
        h = jnp.zeros((SEGMENTS, LANES), F32)
        for j in steps:
            aj = a_s[j * SEGMENTS:(j + 1) * SEGMENTS, cs]
            h = aj * h + b_s[j * SEGMENTS:(j + 1) * SEGMENTS, cs]
            prod = aj * prod
        state = carry[dirn:dirn + 1, cs]
        h0 = jnp.zeros((SEGMENTS, LANES), F32)
        for k in segs:
            h0 = jnp.where(sub1 == k, state, h0)
            state = prod[k:k + 1] * state + h[k:k + 1]
        carry[dirn:dirn + 1, cs] = state
        h = h0
        for j in steps:
            h = a_s[j * SEGMENTS:(j + 1) * SEGMENTS, cs] * h + b_s[j * SEGMENTS:(j + 1) * SEGMENTS, cs]
            hpad[c, pl.ds(j, SEGMENTS, stride=pitch), :] = h
        for k in range(SEGMENTS):
            h_ref[0, k * seg_len:(k + 1) * seg_len, cs] = hpad[c, k * pitch:k * pitch + seg_len, :].astype(h_ref.dtype)


def _lru_body(xfp_ref, xfc_ref, xfn_ref, xbp_ref, xbc_ref, xbn_ref,
              cw_ref, cb_ref, wa_ref, ba_ref, wx_ref, bx_ref, lam_ref,
              hf_ref, hb_ref, xpad_f, a_f, b_f, hpad_f, xpad_b, a_b, b_b, hpad_b, carry):
    i = pl.program_id(1)
    n_tiles = pl.num_programs(1)

    @pl.when(i == 0)
    def _():
        carry[...] = jnp.zeros_like(carry)

    params = (cw_ref, cb_ref, wa_ref, ba_ref, wx_ref, bx_ref, lam_ref)
    _lru_direction(xfp_ref, xfc_ref, xfn_ref, hf_ref, i, n_tiles, False, 0, *params,
                   xpad_f, a_f, b_f, hpad_f, carry)
    _lru_direction(xbp_ref, xbc_ref, xbn_ref, hb_ref, n_tiles - 1 - i, n_tiles, True, 1, *params,
                   xpad_b, a_b, b_b, hpad_b, carry)


def _lru(nat, conv_w, conv_b, w_a, b_a, w_x, b_x, lam, layer):
    b, s, _ = nat.shape
    width = conv_w.shape[-1]
    tt = LRU_ROWS
    n_tiles = s // tt
    halo_per_tile = tt // BF16_ROWS
    last_halo = s // BF16_ROWS - 1

    def tile_of(i, rev):
        return n_tiles - 1 - i if rev else i

    def specs(rev):
        return [
            pl.BlockSpec((1, BF16_ROWS, width),
                         lambda bi, i: (bi, jnp.maximum(tile_of(i, rev) * halo_per_tile - 1, 0), XLRU_COL)),
            pl.BlockSpec((1, tt, width), lambda bi, i: (bi, tile_of(i, rev), XLRU_COL)),
            pl.BlockSpec((1, BF16_ROWS, width),
                         lambda bi, i: (bi, jnp.minimum((tile_of(i, rev) + 1) * halo_per_tile, last_halo), XLRU_COL)),
        ]

    consts = (conv_w, conv_b, w_a, b_a, w_x, b_x, lam)
    padded_rows = tt + SEGMENTS * SEGMENTS
    per_direction = [pltpu.VMEM((width // LANES, padded_rows, LANES), F32),
                     pltpu.VMEM((tt, width), F32),
                     pltpu.VMEM((tt, width), F32),
                     pltpu.VMEM((width // LANES, padded_rows, LANES), F32)]
    return pl.pallas_call(
        _lru_body,
        grid=(b, n_tiles),
        in_specs=specs(False) + specs(True) + [_layer(a, layer) for a in consts],
        out_specs=[pl.BlockSpec((1, tt, width), lambda bi, i: (bi, i, 0)),
                   pl.BlockSpec((1, tt, width), lambda bi, i: (bi, n_tiles - 1 - i, 0))],
        out_shape=[jax.ShapeDtypeStruct((b, s, width), BF16)] * 2,
        scratch_shapes=per_direction + per_direction + [pltpu.VMEM((F32_ROWS, width), F32)],
        compiler_params=pltpu.CompilerParams(dimension_semantics=("parallel", "arbitrary")),
        name="lru",
    )(nat, nat, nat, nat, nat, nat, *consts)


def _token_order_rows(ref, r0, n_rows, slab):
    d, width = ref.shape[1], ref.shape[3]
    per = n_rows // d
    for r in range(d):
        piece = ref[0, r, r0 // d:r0 // d + per, :].astype(F32)
        for c in range(width // LANES):
            slab[c, pl.ds(r, per, stride=d), :] = piece[:, c * LANES:(c + 1) * LANES]
    return [slab[c] for c in range(width // LANES)]


def _merge_body(o0_ref, o1_ref, o2_ref, l0_ref, l1_ref, l2_ref, hf_ref, hb_ref, x_ref,
                ng_ref, wga_ref, wgo_ref, bg_ref, woa_ref, wol_ref, wout_ref, fg_ref, out_ref,
                o_slabs, l_slabs, *, final):
    tm, dm = x_ref.shape
    for blk, r0 in enumerate(range(0, tm, MERGE_SUB_ROWS)):
        rows = slice(r0, r0 + MERGE_SUB_ROWS)
        xn = _rms_norm(x_ref[rows], ng_ref[...]).astype(BF16)

        gatt = jnp.dot(xn, wga_ref[...], preferred_element_type=F32)
        glru, logit_att, logit_lru = (
            jnp.dot(xn, wgo_ref[:, c0:c0 + dm], preferred_element_type=F32) for c0 in (0, dm, 2 * dm))

        o_dilated = [_token_order_rows(ref, r0, MERGE_SUB_ROWS, o_slabs.at[blk, g])
                     for g, ref in enumerate((o1_ref, o2_ref))]
        l1, l2 = (_token_order_rows(ref, r0, MERGE_SUB_ROWS, l_slabs.at[blk, g])[0]
                  for g, ref in enumerate((l1_ref, l2_ref)))
        l0 = l0_ref[rows]
        mx = jnp.maximum(jnp.maximum(l0, l1), l2)
        e0, e1, e2 = jnp.exp(l0 - mx), jnp.exp(l1 - mx), jnp.exp(l2 - mx)
        inv = 1.0 / (e0 + e1 + e2)
        alphas = (e0 * inv, e1 * inv, e2 * inv)

        heads = []
        for h in range(HEADS_PER_GROUP):
            group_outs = (o0_ref[rows, h * HEAD_DIM:(h + 1) * HEAD_DIM].astype(F32), o_dilated[0][h], o_dilated[1][h])
            acc = jnp.zeros((MERGE_SUB_ROWS, HEAD_DIM), F32)
            for g in range(N_GROUPS):
                w = jnp.broadcast_to(alphas[g][:, h * LSE_REP:h * LSE_REP + 1], (MERGE_SUB_ROWS, HEAD_DIM))
                acc = acc + w * group_outs[g]
            heads.append(acc)
        mixed = jnp.concatenate(heads, axis=-1)

        y_att = (mixed * (gatt * _sigmoid(gatt))).astype(BF16)
        h_sum = hf_ref[rows].astype(F32) + hb_ref[rows].astype(F32)
        y_lru = (h_sum * (glru * _sigmoid(glru))).astype(BF16)

        p_att = jnp.dot(y_att, woa_ref[...], preferred_element_type=F32)
        p_lru = jnp.dot(y_lru, wol_ref[...], preferred_element_type=F32)
        gate_att = _sigmoid(logit_att + bg_ref[:, 0:dm])
        gate_lru = _sigmoid(logit_lru + bg_ref[:, dm:2 * dm])
        merged = (gate_att * p_att + gate_lru * p_lru).astype(BF16)
        y = x_ref[rows] + jnp.dot(merged, wout_ref[...], preferred_element_type=F32)
        if final:
            y = _rms_norm(y, fg_ref[...])
        out_ref[rows] = y


def _merge(x2d, outs, lses, h_f, h_b, norm_g, w_in, b_gate, w_o_att, w_o_lru, w_out, final_g, layer, final):
    t, dm = x2d.shape
    tm = MERGE_ROWS
    tiles_per_b = t // outs[1].shape[0] // tm
    att_gate_block = _col_block(N_GROUPS * QKV_WIDTH, GROUP_WIDTH)
    others_block = _col_block(N_GROUPS * QKV_WIDTH + GROUP_WIDTH + dm, 3 * dm)

    def rows(width):
        return pl.BlockSpec((tm, width), lambda i: (i, 0))

    def dilated_rows(a):
        d, width = a.shape[1], a.shape[3]
        return pl.BlockSpec((1, d, tm // d, width), lambda i: (i // tiles_per_b, 0, i % tiles_per_b, 0))

    n_blocks = tm // MERGE_SUB_ROWS
    return pl.pallas_call(
        functools.partial(_merge_body, final=final),
        grid=(t // tm,),
        in_specs=[rows(GROUP_WIDTH), dilated_rows(outs[1]), dilated_rows(outs[2]),
                  rows(LANES), dilated_rows(lses[1]), dilated_rows(lses[2])] + [rows(dm)] * 3
                 + [_layer(norm_g, layer), _layer(w_in, layer, cols=GROUP_WIDTH, col_block=att_gate_block),
                    _layer(w_in, layer, cols=3 * dm, col_block=others_block), _layer(b_gate, layer),
                    _layer(w_o_att, layer), _layer(w_o_lru, layer), _layer(w_out, layer),
                    pl.BlockSpec(final_g.shape, lambda i: (0, 0), pipeline_mode=pl.Buffered(1))],
        out_specs=rows(dm),
        out_shape=jax.ShapeDtypeStruct((t, dm), F32),
        scratch_shapes=[pltpu.VMEM((n_blocks, N_GROUPS - 1, GROUP_WIDTH // LANES, MERGE_SUB_ROWS, LANES), F32),
                        pltpu.VMEM((n_blocks, N_GROUPS - 1, 1, MERGE_SUB_ROWS, LANES), F32)],
        compiler_params=pltpu.CompilerParams(dimension_semantics=("parallel",)),
        name="merge_final" if final else "merge",
    )(*outs, *lses, h_f, h_b, x2d, norm_g, w_in, w_in, b_gate, w_o_att, w_o_lru, w_out, final_g)


def kernel(x, norm_g, w_in, b_gate, conv_w, conv_b, rg_w_a, rg_b_a, rg_w_x, rg_b_x, rg_lam,
           w_o_att, w_o_lru, w_out, final_g):
    b, s, dm = x.shape
    depth = w_in.shape[0]
    t = b * s
    w_in, rg_w_a, rg_w_x, w_o_att, w_o_lru, w_out = (
        w.astype(BF16) for w in (w_in, rg_w_a, rg_w_x, w_o_att, w_o_lru, w_out))
    norm_g, b_gate, conv_b = (v.reshape(depth, 1, -1) for v in (norm_g, b_gate, conv_b))
    final_g = final_g.reshape(1, dm)

    x2d = x.reshape(t, dm)
    for layer in range(depth):
        nat2d, qkv4, qkv16 = _norm_proj(x2d, norm_g, w_in, layer, b)
        nat = nat2d.reshape(b, s, -1)
        outs, lses = [], []
        for group, qkv in enumerate((nat, qkv4, qkv16)):
            o, lse = _attention_group(qkv, group, ATT_PATTERNS[group][1])
            outs.append(o.reshape(t, GROUP_WIDTH) if group == 0 else o)
            lses.append(lse.reshape(t, LANES) if group == 0 else lse)
        h_f, h_b = _lru(nat, conv_w, conv_b, rg_w_a, rg_b_a, rg_w_x, rg_b_x, rg_lam, layer)
        x2d = _merge(x2d, outs, lses, h_f.reshape(t, dm), h_b.reshape(t, dm), norm_g, w_in, b_gate,
                     w_o_att, w_o_lru, w_out, final_g, layer, final=(layer == depth - 1))
    return x2d.reshape(b, s, dm)
```

```python
import functools

import jax
import jax.numpy as jnp
from jax import lax
from jax.experimental import pallas as pl
from jax.experimental.pallas import tpu as pltpu

F32 = jnp.float32
BF16 = jnp.bfloat16

HEAD_DIM = 128
HEADS_PER_GROUP = 4
ATT_PATTERNS = ((128, 1), (512, 4), (2048, 16))
N_GROUPS = len(ATT_PATTERNS)
N_ATT_HEADS = N_GROUPS * HEADS_PER_GROUP
GROUP_WIDTH = HEADS_PER_GROUP * HEAD_DIM
QKV_WIDTH = 3 * GROUP_WIDTH
NEG_INF = -1e30
LRU_BLOCKS = 4
RG_C = 8.0
NORM_EPS = 1e-6
CONV_LEFT = 2

XLRU_COL = 0
NAT_QKV_COL = 2

LANES = 128
F32_ROWS = 8
BF16_ROWS = 16
HALF_KEYS = 64
Q_SUB = 128
K_SUB = Q_SUB + 2 * HALF_KEYS
LSE_REP = LANES // HEADS_PER_GROUP
SEGMENTS = F32_ROWS

PROJ_ROWS = 1024
PROJ_COLS = 512
ATTN_Q_ROWS = {1: 2048, 4: 2048, 16: 512}
ATTN_RESIDUES = {1: 1, 4: 1, 16: 4}
LRU_ROWS = 1024
MERGE_ROWS = 1024
MERGE_SUB_ROWS = 256

LOG2_E = 1.4426950408889634
LN_2 = 0.6931471805599453


def _sigmoid(x):
    return 1.0 / (1.0 + jnp.exp2(x * -LOG2_E))


def _rms_norm(x, g):
    ms = jnp.mean(x * x, axis=-1, keepdims=True)
    return x * lax.rsqrt(ms + NORM_EPS) * g


def _col_block(first_col, cols):
    assert first_col % cols == 0, (first_col, cols)
    return first_col // cols


def _layer(stacked, layer, cols=None, col_block=0):
    block = stacked.shape[1:] if cols is None else stacked.shape[1:-1] + (cols,)
    index = (layer,) + (0,) * (len(block) - 1) + (col_block,)
    return pl.BlockSpec((None,) + block, lambda *_: index, pipeline_mode=pl.Buffered(1))


def _norm_proj_body(x_ref, g_ref, wqkv_ref, wx_ref, nat_ref, s4_ref, s16_ref, xn_ref, xn4_ref, xn16_ref, slab_ref):
    tm, dm = x_ref.shape
    width = wx_ref.shape[1]

    y = _rms_norm(x_ref[...], g_ref[...])
    xn_ref[...] = y.astype(BF16)
    for c in range(dm // LANES):
        slab_ref[c] = y[:, c * LANES:(c + 1) * LANES]
    for ref, d in ((xn4_ref, 4), (xn16_ref, 16)):
        rows = tm // d
        for c in range(dm // LANES):
            for r in range(d):
                ref[r * rows:(r + 1) * rows, c * LANES:(c + 1) * LANES] = (
                    slab_ref[c, pl.ds(r, rows, stride=d), :].astype(BF16))

    def project(lhs_ref, w_ref, col, cols):
        return jnp.dot(lhs_ref[...], w_ref[:, col:col + cols], preferred_element_type=F32).astype(BF16)

    for n0 in range(0, width, PROJ_COLS):
        nat_ref[:, n0:n0 + PROJ_COLS] = project(xn_ref, wx_ref, n0, PROJ_COLS)
    gw = GROUP_WIDTH
    for part in range(3):
        src = part * N_GROUPS * gw
        dst = part * gw
        nat_ref[:, width + dst:width + dst + gw] = project(xn_ref, wqkv_ref, src, gw)
        s4_ref[0, :, :, dst:dst + gw] = project(xn4_ref, wqkv_ref, src + gw, gw).reshape(4, tm // 4, gw)
        s16_ref[0, :, :, dst:dst + gw] = project(xn16_ref, wqkv_ref, src + 2 * gw, gw).reshape(16, tm // 16, gw)


def _norm_proj(x2d, norm_g, w_in, layer, batch):
    t, dm = x2d.shape
    s = t // batch
    n_nat = dm + QKV_WIDTH
    tm = PROJ_ROWS
    tiles_per_b = s // tm
    x_lru_block = _col_block(N_GROUPS * QKV_WIDTH + GROUP_WIDTH, dm)

    def strided_spec(d):
        return pl.BlockSpec((1, d, tm // d, QKV_WIDTH), lambda i: (i // tiles_per_b, 0, i % tiles_per_b, 0))

    return pl.pallas_call(
        _norm_proj_body,
        grid=(t // tm,),
        in_specs=[pl.BlockSpec((tm, dm), lambda i: (i, 0)), _layer(norm_g, layer),
                  _layer(w_in, layer, cols=N_GROUPS * QKV_WIDTH), _layer(w_in, layer, cols=dm, col_block=x_lru_block)],
        out_specs=[pl.BlockSpec((tm, n_nat), lambda i: (i, 0)), strided_spec(4), strided_spec(16)],
        out_shape=[jax.ShapeDtypeStruct((t, n_nat), BF16),
                   jax.ShapeDtypeStruct((batch, 4, s // 4, QKV_WIDTH), BF16),
                   jax.ShapeDtypeStruct((batch, 16, s // 16, QKV_WIDTH), BF16)],
        scratch_shapes=[pltpu.VMEM((tm, dm), BF16), pltpu.VMEM((tm, dm), BF16), pltpu.VMEM((tm, dm), BF16),
                        pltpu.VMEM((dm // LANES, tm, LANES), F32)],
        compiler_params=pltpu.CompilerParams(dimension_semantics=("parallel",)),
        name="norm_proj",
    )(x2d, norm_g, w_in, w_in)


def _attn_body(q_ref, kp_ref, kc_ref, kn_ref, vp_ref, vc_ref, vn_ref, o_ref, lse_ref,
               kcat, vcat, bias, *, group, dilation, tq, residues):
    qi = pl.program_id(1)
    r = pl.program_id(2)
    n_q = pl.num_programs(1)
    kcol = lax.broadcasted_iota(jnp.int32, (Q_SUB, K_SUB), 1)
    slopes = [2.0 ** (-8.0 * (group * HEADS_PER_GROUP + h + 1) / N_ATT_HEADS) for h in range(HEADS_PER_GROUP)]

    @pl.when((pl.program_id(0) == 0) & (qi == 0) & (r == 0))
    def _():
        qrow = lax.broadcasted_iota(jnp.int32, (Q_SUB, K_SUB), 0)
        absd = jnp.abs(kcol - HALF_KEYS - qrow)
        dist = (absd * dilation).astype(F32)
        for h in range(HEADS_PER_GROUP):
            bias[h] = jnp.where(absd <= HALF_KEYS, (-slopes[h] * LOG2_E) * dist, NEG_INF)
            vcat[:, :, (2 * h + 1) * HEAD_DIM:(2 * h + 2) * HEAD_DIM] = jnp.ones(
                (residues, tq + 2 * HALF_KEYS, HEAD_DIM), BF16)

    lane = lax.broadcasted_iota(jnp.int32, (Q_SUB, LANES), 1)
    for res in range(residues):
        blk = (0,) if dilation == 1 else (0, res)
        _attn_residue(q_ref, kp_ref, kc_ref, kn_ref, vp_ref, vc_ref, vn_ref, o_ref, lse_ref,
                      kcat.at[res], vcat.at[res], bias, blk, qi, n_q, kcol, lane, tq)


def _attn_residue(q_ref, kp_ref, kc_ref, kn_ref, vp_ref, vc_ref, vn_ref, o_ref, lse_ref,
                  kcat, vcat, bias, blk, qi, n_q, kcol, lane, tq):
    kcat[0:HALF_KEYS] = kp_ref[blk]
    kcat[HALF_KEYS:HALF_KEYS + tq] = kc_ref[blk]
    kcat[HALF_KEYS + tq:] = kn_ref[blk]
    for h in range(HEADS_PER_GROUP):
        src = slice(h * HEAD_DIM, (h + 1) * HEAD_DIM)
        dst = slice(2 * h * HEAD_DIM, (2 * h + 1) * HEAD_DIM)
        vcat[0:HALF_KEYS, dst] = vp_ref[blk + (slice(None), src)]
        vcat[HALF_KEYS:HALF_KEYS + tq, dst] = vc_ref[blk + (slice(None), src)]
        vcat[HALF_KEYS + tq:, dst] = vn_ref[blk + (slice(None), src)]

    n_sub_tiles = tq // Q_SUB
    for sub in range(n_sub_tiles):
        r0 = sub * Q_SUB
        lse_tile = jnp.zeros((Q_SUB, LANES), F32)
        for h in range(HEADS_PER_GROUP):
            c0 = h * HEAD_DIM
            q = q_ref[blk + (slice(r0, r0 + Q_SUB), slice(c0, c0 + HEAD_DIM))]
            k = kcat[r0:r0 + K_SUB, c0:c0 + HEAD_DIM]
            raw = lax.dot_general(q, k, (((1,), (1,)), ((), ())), preferred_element_type=F32)
            s2 = raw * (HEAD_DIM ** -0.5 * LOG2_E) + bias[h]
            if sub == 0:
                s2 = jnp.where((qi > 0) | (kcol >= HALF_KEYS), s2, NEG_INF)
            if sub == n_sub_tiles - 1:
                s2 = jnp.where((qi < n_q - 1) | (kcol < HALF_KEYS + Q_SUB), s2, NEG_INF)
            m2 = jnp.max(s2, axis=-1, keepdims=True)
            p = jnp.exp2(s2 - m2).astype(BF16)
            acc = jnp.dot(p, vcat[r0:r0 + K_SUB, 2 * c0:2 * c0 + 2 * HEAD_DIM], preferred_element_type=F32)
            den = acc[:, HEAD_DIM:]
            o = acc[:, :HEAD_DIM] / den
            o_ref[blk + (slice(r0, r0 + Q_SUB), slice(c0, c0 + HEAD_DIM))] = o.astype(BF16)
            lse = (m2 + jnp.log2(den)) * LN_2
            lse_tile = jnp.where((lane >= h * LSE_REP) & (lane < (h + 1) * LSE_REP), lse, lse_tile)
        lse_ref[blk + (slice(r0, r0 + Q_SUB), slice(None))] = lse_tile


def _attention_group(qkv, group, dilation):
    if dilation == 1:
        b, s, _ = qkv.shape
    else:
        b, _, n_sub, _ = qkv.shape
        s = n_sub * dilation
    n_sub = s // dilation
    tq = min(ATTN_Q_ROWS[dilation], n_sub)
    residues = ATTN_RESIDUES[dilation]
    q_col = NAT_QKV_COL if dilation == 1 else 0
    halo_per_tile = tq // HALF_KEYS
    last_halo = n_sub // HALF_KEYS - 1

    def spec(rows, row_index, col, width=GROUP_WIDTH):
        if dilation == 1:
            return pl.BlockSpec((1, rows, width), lambda bi, qi, r: (bi, row_index(qi), col))
        return pl.BlockSpec((1, residues, rows, width), lambda bi, qi, r: (bi, r, row_index(qi), col))

    def main_spec(col):
        return spec(tq, lambda qi: qi, col)

    def prev_spec(col):
        return spec(HALF_KEYS, lambda qi: jnp.maximum(qi * halo_per_tile - 1, 0), col)

    def next_spec(col):
        return spec(HALF_KEYS, lambda qi: jnp.minimum((qi + 1) * halo_per_tile, last_halo), col)

    rows_shape = (b, s) if dilation == 1 else (b, dilation, n_sub)
    return pl.pallas_call(
        functools.partial(_attn_body, group=group, dilation=dilation, tq=tq, residues=residues),
        grid=(b, n_sub // tq, dilation // residues),
        in_specs=[main_spec(q_col),
                  prev_spec(q_col + 1), main_spec(q_col + 1), next_spec(q_col + 1),
                  prev_spec(q_col + 2), main_spec(q_col + 2), next_spec(q_col + 2)],
        out_specs=[main_spec(0), spec(tq, lambda qi: qi, 0, LANES)],
        out_shape=[jax.ShapeDtypeStruct(rows_shape + (GROUP_WIDTH,), BF16),
                   jax.ShapeDtypeStruct(rows_shape + (LANES,), F32)],
        scratch_shapes=[pltpu.VMEM((residues, tq + 2 * HALF_KEYS, GROUP_WIDTH), BF16),
                        pltpu.VMEM((residues, tq + 2 * HALF_KEYS, 2 * GROUP_WIDTH), BF16),
                        pltpu.VMEM((HEADS_PER_GROUP, Q_SUB, K_SUB), F32)],
        compiler_params=pltpu.CompilerParams(dimension_semantics=("arbitrary", "arbitrary", "arbitrary")),
        name=f"attn_d{dilation}",
    )(qkv, qkv, qkv, qkv, qkv, qkv, qkv)


def _lru_direction(xp_ref, xc_ref, xn_ref, h_ref, tile, n_tiles, rev, dirn,
                   cw_ref, cb_ref, wa_ref, ba_ref, wx_ref, bx_ref, lam_ref,
                   xpad, a_s, b_s, hpad, carry):
    tt, width = xc_ref.shape[1:]
    bw = width // LRU_BLOCKS
    n_chunks = width // LANES
    seg_len = tt // SEGMENTS
    pitch = seg_len + SEGMENTS
    sub = lax.broadcasted_iota(jnp.int32, (SEGMENTS, width), 0)

    xf = xc_ref[0].astype(F32)
    for c in range(n_chunks):
        for k in range(SEGMENTS):
            xpad[c, k * pitch:k * pitch + seg_len, :] = xf[k * seg_len:(k + 1) * seg_len, c * LANES:(c + 1) * LANES]
    xs = jnp.concatenate(
        [jnp.concatenate([xpad[c, pl.ds(j, SEGMENTS, stride=pitch), :] for j in range(seg_len)], axis=0)
         for c in range(n_chunks)], axis=1)

    halo_prev = jnp.where(tile == 0, 0.0, xp_ref[0, BF16_ROWS - CONV_LEFT:BF16_ROWS, :].astype(F32))
    halo_next = jnp.where(tile == n_tiles - 1, 0.0, xn_ref[0, 0:1, :].astype(F32))
    before1 = jnp.where(sub == 0, halo_prev[1:2], pltpu.roll(xs[tt - SEGMENTS:tt], 1, 0))
    before2 = jnp.where(sub == 0, halo_prev[0:1], pltpu.roll(xs[tt - 2 * SEGMENTS:tt - SEGMENTS], 1, 0))
    after1 = jnp.where(sub == SEGMENTS - 1, halo_next, pltpu.roll(xs[0:SEGMENTS], SEGMENTS - 1, 0))
    xm2 = jnp.concatenate([before2, before1, xs[:tt - 2 * SEGMENTS]], axis=0)
    xm1 = jnp.concatenate([before1, xs[:tt - SEGMENTS]], axis=0)
    xp1 = jnp.concatenate([xs[SEGMENTS:], after1], axis=0)
    xc = cb_ref[...] + xm2 * cw_ref[0:1] + xm1 * cw_ref[1:2] + xs * cw_ref[2:3] + xp1 * cw_ref[3:4]
    xcb = xc.astype(BF16)

    lam = lam_ref[dirn:dirn + 1]
    neg_lam = -lam
    softplus = jnp.maximum(neg_lam, 0.0) + jnp.log1p(jnp.exp(-jnp.abs(neg_lam)))
    coef = -RG_C * softplus

    sub_blk = lax.broadcasted_iota(jnp.int32, (SEGMENTS, bw), 0)
    if rev:
        start_rows = slice(tt - SEGMENTS, tt)
        first = (tile == n_tiles - 1) & (sub_blk == SEGMENTS - 1)
    else:
        start_rows = slice(0, SEGMENTS)
        first = (tile == 0) & (sub_blk == 0)

    for blk in range(LRU_BLOCKS):
        c0 = blk * bw
        xblk = xcb[:, c0:c0 + bw]
        pa = jnp.dot(xblk, wa_ref[dirn, blk], preferred_element_type=F32) + ba_ref[dirn:dirn + 1, c0:c0 + bw] * -LOG2_E
        px = jnp.dot(xblk, wx_ref[dirn, blk], preferred_element_type=F32) + bx_ref[dirn:dirn + 1, c0:c0 + bw] * -LOG2_E
        log_a = coef[:, c0:c0 + bw] / (1.0 + jnp.exp2(pa))
        a = jnp.exp(log_a)
        u = jnp.tanh(log_a) * (-1.0 - a * a)
        mult = jnp.where(u > 0.0, u * lax.rsqrt(u), 0.0)
        gated = xc[:, c0:c0 + bw] / (1.0 + jnp.exp2(px))
        b = mult * gated
        a_s[:, c0:c0 + bw] = a
        b_s[:, c0:c0 + bw] = b
        b_s[start_rows, c0:c0 + bw] = jnp.where(first, gated[start_rows], b[start_rows])

    steps = range(seg_len - 1, -1, -1) if rev else range(seg_len)
    segs = range(SEGMENTS - 1, -1, -1) if rev else range(SEGMENTS)
    sub1 = lax.broadcasted_iota(jnp.int32, (SEGMENTS, LANES), 0)
    for c in range(n_chunks):
        cs = slice(c * LANES, (c + 1) * LANES)
        prod = jnp.ones((SEGMENTS, LANES), F32)
        h = jnp.zeros((SEGMENTS, LANES), F32)
        for j in steps:
            aj = a_s[j * SEGMENTS:(j + 1) * SEGMENTS, cs]
            h = aj * h + b_s[j * SEGMENTS:(j + 1) * SEGMENTS, cs]
            prod = aj * prod
        state = carry[dirn:dirn + 1, cs]
        h0 = jnp.zeros((SEGMENTS, LANES), F32)
        for k in segs:
            h0 = jnp.where(sub1 == k, state, h0)
            state = prod[k:k + 1] * state + h[k:k + 1]
        carry[dirn:dirn + 1, cs] = state
        h = h0
        for j in steps:
            h = a_s[j * SEGMENTS:(j + 1) * SEGMENTS, cs] * h + b_s[j * SEGMENTS:(j + 1) * SEGMENTS, cs]
            hpad[c, pl.ds(j, SEGMENTS, stride=pitch), :] = h
        for k in range(SEGMENTS):
            h_ref[0, k * seg_len:(k + 1) * seg_len, cs] = hpad[c, k * pitch:k * pitch + seg_len, :].astype(h_ref.dtype)


def _lru_body(xfp_ref, xfc_ref, xfn_ref, xbp_ref, xbc_ref, xbn_ref,
              cw_ref, cb_ref, wa_ref, ba_ref, wx_ref, bx_ref, lam_ref,
              hf_ref, hb_ref, xpad_f, a_f, b_f, hpad_f, xpad_b, a_b, b_b, hpad_b, carry):
    i = pl.program_id(1)
    n_tiles = pl.num_programs(1)

    @pl.when(i == 0)
    def _():
        carry[...] = jnp.zeros_like(carry)

    params = (cw_ref, cb_ref, wa_ref, ba_ref, wx_ref, bx_ref, lam_ref)
    _lru_direction(xfp_ref, xfc_ref, xfn_ref, hf_ref, i, n_tiles, False, 0, *params,
                   xpad_f, a_f, b_f, hpad_f, carry)
    _lru_direction(xbp_ref, xbc_ref, xbn_ref, hb_ref, n_tiles - 1 - i, n_tiles, True, 1, *params,
                   xpad_b, a_b, b_b, hpad_b, carry)


def _lru(nat, conv_w, conv_b, w_a, b_a, w_x, b_x, lam, layer):
    b, s, _ = nat.shape
    width = conv_w.shape[-1]
    tt = LRU_ROWS
    n_tiles = s // tt
    halo_per_tile = tt // BF16_ROWS
    last_halo = s // BF16_ROWS - 1

    def tile_of(i, rev):
        return n_tiles - 1 - i if rev else i

    def specs(rev):
        return [
            pl.BlockSpec((1, BF16_ROWS, width),
                         lambda bi, i: (bi, jnp.maximum(tile_of(i, rev) * halo_per_tile - 1, 0), XLRU_COL)),
            pl.BlockSpec((1, tt, width), lambda bi, i: (bi, tile_of(i, rev), XLRU_COL)),
            pl.BlockSpec((1, BF16_ROWS, width),
                         lambda bi, i: (bi, jnp.minimum((tile_of(i, rev) + 1) * halo_per_tile, last_halo), XLRU_COL)),
        ]

    consts = (conv_w, conv_b, w_a, b_a, w_x, b_x, lam)
    padded_rows = tt + SEGMENTS * SEGMENTS
    per_direction = [pltpu.VMEM((width // LANES, padded_rows, LANES), F32),
                     pltpu.VMEM((tt, width), F32),
                     pltpu.VMEM((tt, width), F32),
                     pltpu.VMEM((width // LANES, padded_rows, LANES), F32)]
    return pl.pallas_call(
        _lru_body,
        grid=(b, n_tiles),
        in_specs=specs(False) + specs(True) + [_layer(a, layer) for a in consts],
        out_specs=[pl.BlockSpec((1, tt, width), lambda bi, i: (bi, i, 0)),
                   pl.BlockSpec((1, tt, width), lambda bi, i: (bi, n_tiles - 1 - i, 0))],
        out_shape=[jax.ShapeDtypeStruct((b, s, width), BF16)] * 2,
        scratch_shapes=per_direction + per_direction + [pltpu.VMEM((F32_ROWS, width), F32)],
        compiler_params=pltpu.CompilerParams(dimension_semantics=("parallel", "arbitrary")),
        name="lru",
    )(nat, nat, nat, nat, nat, nat, *consts)


def _token_order_rows(ref, r0, n_rows, slab):
    d, width = ref.shape[1], ref.shape[3]
    per = n_rows // d
    for r in range(d):
        piece = ref[0, r, r0 // d:r0 // d + per, :].astype(F32)
        for c in range(width // LANES):
            slab[c, pl.ds(r, per, stride=d), :] = piece[:, c * LANES:(c + 1) * LANES]
    return [slab[c] for c in range(width // LANES)]


def _merge_body(o0_ref, o1_ref, o2_ref, l0_ref, l1_ref, l2_ref, hf_ref, hb_ref, x_ref,
                ng_ref, wga_ref, wgo_ref, bg_ref, woa_ref, wol_ref, wout_ref, fg_ref, out_ref,
                o_slabs, l_slabs, *, final):
    tm, dm = x_ref.shape
    for blk, r0 in enumerate(range(0, tm, MERGE_SUB_ROWS)):
        rows = slice(r0, r0 + MERGE_SUB_ROWS)
        xn = _rms_norm(x_ref[rows], ng_ref[...]).astype(BF16)

        gatt = jnp.dot(xn, wga_ref[...], preferred_element_type=F32)
        glru, logit_att, logit_lru = (
            jnp.dot(xn, wgo_ref[:, c0:c0 + dm], preferred_element_type=F32) for c0 in (0, dm, 2 * dm))

        o_dilated = [_token_order_rows(ref, r0, MERGE_SUB_ROWS, o_slabs.at[blk, g])
                     for g, ref in enumerate((o1_ref, o2_ref))]
        l1, l2 = (_token_order_rows(ref, r0, MERGE_SUB_ROWS, l_slabs.at[blk, g])[0]
                  for g, ref in enumerate((l1_ref, l2_ref)))
        l0 = l0_ref[rows]
        mx = jnp.maximum(jnp.maximum(l0, l1), l2)
        e0, e1, e2 = jnp.exp(l0 - mx), jnp.exp(l1 - mx), jnp.exp(l2 - mx)
        inv = 1.0 / (e0 + e1 + e2)
        alphas = (e0 * inv, e1 * inv, e2 * inv)

        heads = []
        for h in range(HEADS_PER_GROUP):
            group_outs = (o0_ref[rows, h * HEAD_DIM:(h + 1) * HEAD_DIM].astype(F32), o_dilated[0][h], o_dilated[1][h])
            acc = jnp.zeros((MERGE_SUB_ROWS, HEAD_DIM), F32)
            for g in range(N_GROUPS):
                w = jnp.broadcast_to(alphas[g][:, h * LSE_REP:h * LSE_REP + 1], (MERGE_SUB_ROWS, HEAD_DIM))
                acc = acc + w * group_outs[g]
            heads.append(acc)
        mixed = jnp.concatenate(heads, axis=-1)

        y_att = (mixed * (gatt * _sigmoid(gatt))).astype(BF16)
        h_sum = hf_ref[rows].astype(F32) + hb_ref[rows].astype(F32)
        y_lru = (h_sum * (glru * _sigmoid(glru))).astype(BF16)

        p_att = jnp.dot(y_att, woa_ref[...], preferred_element_type=F32)
        p_lru = jnp.dot(y_lru, wol_ref[...], preferred_element_type=F32)
        gate_att = _sigmoid(logit_att + bg_ref[:, 0:dm])
        gate_lru = _sigmoid(logit_lru + bg_ref[:, dm:2 * dm])
        merged = (gate_att * p_att + gate_lru * p_lru).astype(BF16)
        y = x_ref[rows] + jnp.dot(merged, wout_ref[...], preferred_element_type=F32)
        if final:
            y = _rms_norm(y, fg_ref[...])
        out_ref[rows] = y


def _merge(x2d, outs, lses, h_f, h_b, norm_g, w_in, b_gate, w_o_att, w_o_lru, w_out, final_g, layer, final):
    t, dm = x2d.shape
    tm = MERGE_ROWS
    tiles_per_b = t // outs[1].shape[0] // tm
    att_gate_block = _col_block(N_GROUPS * QKV_WIDTH, GROUP_WIDTH)
    others_block = _col_block(N_GROUPS * QKV_WIDTH + GROUP_WIDTH + dm, 3 * dm)

    def rows(width):
        return pl.BlockSpec((tm, width), lambda i: (i, 0))

    def dilated_rows(a):
        d, width = a.shape[1], a.shape[3]
        return pl.BlockSpec((1, d, tm // d, width), lambda i: (i // tiles_per_b, 0, i % tiles_per_b, 0))

    n_blocks = tm // MERGE_SUB_ROWS
    return pl.pallas_call(
        functools.partial(_merge_body, final=final),
        grid=(t // tm,),
        in_specs=[rows(GROUP_WIDTH), dilated_rows(outs[1]), dilated_rows(outs[2]),
                  rows(LANES), dilated_rows(lses[1]), dilated_rows(lses[2])] + [rows(dm)] * 3
                 + [_layer(norm_g, layer), _layer(w_in, layer, cols=GROUP_WIDTH, col_block=att_gate_block),
                    _layer(w_in, layer, cols=3 * dm, col_block=others_block), _layer(b_gate, layer),
                    _layer(w_o_att, layer), _layer(w_o_lru, layer), _layer(w_out, layer),
                    pl.BlockSpec(final_g.shape, lambda i: (0, 0), pipeline_mode=pl.Buffered(1))],
        out_specs=rows(dm),
        out_shape=jax.ShapeDtypeStruct((t, dm), F32),
        scratch_shapes=[pltpu.VMEM((n_blocks, N_GROUPS - 1, GROUP_WIDTH // LANES, MERGE_SUB_ROWS, LANES), F32),
                        pltpu.VMEM((n_blocks, N_GROUPS - 1, 1, MERGE_SUB_ROWS, LANES), F32)],
        compiler_params=pltpu.CompilerParams(dimension_semantics=("parallel",)),
        name="merge_final" if final else "merge",
    )(*outs, *lses, h_f, h_b, x2d, norm_g, w_in, w_in, b_gate, w_o_att, w_o_lru, w_out, final_g)


def kernel(x, norm_g, w_in, b_gate, conv_w, conv_b, rg_w_a, rg_b_a, rg_w_x, rg_b_x, rg_lam,
           w_o_att, w_o_lru, w_out, final_g):
    b, s, dm = x.shape
    depth = w_in.shape[0]
    t = b * s
    w_in, w_o_att, w_o_lru, w_out = (w.astype(BF16) for w in (w_in, w_o_att, w_o_lru, w_out))
    rg_w_a, rg_w_x = ((w * -LOG2_E).astype(BF16) for w in (rg_w_a, rg_w_x))
    norm_g, b_gate, conv_b = (v.reshape(depth, 1, -1) for v in (norm_g, b_gate, conv_b))
    final_g = final_g.reshape(1, dm)

    x2d = x.reshape(t, dm)
    for layer in range(depth):
        nat2d, qkv4, qkv16 = _norm_proj(x2d, norm_g, w_in, layer, b)
        nat = nat2d.reshape(b, s, -1)
        outs, lses = [], []
        for group, qkv in enumerate((nat, qkv4, qkv16)):
            o, lse = _attention_group(qkv, group, ATT_PATTERNS[group][1])
            outs.append(o.reshape(t, GROUP_WIDTH) if group == 0 else o)
            lses.append(lse.reshape(t, LANES) if group == 0 else lse)
        h_f, h_b = _lru(nat, conv_w, conv_b, rg_w_a, rg_b_a, rg_w_x, rg_b_x, rg_lam, layer)
        x2d = _merge(x2d, outs, lses, h_f.reshape(t, dm), h_b.reshape(t, dm), norm_g, w_in, b_gate,
                     w_o_att, w_o_lru, w_out, final_g, layer, final=(layer == depth - 1))
    return x2d.reshape(b, s, dm)
```

```python
import functools

import jax
import jax.numpy as jnp
from jax import lax
from jax.experimental import pallas as pl
from jax.experimental.pallas import tpu as pltpu

F32 = jnp.float32
BF16 = jnp.bfloat16

HEAD_DIM = 128
HEADS_PER_GROUP = 4
ATT_PATTERNS = ((128, 1), (512, 4), (2048, 16))
N_GROUPS = len(ATT_PATTERNS)
N_ATT_HEADS = N_GROUPS * HEADS_PER_GROUP
GROUP_WIDTH = HEADS_PER_GROUP * HEAD_DIM
QKV_WIDTH = 3 * GROUP_WIDTH
NEG_INF = -1e30
LRU_BLOCKS = 4
RG_C = 8.0
NORM_EPS = 1e-6
CONV_LEFT = 2

XLRU_COL = 0
NAT_QKV_COL = 2

LANES = 128
F32_ROWS = 8
BF16_ROWS = 16
HALF_KEYS = 64
Q_SUB = 128
K_SUB = Q_SUB + 2 * HALF_KEYS
LSE_REP = LANES // HEADS_PER_GROUP
SEGMENTS = F32_ROWS

PROJ_ROWS = 1024
PROJ_COLS = 512
ATTN_Q_ROWS = {1: 2048, 4: 2048, 16: 512}
ATTN_RESIDUES = {1: 1, 4: 1, 16: 4}
LRU_ROWS = 1024
MERGE_ROWS = 1024
MERGE_SUB_ROWS = 256

LOG2_E = 1.4426950408889634
LN_2 = 0.6931471805599453


def _sigmoid(x):
    return 1.0 / (1.0 + jnp.exp2(x * -LOG2_E))


def _rms_norm(x, g):
    ms = jnp.mean(x * x, axis=-1, keepdims=True)
    return x * lax.rsqrt(ms + NORM_EPS) * g


def _col_block(first_col, cols):
    assert first_col % cols == 0, (first_col, cols)
    return first_col // cols


def _layer(stacked, layer, cols=None, col_block=0):
    block = stacked.shape[1:] if cols is None else stacked.shape[1:-1] + (cols,)
    index = (layer,) + (0,) * (len(block) - 1) + (col_block,)
    return pl.BlockSpec((None,) + block, lambda *_: index, pipeline_mode=pl.Buffered(1))


def _norm_proj_body(x_ref, g_ref, wqkv_ref, wx_ref, nat_ref, s4_ref, s16_ref, xn_ref, xn4_ref, xn16_ref, slab_ref):
    tm, dm = x_ref.shape
    width = wx_ref.shape[1]

    y = _rms_norm(x_ref[...], g_ref[...])
    xn_ref[...] = y.astype(BF16)
    for c in range(dm // LANES):
        slab_ref[c] = y[:, c * LANES:(c + 1) * LANES]
    for ref, d in ((xn4_ref, 4), (xn16_ref, 16)):
        rows = tm // d
        for c in range(dm // LANES):
            for r in range(d):
                ref[r * rows:(r + 1) * rows, c * LANES:(c + 1) * LANES] = (
                    slab_ref[c, pl.ds(r, rows, stride=d), :].astype(BF16))

    def project(lhs_ref, w_ref, col, cols):
        return jnp.dot(lhs_ref[...], w_ref[:, col:col + cols], preferred_element_type=F32).astype(BF16)

    for n0 in range(0, width, PROJ_COLS):
        nat_ref[:, n0:n0 + PROJ_COLS] = project(xn_ref, wx_ref, n0, PROJ_COLS)
    gw = GROUP_WIDTH
    for part in range(3):
        src = part * N_GROUPS * gw
        dst = part * gw
        nat_ref[:, width + dst:width + dst + gw] = project(xn_ref, wqkv_ref, src, gw)
        s4_ref[0, :, :, dst:dst + gw] = project(xn4_ref, wqkv_ref, src + gw, gw).reshape(4, tm // 4, gw)
        s16_ref[0, :, :, dst:dst + gw] = project(xn16_ref, wqkv_ref, src + 2 * gw, gw).reshape(16, tm // 16, gw)


def _norm_proj(x2d, norm_g, w_in, layer, batch):
    t, dm = x2d.shape
    s = t // batch
    n_nat = dm + QKV_WIDTH
    tm = PROJ_ROWS
    tiles_per_b = s // tm
    x_lru_block = _col_block(N_GROUPS * QKV_WIDTH + GROUP_WIDTH, dm)

    def strided_spec(d):
        return pl.BlockSpec((1, d, tm // d, QKV_WIDTH), lambda i: (i // tiles_per_b, 0, i % tiles_per_b, 0))

    return pl.pallas_call(
        _norm_proj_body,
        grid=(t // tm,),
        in_specs=[pl.BlockSpec((tm, dm), lambda i: (i, 0)), _layer(norm_g, layer),
                  _layer(w_in, layer, cols=N_GROUPS * QKV_WIDTH), _layer(w_in, layer, cols=dm, col_block=x_lru_block)],
        out_specs=[pl.BlockSpec((tm, n_nat), lambda i: (i, 0)), strided_spec(4), strided_spec(16)],
        out_shape=[jax.ShapeDtypeStruct((t, n_nat), BF16),
                   jax.ShapeDtypeStruct((batch, 4, s // 4, QKV_WIDTH), BF16),
                   jax.ShapeDtypeStruct((batch, 16, s // 16, QKV_WIDTH), BF16)],
        scratch_shapes=[pltpu.VMEM((tm, dm), BF16), pltpu.VMEM((tm, dm), BF16), pltpu.VMEM((tm, dm), BF16),
                        pltpu.VMEM((dm // LANES, tm, LANES), F32)],
        compiler_params=pltpu.CompilerParams(dimension_semantics=("parallel",)),
        name="norm_proj",
    )(x2d, norm_g, w_in, w_in)


def _attn_body(q_ref, kp_ref, kc_ref, kn_ref, vp_ref, vc_ref, vn_ref, o_ref, lse_ref,
               kcat, vcat, bias, *, group, dilation, tq, residues):
    qi = pl.program_id(1)
    r = pl.program_id(2)
    n_q = pl.num_programs(1)
    kcol = lax.broadcasted_iota(jnp.int32, (Q_SUB, K_SUB), 1)
    slopes = [2.0 ** (-8.0 * (group * HEADS_PER_GROUP + h + 1) / N_ATT_HEADS) for h in range(HEADS_PER_GROUP)]

    @pl.when((pl.program_id(0) == 0) & (qi == 0) & (r == 0))
    def _():
        qrow = lax.broadcasted_iota(jnp.int32, (Q_SUB, K_SUB), 0)
        absd = jnp.abs(kcol - HALF_KEYS - qrow)
        dist = (absd * dilation).astype(F32)
        for h in range(HEADS_PER_GROUP):
            bias[h] = jnp.where(absd <= HALF_KEYS, (-slopes[h] * LOG2_E) * dist, NEG_INF)
            vcat[:, :, (2 * h + 1) * HEAD_DIM:(2 * h + 2) * HEAD_DIM] = jnp.ones(
                (residues, tq + 2 * HALF_KEYS, HEAD_DIM), BF16)

    lane = lax.broadcasted_iota(jnp.int32, (Q_SUB, LANES), 1)
    for res in range(residues):
        blk = (0,) if dilation == 1 else (0, res)
        _attn_residue(q_ref, kp_ref, kc_ref, kn_ref, vp_ref, vc_ref, vn_ref, o_ref, lse_ref,
                      kcat.at[res], vcat.at[res], bias, blk, qi, n_q, kcol, lane, tq)


def _attn_residue(q_ref, kp_ref, kc_ref, kn_ref, vp_ref, vc_ref, vn_ref, o_ref, lse_ref,
                  kcat, vcat, bias, blk, qi, n_q, kcol, lane, tq):
    kcat[0:HALF_KEYS] = kp_ref[blk]
    kcat[HALF_KEYS:HALF_KEYS + tq] = kc_ref[blk]
    kcat[HALF_KEYS + tq:] = kn_ref[blk]
    for h in range(HEADS_PER_GROUP):
        src = slice(h * HEAD_DIM, (h + 1) * HEAD_DIM)
        dst = slice(2 * h * HEAD_DIM, (2 * h + 1) * HEAD_DIM)
        vcat[0:HALF_KEYS, dst] = vp_ref[blk + (slice(None), src)]
        vcat[HALF_KEYS:HALF_KEYS + tq, dst] = vc_ref[blk + (slice(None), src)]
        vcat[HALF_KEYS + tq:, dst] = vn_ref[blk + (slice(None), src)]

    n_sub_tiles = tq // Q_SUB
    for sub in range(n_sub_tiles):
        r0 = sub * Q_SUB
        lse_tile = jnp.zeros((Q_SUB, LANES), F32)
        for h in range(HEADS_PER_GROUP):
            c0 = h * HEAD_DIM
            q = q_ref[blk + (slice(r0, r0 + Q_SUB), slice(c0, c0 + HEAD_DIM))]
            k = kcat[r0:r0 + K_SUB, c0:c0 + HEAD_DIM]
            raw = lax.dot_general(q, k, (((1,), (1,)), ((), ())), preferred_element_type=F32)
            s2 = raw + bias[h]
            if sub == 0:
                s2 = jnp.where((qi > 0) | (kcol >= HALF_KEYS), s2, NEG_INF)
            if sub == n_sub_tiles - 1:
                s2 = jnp.where((qi < n_q - 1) | (kcol < HALF_KEYS + Q_SUB), s2, NEG_INF)
            m2 = jnp.max(s2, axis=-1, keepdims=True)
            p = jnp.exp2(s2 - m2).astype(BF16)
            acc = jnp.dot(p, vcat[r0:r0 + K_SUB, 2 * c0:2 * c0 + 2 * HEAD_DIM], preferred_element_type=F32)
            den = acc[:, HEAD_DIM:]
            o = acc[:, :HEAD_DIM] / den
            o_ref[blk + (slice(r0, r0 + Q_SUB), slice(c0, c0 + HEAD_DIM))] = o.astype(BF16)
            lse = (m2 + jnp.log2(den)) * LN_2
            lse_tile = jnp.where((lane >= h * LSE_REP) & (lane < (h + 1) * LSE_REP), lse, lse_tile)
        lse_ref[blk + (slice(r0, r0 + Q_SUB), slice(None))] = lse_tile


def _attention_group(qkv, group, dilation):
    if dilation == 1:
        b, s, _ = qkv.shape
    else:
        b, _, n_sub, _ = qkv.shape
        s = n_sub * dilation
    n_sub = s // dilation
    tq = min(ATTN_Q_ROWS[dilation], n_sub)
    residues = ATTN_RESIDUES[dilation]
    q_col = NAT_QKV_COL if dilation == 1 else 0
    halo_per_tile = tq // HALF_KEYS
    last_halo = n_sub // HALF_KEYS - 1

    def spec(rows, row_index, col, width=GROUP_WIDTH):
        if dilation == 1:
            return pl.BlockSpec((1, rows, width), lambda bi, qi, r: (bi, row_index(qi), col))
        return pl.BlockSpec((1, residues, rows, width), lambda bi, qi, r: (bi, r, row_index(qi), col))

    def main_spec(col):
        return spec(tq, lambda qi: qi, col)

    def prev_spec(col):
        return spec(HALF_KEYS, lambda qi: jnp.maximum(qi * halo_per_tile - 1, 0), col)

    def next_spec(col):
        return spec(HALF_KEYS, lambda qi: jnp.minimum((qi + 1) * halo_per_tile, last_halo), col)

    rows_shape = (b, s) if dilation == 1 else (b, dilation, n_sub)
    return pl.pallas_call(
        functools.partial(_attn_body, group=group, dilation=dilation, tq=tq, residues=residues),
        grid=(b, n_sub // tq, dilation // residues),
        in_specs=[main_spec(q_col),
                  prev_spec(q_col + 1), main_spec(q_col + 1), next_spec(q_col + 1),
                  prev_spec(q_col + 2), main_spec(q_col + 2), next_spec(q_col + 2)],
        out_specs=[main_spec(0), spec(tq, lambda qi: qi, 0, LANES)],
        out_shape=[jax.ShapeDtypeStruct(rows_shape + (GROUP_WIDTH,), BF16),
                   jax.ShapeDtypeStruct(rows_shape + (LANES,), F32)],
        scratch_shapes=[pltpu.VMEM((residues, tq + 2 * HALF_KEYS, GROUP_WIDTH), BF16),
                        pltpu.VMEM((residues, tq + 2 * HALF_KEYS, 2 * GROUP_WIDTH), BF16),
                        pltpu.VMEM((HEADS_PER_GROUP, Q_SUB, K_SUB), F32)],
        compiler_params=pltpu.CompilerParams(dimension_semantics=("arbitrary", "arbitrary", "arbitrary")),
        name=f"attn_d{dilation}",
    )(qkv, qkv, qkv, qkv, qkv, qkv, qkv)


def _lru_direction(xp_ref, xc_ref, xn_ref, h_ref, tile, n_tiles, rev, dirn,
                   cw_ref, cb_ref, wa_ref, ba_ref, wx_ref, bx_ref, lam_ref,
                   xpad, a_s, b_s, hpad, carry):
    tt, width = xc_ref.shape[1:]
    bw = width // LRU_BLOCKS
    n_chunks = width // LANES
    seg_len = tt // SEGMENTS
    pitch = seg_len + SEGMENTS
    sub = lax.broadcasted_iota(jnp.int32, (SEGMENTS, width), 0)

    xf = xc_ref[0].astype(F32)
    for c in range(n_chunks):
        for k in range(SEGMENTS):
            xpad[c, k * pitch:k * pitch + seg_len, :] = xf[k * seg_len:(k + 1) * seg_len, c * LANES:(c + 1) * LANES]
    xs = jnp.concatenate(
        [jnp.concatenate([xpad[c, pl.ds(j, SEGMENTS, stride=pitch), :] for j in range(seg_len)], axis=0)
         for c in range(n_chunks)], axis=1)

    halo_prev = jnp.where(tile == 0, 0.0, xp_ref[0, BF16_ROWS - CONV_LEFT:BF16_ROWS, :].astype(F32))
    halo_next = jnp.where(tile == n_tiles - 1, 0.0, xn_ref[0, 0:1, :].astype(F32))
    before1 = jnp.where(sub == 0, halo_prev[1:2], pltpu.roll(xs[tt - SEGMENTS:tt], 1, 0))
    before2 = jnp.where(sub == 0, halo_prev[0:1], pltpu.roll(xs[tt - 2 * SEGMENTS:tt - SEGMENTS], 1, 0))
    after1 = jnp.where(sub == SEGMENTS - 1, halo_next, pltpu.roll(xs[0:SEGMENTS], SEGMENTS - 1, 0))
    xm2 = jnp.concatenate([before2, before1, xs[:tt - 2 * SEGMENTS]], axis=0)
    xm1 = jnp.concatenate([before1, xs[:tt - SEGMENTS]], axis=0)
    xp1 = jnp.concatenate([xs[SEGMENTS:], after1], axis=0)
    xc = cb_ref[...] + xm2 * cw_ref[0:1] + xm1 * cw_ref[1:2] + xs * cw_ref[2:3] + xp1 * cw_ref[3:4]
    xcb = xc.astype(BF16)

    lam = lam_ref[dirn:dirn + 1]
    neg_lam = -lam
    softplus = jnp.maximum(neg_lam, 0.0) + jnp.log1p(jnp.exp(-jnp.abs(neg_lam)))
    coef = -RG_C * softplus

    sub_blk = lax.broadcasted_iota(jnp.int32, (SEGMENTS, bw), 0)
    if rev:
        start_rows = slice(tt - SEGMENTS, tt)
        first = (tile == n_tiles - 1) & (sub_blk == SEGMENTS - 1)
    else:
        start_rows = slice(0, SEGMENTS)
        first = (tile == 0) & (sub_blk == 0)

    for blk in range(LRU_BLOCKS):
        c0 = blk * bw
        xblk = xcb[:, c0:c0 + bw]
        pa = jnp.dot(xblk, wa_ref[dirn, blk], preferred_element_type=F32) + ba_ref[dirn:dirn + 1, c0:c0 + bw] * -LOG2_E
        px = jnp.dot(xblk, wx_ref[dirn, blk], preferred_element_type=F32) + bx_ref[dirn:dirn + 1, c0:c0 + bw] * -LOG2_E
        log_a = coef[:, c0:c0 + bw] / (1.0 + jnp.exp2(pa))
        a = jnp.exp(log_a)
        u = jnp.tanh(log_a) * (-1.0 - a * a)
        mult = jnp.where(u > 0.0, u * lax.rsqrt(u), 0.0)
        gated = xc[:, c0:c0 + bw] / (1.0 + jnp.exp2(px))
        b = mult * gated
        a_s[:, c0:c0 + bw] = a
        b_s[:, c0:c0 + bw] = b
        b_s[start_rows, c0:c0 + bw] = jnp.where(first, gated[start_rows], b[start_rows])

    steps = range(seg_len - 1, -1, -1) if rev else range(seg_len)
    segs = range(SEGMENTS - 1, -1, -1) if rev else range(SEGMENTS)
    sub1 = lax.broadcasted_iota(jnp.int32, (SEGMENTS, LANES), 0)
    for c in range(n_chunks):
        cs = slice(c * LANES, (c + 1) * LANES)
        prod = jnp.ones((SEGMENTS, LANES), F32)
        h = jnp.zeros((SEGMENTS, LANES), F32)
        for j in steps:
            aj = a_s[j * SEGMENTS:(j + 1) * SEGMENTS, cs]
            h = aj * h + b_s[j * SEGMENTS:(j + 1) * SEGMENTS, cs]
            prod = aj * prod
        state = carry[dirn:dirn + 1, cs]
        h0 = jnp.zeros((SEGMENTS, LANES), F32)
        for k in segs:
            h0 = jnp.where(sub1 == k, state, h0)
            state = prod[k:k + 1] * state + h[k:k + 1]
        carry[dirn:dirn + 1, cs] = state
        h = h0
        for j in steps:
            h = a_s[j * SEGMENTS:(j + 1) * SEGMENTS, cs] * h + b_s[j * SEGMENTS:(j + 1) * SEGMENTS, cs]
            hpad[c, pl.ds(j, SEGMENTS, stride=pitch), :] = h
        for k in range(SEGMENTS):
            h_ref[0, k * seg_len:(k + 1) * seg_len, cs] = hpad[c, k * pitch:k * pitch + seg_len, :].astype(h_ref.dtype)


def _lru_body(xfp_ref, xfc_ref, xfn_ref, xbp_ref, xbc_ref, xbn_ref,
              cw_ref, cb_ref, wa_ref, ba_ref, wx_ref, bx_ref, lam_ref,
              hf_ref, hb_ref, xpad_f, a_f, b_f, hpad_f, xpad_b, a_b, b_b, hpad_b, carry):
    i = pl.program_id(1)
    n_tiles = pl.num_programs(1)

    @pl.when(i == 0)
    def _():
        carry[...] = jnp.zeros_like(carry)

    params = (cw_ref, cb_ref, wa_ref, ba_ref, wx_ref, bx_ref, lam_ref)
    _lru_direction(xfp_ref, xfc_ref, xfn_ref, hf_ref, i, n_tiles, False, 0, *params,
                   xpad_f, a_f, b_f, hpad_f, carry)
    _lru_direction(xbp_ref, xbc_ref, xbn_ref, hb_ref, n_tiles - 1 - i, n_tiles, True, 1, *params,
                   xpad_b, a_b, b_b, hpad_b, carry)


def _lru(nat, conv_w, conv_b, w_a, b_a, w_x, b_x, lam, layer):
    b, s, _ = nat.shape
    width = conv_w.shape[-1]
    tt = LRU_ROWS
    n_tiles = s // tt
    halo_per_tile = tt // BF16_ROWS
    last_halo = s // BF16_ROWS - 1

    def tile_of(i, rev):
        return n_tiles - 1 - i if rev else i

    def specs(rev):
        return [
            pl.BlockSpec((1, BF16_ROWS, width),
                         lambda bi, i: (bi, jnp.maximum(tile_of(i, rev) * halo_per_tile - 1, 0), XLRU_COL)),
            pl.BlockSpec((1, tt, width), lambda bi, i: (bi, tile_of(i, rev), XLRU_COL)),
            pl.BlockSpec((1, BF16_ROWS, width),
                         lambda bi, i: (bi, jnp.minimum((tile_of(i, rev) + 1) * halo_per_tile, last_halo), XLRU_COL)),
        ]

    consts = (conv_w, conv_b, w_a, b_a, w_x, b_x, lam)
    padded_rows = tt + SEGMENTS * SEGMENTS
    per_direction = [pltpu.VMEM((width // LANES, padded_rows, LANES), F32),
                     pltpu.VMEM((tt, width), F32),
                     pltpu.VMEM((tt, width), F32),
                     pltpu.VMEM((width // LANES, padded_rows, LANES), F32)]
    return pl.pallas_call(
        _lru_body,
        grid=(b, n_tiles),
        in_specs=specs(False) + specs(True) + [_layer(a, layer) for a in consts],
        out_specs=[pl.BlockSpec((1, tt, width), lambda bi, i: (bi, i, 0)),
                   pl.BlockSpec((1, tt, width), lambda bi, i: (bi, n_tiles - 1 - i, 0))],
        out_shape=[jax.ShapeDtypeStruct((b, s, width), BF16)] * 2,
        scratch_shapes=per_direction + per_direction + [pltpu.VMEM((F32_ROWS, width), F32)],
        compiler_params=pltpu.CompilerParams(dimension_semantics=("parallel", "arbitrary")),
        name="lru",
    )(nat, nat, nat, nat, nat, nat, *consts)


def _token_order_rows(ref, r0, n_rows, slab):
    d, width = ref.shape[1], ref.shape[3]
    per = n_rows // d
    for r in range(d):
        piece = ref[0, r, r0 // d:r0 // d + per, :].astype(F32)
        for c in range(width // LANES):
            slab[c, pl.ds(r, per, stride=d), :] = piece[:, c * LANES:(c + 1) * LANES]
    return [slab[c] for c in range(width // LANES)]


def _merge_body(o0_ref, o1_ref, o2_ref, l0_ref, l1_ref, l2_ref, hf_ref, hb_ref, x_ref,
                ng_ref, wga_ref, wgo_ref, bg_ref, woa_ref, wol_ref, wout_ref, fg_ref, out_ref,
                o_slabs, l_slabs, *, final):
    tm, dm = x_ref.shape
    for blk, r0 in enumerate(range(0, tm, MERGE_SUB_ROWS)):
        rows = slice(r0, r0 + MERGE_SUB_ROWS)
        xn = _rms_norm(x_ref[rows], ng_ref[...]).astype(BF16)

        gatt = jnp.dot(xn, wga_ref[...], preferred_element_type=F32)
        glru, logit_att, logit_lru = (
            jnp.dot(xn, wgo_ref[:, c0:c0 + dm], preferred_element_type=F32) for c0 in (0, dm, 2 * dm))

        o_dilated = [_token_order_rows(ref, r0, MERGE_SUB_ROWS, o_slabs.at[blk, g])
                     for g, ref in enumerate((o1_ref, o2_ref))]
        l1, l2 = (_token_order_rows(ref, r0, MERGE_SUB_ROWS, l_slabs.at[blk, g])[0]
                  for g, ref in enumerate((l1_ref, l2_ref)))
        l0 = l0_ref[rows]
        mx = jnp.maximum(jnp.maximum(l0, l1), l2)
        e0, e1, e2 = jnp.exp(l0 - mx), jnp.exp(l1 - mx), jnp.exp(l2 - mx)
        inv = 1.0 / (e0 + e1 + e2)
        alphas = (e0 * inv, e1 * inv, e2 * inv)

        heads = []
        for h in range(HEADS_PER_GROUP):
            group_outs = (o0_ref[rows, h * HEAD_DIM:(h + 1) * HEAD_DIM].astype(F32), o_dilated[0][h], o_dilated[1][h])
            acc = jnp.zeros((MERGE_SUB_ROWS, HEAD_DIM), F32)
            for g in range(N_GROUPS):
                w = jnp.broadcast_to(alphas[g][:, h * LSE_REP:h * LSE_REP + 1], (MERGE_SUB_ROWS, HEAD_DIM))
                acc = acc + w * group_outs[g]
            heads.append(acc)
        mixed = jnp.concatenate(heads, axis=-1)

        y_att = (mixed * (gatt * _sigmoid(gatt))).astype(BF16)
        h_sum = hf_ref[rows].astype(F32) + hb_ref[rows].astype(F32)
        y_lru = (h_sum * (glru * _sigmoid(glru))).astype(BF16)

        p_att = jnp.dot(y_att, woa_ref[...], preferred_element_type=F32)
        p_lru = jnp.dot(y_lru, wol_ref[...], preferred_element_type=F32)
        gate_att = 1.0 / (1.0 + jnp.exp2(logit_att + bg_ref[:, 0:dm] * -LOG2_E))
        gate_lru = 1.0 / (1.0 + jnp.exp2(logit_lru + bg_ref[:, dm:2 * dm] * -LOG2_E))
        merged = (gate_att * p_att + gate_lru * p_lru).astype(BF16)
        y = x_ref[rows] + jnp.dot(merged, wout_ref[...], preferred_element_type=F32)
        if final:
            y = _rms_norm(y, fg_ref[...])
        out_ref[rows] = y


def _merge(x2d, outs, lses, h_f, h_b, norm_g, w_in, b_gate, w_o_att, w_o_lru, w_out, final_g, layer, final):
    t, dm = x2d.shape
    tm = MERGE_ROWS
    tiles_per_b = t // outs[1].shape[0] // tm
    att_gate_block = _col_block(N_GROUPS * QKV_WIDTH, GROUP_WIDTH)
    others_block = _col_block(N_GROUPS * QKV_WIDTH + GROUP_WIDTH + dm, 3 * dm)

    def rows(width):
        return pl.BlockSpec((tm, width), lambda i: (i, 0))

    def dilated_rows(a):
        d, width = a.shape[1], a.shape[3]
        return pl.BlockSpec((1, d, tm // d, width), lambda i: (i // tiles_per_b, 0, i % tiles_per_b, 0))

    n_blocks = tm // MERGE_SUB_ROWS
    return pl.pallas_call(
        functools.partial(_merge_body, final=final),
        grid=(t // tm,),
        in_specs=[rows(GROUP_WIDTH), dilated_rows(outs[1]), dilated_rows(outs[2]),
                  rows(LANES), dilated_rows(lses[1]), dilated_rows(lses[2])] + [rows(dm)] * 3
                 + [_layer(norm_g, layer), _layer(w_in, layer, cols=GROUP_WIDTH, col_block=att_gate_block),
                    _layer(w_in, layer, cols=3 * dm, col_block=others_block), _layer(b_gate, layer),
                    _layer(w_o_att, layer), _layer(w_o_lru, layer), _layer(w_out, layer),
                    pl.BlockSpec(final_g.shape, lambda i: (0, 0), pipeline_mode=pl.Buffered(1))],
        out_specs=rows(dm),
        out_shape=jax.ShapeDtypeStruct((t, dm), F32),
        scratch_shapes=[pltpu.VMEM((n_blocks, N_GROUPS - 1, GROUP_WIDTH // LANES, MERGE_SUB_ROWS, LANES), F32),
                        pltpu.VMEM((n_blocks, N_GROUPS - 1, 1, MERGE_SUB_ROWS, LANES), F32)],
        compiler_params=pltpu.CompilerParams(dimension_semantics=("parallel",)),
        name="merge_final" if final else "merge",
    )(*outs, *lses, h_f, h_b, x2d, norm_g, w_in, w_in, b_gate, w_o_att, w_o_lru, w_out, final_g)


def kernel(x, norm_g, w_in, b_gate, conv_w, conv_b, rg_w_a, rg_b_a, rg_w_x, rg_b_x, rg_lam,
           w_o_att, w_o_lru, w_out, final_g):
    b, s, dm = x.shape
    depth = w_in.shape[0]
    t = b * s
    w_o_att, w_o_lru, w_out = (w.astype(BF16) for w in (w_o_att, w_o_lru, w_out))
    n_in = w_in.shape[-1]
    col = jnp.arange(n_in)
    col_scale = jnp.where(col < N_GROUPS * GROUP_WIDTH, HEAD_DIM ** -0.5 * LOG2_E,
                          jnp.where(col >= n_in - 2 * dm, -LOG2_E, 1.0)).astype(F32)
    w_in = (w_in * col_scale).astype(BF16)
    rg_w_a, rg_w_x = ((w * -LOG2_E).astype(BF16) for w in (rg_w_a, rg_w_x))
    norm_g, b_gate, conv_b = (v.reshape(depth, 1, -1) for v in (norm_g, b_gate, conv_b))
    final_g = final_g.reshape(1, dm)

    x2d = x.reshape(t, dm)
    for layer in range(depth):
        nat2d, qkv4, qkv16 = _norm_proj(x2d, norm_g, w_in, layer, b)
        nat = nat2d.reshape(b, s, -1)
        outs, lses = [], []
        for group, qkv in enumerate((nat, qkv4, qkv16)):
            o, lse = _attention_group(qkv, group, ATT_PATTERNS[group][1])
            outs.append(o.reshape(t, GROUP_WIDTH) if group == 0 else o)
            lses.append(lse.reshape(t, LANES) if group == 0 else lse)
        h_f, h_b = _lru(nat, conv_w, conv_b, rg_w_a, rg_b_a, rg_w_x, rg_b_x, rg_lam, layer)
        x2d = _merge(x2d, outs, lses, h_f.reshape(t, dm), h_b.reshape(t, dm), norm_g, w_in, b_gate,
                     w_o_att, w_o_lru, w_out, final_g, layer, final=(layer == depth - 1))
    return x2d.reshape(b, s, dm)
```

```python
import functools

import jax
import jax.numpy as jnp
from jax import lax
from jax.experimental import pallas as pl
from jax.experimental.pallas import tpu as pltpu

F32 = jnp.float32
BF16 = jnp.bfloat16

HEAD_DIM = 128
HEADS_PER_GROUP = 4
ATT_PATTERNS = ((128, 1), (512, 4), (2048, 16))
N_GROUPS = len(ATT_PATTERNS)
N_ATT_HEADS = N_GROUPS * HEADS_PER_GROUP
GROUP_WIDTH = HEADS_PER_GROUP * HEAD_DIM
QKV_WIDTH = 3 * GROUP_WIDTH
NEG_INF = -1e30
LRU_BLOCKS = 4
RG_C = 8.0
NORM_EPS = 1e-6
CONV_LEFT = 2

XLRU_COL = 0
NAT_QKV_COL = 2

LANES = 128
F32_ROWS = 8
BF16_ROWS = 16
HALF_KEYS = 64
Q_SUB = 128
K_SUB = Q_SUB + 2 * HALF_KEYS
LSE_REP = LANES // HEADS_PER_GROUP
SEGMENTS = F32_ROWS

PROJ_ROWS = 1024
PROJ_COLS = 512
ATTN_Q_ROWS = {1: 2048, 4: 2048, 16: 512}
ATTN_RESIDUES = {1: 1, 4: 1, 16: 4}
LRU_ROWS = 1024
MERGE_ROWS = 1024
MERGE_SUB_ROWS = 256

LOG2_E = 1.4426950408889634
LN_2 = 0.6931471805599453


def _sigmoid(x):
    return 1.0 / (1.0 + jnp.exp2(x * -LOG2_E))


def _rms_norm(x, g):
    ms = jnp.mean(x * x, axis=-1, keepdims=True)
    return x * lax.rsqrt(ms + NORM_EPS) * g


def _col_block(first_col, cols):
    assert first_col % cols == 0, (first_col, cols)
    return first_col // cols


def _layer(stacked, layer, cols=None, col_block=0):
    block = stacked.shape[1:] if cols is None else stacked.shape[1:-1] + (cols,)
    index = (layer,) + (0,) * (len(block) - 1) + (col_block,)
    return pl.BlockSpec((None,) + block, lambda *_: index, pipeline_mode=pl.Buffered(1))


DILATIONS = tuple(d for _, d in ATT_PATTERNS)
assert all(w // (2 * d) == HALF_KEYS for w, d in ATT_PATTERNS)


def _norm_proj_body(x_ref, g_ref, wqkv_ref, wx_ref, nat_ref, *refs):
    dilated_out = refs[:N_GROUPS - 1]
    xn_ref, *dilated_lhs, slab_ref = refs[N_GROUPS - 1:]
    tm, dm = x_ref.shape
    width = wx_ref.shape[1]
    gw = GROUP_WIDTH

    y = _rms_norm(x_ref[...], g_ref[...])
    xn_ref[...] = y.astype(BF16)
    for c in range(dm // LANES):
        slab_ref[c] = y[:, c * LANES:(c + 1) * LANES]

    def project(lhs_ref, w_ref, col, cols):
        return jnp.dot(lhs_ref[...], w_ref[:, col:col + cols], preferred_element_type=F32).astype(BF16)

    for n0 in range(0, width, PROJ_COLS):
        nat_ref[:, n0:n0 + PROJ_COLS] = project(xn_ref, wx_ref, n0, PROJ_COLS)
    for part in range(3):
        nat_ref[:, width + part * gw:width + (part + 1) * gw] = project(xn_ref, wqkv_ref, part * N_GROUPS * gw, gw)

    for group, (lhs_ref, out_ref) in enumerate(zip(dilated_lhs, dilated_out), start=1):
        d = DILATIONS[group]
        rows = tm // d
        for c in range(dm // LANES):
            for r in range(d):
                lhs_ref[r * rows:(r + 1) * rows, c * LANES:(c + 1) * LANES] = (
                    slab_ref[c, pl.ds(r, rows, stride=d), :].astype(BF16))
        for part in range(3):
            out_ref[0, :, :, part * gw:(part + 1) * gw] = (
                project(lhs_ref, wqkv_ref, (part * N_GROUPS + group) * gw, gw).reshape(d, rows, gw))


def _norm_proj(x2d, norm_g, w_in, layer, batch):
    t, dm = x2d.shape
    s = t // batch
    n_nat = dm + QKV_WIDTH
    tm = PROJ_ROWS
    tiles_per_b = s // tm
    x_lru_block = _col_block(N_GROUPS * QKV_WIDTH + GROUP_WIDTH, dm)

    def strided_spec(d):
        return pl.BlockSpec((1, d, tm // d, QKV_WIDTH), lambda i: (i // tiles_per_b, 0, i % tiles_per_b, 0))

    return pl.pallas_call(
        _norm_proj_body,
        grid=(t // tm,),
        in_specs=[pl.BlockSpec((tm, dm), lambda i: (i, 0)), _layer(norm_g, layer),
                  _layer(w_in, layer, cols=N_GROUPS * QKV_WIDTH), _layer(w_in, layer, cols=dm, col_block=x_lru_block)],
        out_specs=[pl.BlockSpec((tm, n_nat), lambda i: (i, 0))] + [strided_spec(d) for d in DILATIONS[1:]],
        out_shape=[jax.ShapeDtypeStruct((t, n_nat), BF16)]
                  + [jax.ShapeDtypeStruct((batch, d, s // d, QKV_WIDTH), BF16) for d in DILATIONS[1:]],
        scratch_shapes=[pltpu.VMEM((tm, dm), BF16)] * N_GROUPS + [pltpu.VMEM((dm // LANES, tm, LANES), F32)],
        compiler_params=pltpu.CompilerParams(dimension_semantics=("parallel",)),
        name="norm_proj",
    )(x2d, norm_g, w_in, w_in)


def _attn_body(q_ref, kp_ref, kc_ref, kn_ref, vp_ref, vc_ref, vn_ref, o_ref, lse_ref,
               kcat, vcat, bias, *, group, dilation, tq, residues):
    qi = pl.program_id(1)
    r = pl.program_id(2)
    n_q = pl.num_programs(1)
    kcol = lax.broadcasted_iota(jnp.int32, (Q_SUB, K_SUB), 1)
    slopes = [2.0 ** (-8.0 * (group * HEADS_PER_GROUP + h + 1) / N_ATT_HEADS) for h in range(HEADS_PER_GROUP)]

    @pl.when((pl.program_id(0) == 0) & (qi == 0) & (r == 0))
    def _():
        qrow = lax.broadcasted_iota(jnp.int32, (Q_SUB, K_SUB), 0)
        absd = jnp.abs(kcol - HALF_KEYS - qrow)
        dist = (absd * dilation).astype(F32)
        for h in range(HEADS_PER_GROUP):
            bias[h] = jnp.where(absd <= HALF_KEYS, (-slopes[h] * LOG2_E) * dist, NEG_INF)
            vcat[:, :, (2 * h + 1) * HEAD_DIM:(2 * h + 2) * HEAD_DIM] = jnp.ones(
                (residues, tq + 2 * HALF_KEYS, HEAD_DIM), BF16)

    lane = lax.broadcasted_iota(jnp.int32, (Q_SUB, LANES), 1)
    for res in range(residues):
        blk = (0,) if dilation == 1 else (0, res)
        _attn_residue(q_ref, kp_ref, kc_ref, kn_ref, vp_ref, vc_ref, vn_ref, o_ref, lse_ref,
                      kcat.at[res], vcat.at[res], bias, blk, qi, n_q, kcol, lane, tq)


def _attn_residue(q_ref, kp_ref, kc_ref, kn_ref, vp_ref, vc_ref, vn_ref, o_ref, lse_ref,
                  kcat, vcat, bias, blk, qi, n_q, kcol, lane, tq):
    kcat[0:HALF_KEYS] = kp_ref[blk]
    kcat[HALF_KEYS:HALF_KEYS + tq] = kc_ref[blk]
    kcat[HALF_KEYS + tq:] = kn_ref[blk]
    for h in range(HEADS_PER_GROUP):
        src = slice(h * HEAD_DIM, (h + 1) * HEAD_DIM)
        dst = slice(2 * h * HEAD_DIM, (2 * h + 1) * HEAD_DIM)
        vcat[0:HALF_KEYS, dst] = vp_ref[blk + (slice(None), src)]
        vcat[HALF_KEYS:HALF_KEYS + tq, dst] = vc_ref[blk + (slice(None), src)]
        vcat[HALF_KEYS + tq:, dst] = vn_ref[blk + (slice(None), src)]

    n_sub_tiles = tq // Q_SUB
    for sub in range(n_sub_tiles):
        r0 = sub * Q_SUB
        lse_tile = jnp.zeros((Q_SUB, LANES), F32)
        for h in range(HEADS_PER_GROUP):
            c0 = h * HEAD_DIM
            q = q_ref[blk + (slice(r0, r0 + Q_SUB), slice(c0, c0 + HEAD_DIM))]
            k = kcat[r0:r0 + K_SUB, c0:c0 + HEAD_DIM]
            raw = lax.dot_general(q, k, (((1,), (1,)), ((), ())), preferred_element_type=F32)
            s2 = raw + bias[h]
            if sub == 0:
                s2 = jnp.where((qi > 0) | (kcol >= HALF_KEYS), s2, NEG_INF)
            if sub == n_sub_tiles - 1:
                s2 = jnp.where((qi < n_q - 1) | (kcol < HALF_KEYS + Q_SUB), s2, NEG_INF)
            m2 = jnp.max(s2, axis=-1, keepdims=True)
            p = jnp.exp2(s2 - m2).astype(BF16)
            acc = jnp.dot(p, vcat[r0:r0 + K_SUB, 2 * c0:2 * c0 + 2 * HEAD_DIM], preferred_element_type=F32)
            den = acc[:, HEAD_DIM:]
            o = acc[:, :HEAD_DIM] / den
            o_ref[blk + (slice(r0, r0 + Q_SUB), slice(c0, c0 + HEAD_DIM))] = o.astype(BF16)
            lse = (m2 + jnp.log2(den)) * LN_2
            lse_tile = jnp.where((lane >= h * LSE_REP) & (lane < (h + 1) * LSE_REP), lse, lse_tile)
        lse_ref[blk + (slice(r0, r0 + Q_SUB), slice(None))] = lse_tile


def _attention_group(qkv, group, dilation):
    if dilation == 1:
        b, s, _ = qkv.shape
    else:
        b, _, n_sub, _ = qkv.shape
        s = n_sub * dilation
    n_sub = s // dilation
    tq = min(ATTN_Q_ROWS[dilation], n_sub)
    residues = ATTN_RESIDUES[dilation]
    q_col = NAT_QKV_COL if dilation == 1 else 0
    halo_per_tile = tq // HALF_KEYS
    last_halo = n_sub // HALF_KEYS - 1

    def spec(rows, row_index, col, width=GROUP_WIDTH):
        if dilation == 1:
            return pl.BlockSpec((1, rows, width), lambda bi, qi, r: (bi, row_index(qi), col))
        return pl.BlockSpec((1, residues, rows, width), lambda bi, qi, r: (bi, r, row_index(qi), col))

    def main_spec(col):
        return spec(tq, lambda qi: qi, col)

    def prev_spec(col):
        return spec(HALF_KEYS, lambda qi: jnp.maximum(qi * halo_per_tile - 1, 0), col)

    def next_spec(col):
        return spec(HALF_KEYS, lambda qi: jnp.minimum((qi + 1) * halo_per_tile, last_halo), col)

    rows_shape = (b, s) if dilation == 1 else (b, dilation, n_sub)
    return pl.pallas_call(
        functools.partial(_attn_body, group=group, dilation=dilation, tq=tq, residues=residues),
        grid=(b, n_sub // tq, dilation // residues),
        in_specs=[main_spec(q_col),
                  prev_spec(q_col + 1), main_spec(q_col + 1), next_spec(q_col + 1),
                  prev_spec(q_col + 2), main_spec(q_col + 2), next_spec(q_col + 2)],
        out_specs=[main_spec(0), spec(tq, lambda qi: qi, 0, LANES)],
        out_shape=[jax.ShapeDtypeStruct(rows_shape + (GROUP_WIDTH,), BF16),
                   jax.ShapeDtypeStruct(rows_shape + (LANES,), F32)],
        scratch_shapes=[pltpu.VMEM((residues, tq + 2 * HALF_KEYS, GROUP_WIDTH), BF16),
                        pltpu.VMEM((residues, tq + 2 * HALF_KEYS, 2 * GROUP_WIDTH), BF16),
                        pltpu.VMEM((HEADS_PER_GROUP, Q_SUB, K_SUB), F32)],
        compiler_params=pltpu.CompilerParams(dimension_semantics=("arbitrary", "arbitrary", "arbitrary")),
        name=f"attn_d{dilation}",
    )(qkv, qkv, qkv, qkv, qkv, qkv, qkv)


def _lru_direction(xp_ref, xc_ref, xn_ref, h_ref, tile, n_tiles, rev, dirn,
                   cw_ref, cb_ref, wa_ref, ba_ref, wx_ref, bx_ref, lam_ref,
                   xpad, a_s, b_s, hpad, carry):
    tt, width = xc_ref.shape[1:]
    bw = width // LRU_BLOCKS
    n_chunks = width // LANES
    seg_len = tt // SEGMENTS
    pitch = seg_len + SEGMENTS
    sub = lax.broadcasted_iota(jnp.int32, (SEGMENTS, width), 0)

    xf = xc_ref[0].astype(F32)
    for c in range(n_chunks):
        for k in range(SEGMENTS):
            xpad[c, k * pitch:k * pitch + seg_len, :] = xf[k * seg_len:(k + 1) * seg_len, c * LANES:(c + 1) * LANES]
    xs = jnp.concatenate(
        [jnp.concatenate([xpad[c, pl.ds(j, SEGMENTS, stride=pitch), :] for j in range(seg_len)], axis=0)
         for c in range(n_chunks)], axis=1)

    halo_prev = jnp.where(tile == 0, 0.0, xp_ref[0, BF16_ROWS - CONV_LEFT:BF16_ROWS, :].astype(F32))
    halo_next = jnp.where(tile == n_tiles - 1, 0.0, xn_ref[0, 0:1, :].astype(F32))
    before1 = jnp.where(sub == 0, halo_prev[1:2], pltpu.roll(xs[tt - SEGMENTS:tt], 1, 0))
    before2 = jnp.where(sub == 0, halo_prev[0:1], pltpu.roll(xs[tt - 2 * SEGMENTS:tt - SEGMENTS], 1, 0))
    after1 = jnp.where(sub == SEGMENTS - 1, halo_next, pltpu.roll(xs[0:SEGMENTS], SEGMENTS - 1, 0))
    xm2 = jnp.concatenate([before2, before1, xs[:tt - 2 * SEGMENTS]], axis=0)
    xm1 = jnp.concatenate([before1, xs[:tt - SEGMENTS]], axis=0)
    xp1 = jnp.concatenate([xs[SEGMENTS:], after1], axis=0)
    xc = cb_ref[...] + xm2 * cw_ref[0:1] + xm1 * cw_ref[1:2] + xs * cw_ref[2:3] + xp1 * cw_ref[3:4]
    xcb = xc.astype(BF16)

    lam = lam_ref[dirn:dirn + 1]
    neg_lam = -lam
    softplus = jnp.maximum(neg_lam, 0.0) + jnp.log1p(jnp.exp(-jnp.abs(neg_lam)))
    coef = -RG_C * softplus

    sub_blk = lax.broadcasted_iota(jnp.int32, (SEGMENTS, bw), 0)
    if rev:
        start_rows = slice(tt - SEGMENTS, tt)
        first = (tile == n_tiles - 1) & (sub_blk == SEGMENTS - 1)
    else:
        start_rows = slice(0, SEGMENTS)
        first = (tile == 0) & (sub_blk == 0)

    for blk in range(LRU_BLOCKS):
        c0 = blk * bw
        xblk = xcb[:, c0:c0 + bw]
        pa = jnp.dot(xblk, wa_ref[dirn, blk], preferred_element_type=F32) + ba_ref[dirn:dirn + 1, c0:c0 + bw] * -LOG2_E
        px = jnp.dot(xblk, wx_ref[dirn, blk], preferred_element_type=F32) + bx_ref[dirn:dirn + 1, c0:c0 + bw] * -LOG2_E
        log_a = coef[:, c0:c0 + bw] / (1.0 + jnp.exp2(pa))
        a = jnp.exp(log_a)
        u = jnp.tanh(log_a) * (-1.0 - a * a)
        mult = jnp.where(u > 0.0, u * lax.rsqrt(u), 0.0)
        gated = xc[:, c0:c0 + bw] / (1.0 + jnp.exp2(px))
        b = mult * gated
        a_s[:, c0:c0 + bw] = a
        b_s[:, c0:c0 + bw] = b
        b_s[start_rows, c0:c0 + bw] = jnp.where(first, gated[start_rows], b[start_rows])

    steps = range(seg_len - 1, -1, -1) if rev else range(seg_len)
    segs = range(SEGMENTS - 1, -1, -1) if rev else range(SEGMENTS)
    sub1 = lax.broadcasted_iota(jnp.int32, (SEGMENTS, LANES), 0)
    for c in range(n_chunks):
        cs = slice(c * LANES, (c + 1) * LANES)
        prod = jnp.ones((SEGMENTS, LANES), F32)
        h = jnp.zeros((SEGMENTS, LANES), F32)
        for j in steps:
            aj = a_s[j * SEGMENTS:(j + 1) * SEGMENTS, cs]
            h = aj * h + b_s[j * SEGMENTS:(j + 1) * SEGMENTS, cs]
            prod = aj * prod
        state = carry[dirn:dirn + 1, cs]
        h0 = jnp.zeros((SEGMENTS, LANES), F32)
        for k in segs:
            h0 = jnp.where(sub1 == k, state, h0)
            state = prod[k:k + 1] * state + h[k:k + 1]
        carry[dirn:dirn + 1, cs] = state
        h = h0
        for j in steps:
            h = a_s[j * SEGMENTS:(j + 1) * SEGMENTS, cs] * h + b_s[j * SEGMENTS:(j + 1) * SEGMENTS, cs]
            hpad[c, pl.ds(j, SEGMENTS, stride=pitch), :] = h
        for k in range(SEGMENTS):
            h_ref[0, k * seg_len:(k + 1) * seg_len, cs] = hpad[c, k * pitch:k * pitch + seg_len, :].astype(h_ref.dtype)


def _lru_body(xfp_ref, xfc_ref, xfn_ref, xbp_ref, xbc_ref, xbn_ref,
              cw_ref, cb_ref, wa_ref, ba_ref, wx_ref, bx_ref, lam_ref,
              hf_ref, hb_ref, xpad_f, a_f, b_f, hpad_f, xpad_b, a_b, b_b, hpad_b, carry):
    i = pl.program_id(1)
    n_tiles = pl.num_programs(1)

    @pl.when(i == 0)
    def _():
        carry[...] = jnp.zeros_like(carry)

    params = (cw_ref, cb_ref, wa_ref, ba_ref, wx_ref, bx_ref, lam_ref)
    _lru_direction(xfp_ref, xfc_ref, xfn_ref, hf_ref, i, n_tiles, False, 0, *params,
                   xpad_f, a_f, b_f, hpad_f, carry)
    _lru_direction(xbp_ref, xbc_ref, xbn_ref, hb_ref, n_tiles - 1 - i, n_tiles, True, 1, *params,
                   xpad_b, a_b, b_b, hpad_b, carry)


def _lru(nat, conv_w, conv_b, w_a, b_a, w_x, b_x, lam, layer):
    b, s, _ = nat.shape
    width = conv_w.shape[-1]
    tt = LRU_ROWS
    n_tiles = s // tt
    halo_per_tile = tt // BF16_ROWS
    last_halo = s // BF16_ROWS - 1

    def tile_of(i, rev):
        return n_tiles - 1 - i if rev else i

    def specs(rev):
        return [
            pl.BlockSpec((1, BF16_ROWS, width),
                         lambda bi, i: (bi, jnp.maximum(tile_of(i, rev) * halo_per_tile - 1, 0), XLRU_COL)),
            pl.BlockSpec((1, tt, width), lambda bi, i: (bi, tile_of(i, rev), XLRU_COL)),
            pl.BlockSpec((1, BF16_ROWS, width),
                         lambda bi, i: (bi, jnp.minimum((tile_of(i, rev) + 1) * halo_per_tile, last_halo), XLRU_COL)),
        ]

    consts = (conv_w, conv_b, w_a, b_a, w_x, b_x, lam)
    padded_rows = tt + SEGMENTS * SEGMENTS
    per_direction = [pltpu.VMEM((width // LANES, padded_rows, LANES), F32),
                     pltpu.VMEM((tt, width), F32),
                     pltpu.VMEM((tt, width), F32),
                     pltpu.VMEM((width // LANES, padded_rows, LANES), F32)]
    return pl.pallas_call(
        _lru_body,
        grid=(b, n_tiles),
        in_specs=specs(False) + specs(True) + [_layer(a, layer) for a in consts],
        out_specs=[pl.BlockSpec((1, tt, width), lambda bi, i: (bi, i, 0)),
                   pl.BlockSpec((1, tt, width), lambda bi, i: (bi, n_tiles - 1 - i, 0))],
        out_shape=[jax.ShapeDtypeStruct((b, s, width), BF16)] * 2,
        scratch_shapes=per_direction + per_direction + [pltpu.VMEM((F32_ROWS, width), F32)],
        compiler_params=pltpu.CompilerParams(dimension_semantics=("parallel", "arbitrary")),
        name="lru",
    )(nat, nat, nat, nat, nat, nat, *consts)


def _token_order_rows(ref, r0, n_rows, slab):
    d, width = ref.shape[1], ref.shape[3]
    per = n_rows // d
    for r in range(d):
        piece = ref[0, r, r0 // d:r0 // d + per, :].astype(F32)
        for c in range(width // LANES):
            slab[c, pl.ds(r, per, stride=d), :] = piece[:, c * LANES:(c + 1) * LANES]
    return [slab[c] for c in range(width // LANES)]


def _merge_body(o0_ref, o1_ref, o2_ref, l0_ref, l1_ref, l2_ref, hf_ref, hb_ref, x_ref,
                ng_ref, wga_ref, wgo_ref, bg_ref, woa_ref, wol_ref, wout_ref, fg_ref, out_ref,
                o_slabs, l_slabs, *, final):
    tm, dm = x_ref.shape
    for blk, r0 in enumerate(range(0, tm, MERGE_SUB_ROWS)):
        rows = slice(r0, r0 + MERGE_SUB_ROWS)
        xn = _rms_norm(x_ref[rows], ng_ref[...]).astype(BF16)

        gatt = jnp.dot(xn, wga_ref[...], preferred_element_type=F32)
        glru, logit_att, logit_lru = (
            jnp.dot(xn, wgo_ref[:, c0:c0 + dm], preferred_element_type=F32) for c0 in (0, dm, 2 * dm))

        o_dilated = [_token_order_rows(ref, r0, MERGE_SUB_ROWS, o_slabs.at[blk, g])
                     for g, ref in enumerate((o1_ref, o2_ref))]
        l1, l2 = (_token_order_rows(ref, r0, MERGE_SUB_ROWS, l_slabs.at[blk, g])[0]
                  for g, ref in enumerate((l1_ref, l2_ref)))
        l0 = l0_ref[rows]
        mx = jnp.maximum(jnp.maximum(l0, l1), l2)
        e0, e1, e2 = jnp.exp(l0 - mx), jnp.exp(l1 - mx), jnp.exp(l2 - mx)
        inv = 1.0 / (e0 + e1 + e2)
        alphas = (e0 * inv, e1 * inv, e2 * inv)

        heads = []
        for h in range(HEADS_PER_GROUP):
            group_outs = (o0_ref[rows, h * HEAD_DIM:(h + 1) * HEAD_DIM].astype(F32), o_dilated[0][h], o_dilated[1][h])
            acc = jnp.zeros((MERGE_SUB_ROWS, HEAD_DIM), F32)
            for g in range(N_GROUPS):
                w = jnp.broadcast_to(alphas[g][:, h * LSE_REP:h * LSE_REP + 1], (MERGE_SUB_ROWS, HEAD_DIM))
                acc = acc + w * group_outs[g]
            heads.append(acc)
        mixed = jnp.concatenate(heads, axis=-1)

        y_att = (mixed * (gatt * _sigmoid(gatt))).astype(BF16)
        h_sum = hf_ref[rows].astype(F32) + hb_ref[rows].astype(F32)
        y_lru = (h_sum * (glru * _sigmoid(glru))).astype(BF16)

        p_att = jnp.dot(y_att, woa_ref[...], preferred_element_type=F32)
        p_lru = jnp.dot(y_lru, wol_ref[...], preferred_element_type=F32)
        gate_att = 1.0 / (1.0 + jnp.exp2(logit_att + bg_ref[:, 0:dm] * -LOG2_E))
        gate_lru = 1.0 / (1.0 + jnp.exp2(logit_lru + bg_ref[:, dm:2 * dm] * -LOG2_E))
        merged = (gate_att * p_att + gate_lru * p_lru).astype(BF16)
        y = x_ref[rows] + jnp.dot(merged, wout_ref[...], preferred_element_type=F32)
        if final:
            y = _rms_norm(y, fg_ref[...])
        out_ref[rows] = y


def _merge(x2d, outs, lses, h_f, h_b, norm_g, w_in, b_gate, w_o_att, w_o_lru, w_out, final_g, layer, final):
    t, dm = x2d.shape
    tm = MERGE_ROWS
    tiles_per_b = t // outs[1].shape[0] // tm
    att_gate_block = _col_block(N_GROUPS * QKV_WIDTH, GROUP_WIDTH)
    others_block = _col_block(N_GROUPS * QKV_WIDTH + GROUP_WIDTH + dm, 3 * dm)

    def rows(width):
        return pl.BlockSpec((tm, width), lambda i: (i, 0))

    def dilated_rows(a):
        d, width = a.shape[1], a.shape[3]
        return pl.BlockSpec((1, d, tm // d, width), lambda i: (i // tiles_per_b, 0, i % tiles_per_b, 0))

    n_blocks = tm // MERGE_SUB_ROWS
    return pl.pallas_call(
        functools.partial(_merge_body, final=final),
        grid=(t // tm,),
        in_specs=[rows(GROUP_WIDTH), dilated_rows(outs[1]), dilated_rows(outs[2]),
                  rows(LANES), dilated_rows(lses[1]), dilated_rows(lses[2])] + [rows(dm)] * 3
                 + [_layer(norm_g, layer), _layer(w_in, layer, cols=GROUP_WIDTH, col_block=att_gate_block),
                    _layer(w_in, layer, cols=3 * dm, col_block=others_block), _layer(b_gate, layer),
                    _layer(w_o_att, layer), _layer(w_o_lru, layer), _layer(w_out, layer),
                    pl.BlockSpec(final_g.shape, lambda i: (0, 0), pipeline_mode=pl.Buffered(1))],
        out_specs=rows(dm),
        out_shape=jax.ShapeDtypeStruct((t, dm), F32),
        scratch_shapes=[pltpu.VMEM((n_blocks, N_GROUPS - 1, GROUP_WIDTH // LANES, MERGE_SUB_ROWS, LANES), F32),
                        pltpu.VMEM((n_blocks, N_GROUPS - 1, 1, MERGE_SUB_ROWS, LANES), F32)],
        compiler_params=pltpu.CompilerParams(dimension_semantics=("parallel",)),
        name="merge_final" if final else "merge",
    )(*outs, *lses, h_f, h_b, x2d, norm_g, w_in, w_in, b_gate, w_o_att, w_o_lru, w_out, final_g)


def kernel(x, norm_g, w_in, b_gate, conv_w, conv_b, rg_w_a, rg_b_a, rg_w_x, rg_b_x, rg_lam,
           w_o_att, w_o_lru, w_out, final_g):
    b, s, dm = x.shape
    depth = w_in.shape[0]
    t = b * s
    w_o_att, w_o_lru, w_out = (w.astype(BF16) for w in (w_o_att, w_o_lru, w_out))
    n_in = w_in.shape[-1]
    col = jnp.arange(n_in)
    col_scale = jnp.where(col < N_GROUPS * GROUP_WIDTH, HEAD_DIM ** -0.5 * LOG2_E,
                          jnp.where(col >= n_in - 2 * dm, -LOG2_E, 1.0)).astype(F32)
    w_in = (w_in * col_scale).astype(BF16)
    rg_w_a, rg_w_x = ((w * -LOG2_E).astype(BF16) for w in (rg_w_a, rg_w_x))
    norm_g, b_gate, conv_b = (v.reshape(depth, 1, -1) for v in (norm_g, b_gate, conv_b))
    final_g = final_g.reshape(1, dm)

    x2d = x.reshape(t, dm)
    for layer in range(depth):
        nat2d, qkv4, qkv16 = _norm_proj(x2d, norm_g, w_in, layer, b)
        nat = nat2d.reshape(b, s, -1)
        outs, lses = [], []
        for group, qkv in enumerate((nat, qkv4, qkv16)):
            o, lse = _attention_group(qkv, group, ATT_PATTERNS[group][1])
            outs.append(o.reshape(t, GROUP_WIDTH) if group == 0 else o)
            lses.append(lse.reshape(t, LANES) if group == 0 else lse)
        h_f, h_b = _lru(nat, conv_w, conv_b, rg_w_a, rg_b_a, rg_w_x, rg_b_x, rg_lam, layer)
        x2d = _merge(x2d, outs, lses, h_f.reshape(t, dm), h_b.reshape(t, dm), norm_g, w_in, b_gate,
                     w_o_att, w_o_lru, w_out, final_g, layer, final=(layer == depth - 1))
    return x2d.reshape(b, s, dm)
```

```python
import functools

import jax
import jax.numpy as jnp
from jax import lax
from jax.experimental import pallas as pl
from jax.experimental.pallas import tpu as pltpu

F32 = jnp.float32
BF16 = jnp.bfloat16

HEAD_DIM = 128
HEADS_PER_GROUP = 4
ATT_PATTERNS = ((128, 1), (512, 4), (2048, 16))
N_GROUPS = len(ATT_PATTERNS)
N_ATT_HEADS = N_GROUPS * HEADS_PER_GROUP
GROUP_WIDTH = HEADS_PER_GROUP * HEAD_DIM
QKV_WIDTH = 3 * GROUP_WIDTH
NEG_INF = -1e30
LRU_BLOCKS = 4
RG_C = 8.0
NORM_EPS = 1e-6
CONV_LEFT = 2

XLRU_COL = 0
NAT_QKV_COL = 2

LANES = 128
F32_ROWS = 8
BF16_ROWS = 16
HALF_KEYS = 64
Q_SUB = 128
K_SUB = Q_SUB + 2 * HALF_KEYS
LSE_REP = LANES // HEADS_PER_GROUP
SEGMENTS = F32_ROWS

PROJ_ROWS = 1024
PROJ_COLS = 512
WEIGHT_CHUNK_COLS = 256
ATTN_Q_ROWS = {1: 2048, 4: 2048, 16: 512}
ATTN_RESIDUES = {1: 1, 4: 1, 16: 4}
LRU_ROWS = 1024
MERGE_ROWS = 1024
MERGE_SUB_ROWS = 256

LOG2_E = 1.4426950408889634
LN_2 = 0.6931471805599453


def _sigmoid(x):
    return 1.0 / (1.0 + jnp.exp2(x * -LOG2_E))


def _rms_norm(x, g):
    ms = jnp.mean(x * x, axis=-1, keepdims=True)
    return x * lax.rsqrt(ms + NORM_EPS) * g


def _col_block(first_col, cols):
    assert first_col % cols == 0, (first_col, cols)
    return first_col // cols


def _layer(stacked, layer, cols=None, col_block=0):
    block = stacked.shape[1:] if cols is None else stacked.shape[1:-1] + (cols,)
    index = (layer,) + (0,) * (len(block) - 1) + (col_block,)
    return pl.BlockSpec((None,) + block, lambda *_: index, pipeline_mode=pl.Buffered(1))


def _load_weight_columns(w_hbm, layer, chunks, stage, sem):
    cols = stage.shape[-1]

    def copy(c):
        return pltpu.make_async_copy(w_hbm.at[layer, :, pl.ds(chunks[c][0], cols)], stage.at[c % 2], sem.at[c % 2])

    copy(0).start()
    for c, (_, dst, dst_col, scale) in enumerate(chunks):
        if c + 1 < len(chunks):
            copy(c + 1).start()
        copy(c).wait()
        w = stage[c % 2]
        dst[:, dst_col:dst_col + cols] = (w if scale == 1.0 else w * scale).astype(BF16)


DILATIONS = tuple(d for _, d in ATT_PATTERNS)
assert all(w // (2 * d) == HALF_KEYS for w, d in ATT_PATTERNS)


def _norm_proj_body(x_ref, g_ref, w_hbm, nat_ref, *refs, layer):
    dilated_out = refs[:N_GROUPS - 1]
    xn_ref, *dilated_lhs, slab_ref, wqkv_ref, wx_ref, stage, sem = refs[N_GROUPS - 1:]
    tm, dm = x_ref.shape
    width = wx_ref.shape[1]
    gw = GROUP_WIDTH

    @pl.when(pl.program_id(0) == 0)
    def _():
        x_lru_col = N_GROUPS * QKV_WIDTH + gw
        chunks = [(c0, wqkv_ref, c0, HEAD_DIM ** -0.5 * LOG2_E if c0 < N_GROUPS * gw else 1.0)
                  for c0 in range(0, N_GROUPS * QKV_WIDTH, WEIGHT_CHUNK_COLS)]
        chunks += [(x_lru_col + c0, wx_ref, c0, 1.0) for c0 in range(0, width, WEIGHT_CHUNK_COLS)]
        _load_weight_columns(w_hbm, layer, chunks, stage, sem)

    y = _rms_norm(x_ref[...], g_ref[...])
    xn_ref[...] = y.astype(BF16)
    for c in range(dm // LANES):
        slab_ref[c] = y[:, c * LANES:(c + 1) * LANES]

    def project(lhs_ref, w_ref, col, cols):
        return jnp.dot(lhs_ref[...], w_ref[:, col:col + cols], preferred_element_type=F32).astype(BF16)

    for n0 in range(0, width, PROJ_COLS):
        nat_ref[:, n0:n0 + PROJ_COLS] = project(xn_ref, wx_ref, n0, PROJ_COLS)
    for part in range(3):
        nat_ref[:, width + part * gw:width + (part + 1) * gw] = project(xn_ref, wqkv_ref, part * N_GROUPS * gw, gw)

    for group, (lhs_ref, out_ref) in enumerate(zip(dilated_lhs, dilated_out), start=1):
        d = DILATIONS[group]
        rows = tm // d
        for c in range(dm // LANES):
            for r in range(d):
                lhs_ref[r * rows:(r + 1) * rows, c * LANES:(c + 1) * LANES] = (
                    slab_ref[c, pl.ds(r, rows, stride=d), :].astype(BF16))
        for part in range(3):
            out_ref[0, :, :, part * gw:(part + 1) * gw] = (
                project(lhs_ref, wqkv_ref, (part * N_GROUPS + group) * gw, gw).reshape(d, rows, gw))


def _norm_proj(x2d, norm_g, w_in, layer, batch):
    t, dm = x2d.shape
    s = t // batch
    n_nat = dm + QKV_WIDTH
    tm = PROJ_ROWS
    tiles_per_b = s // tm

    def strided_spec(d):
        return pl.BlockSpec((1, d, tm // d, QKV_WIDTH), lambda i: (i // tiles_per_b, 0, i % tiles_per_b, 0))

    return pl.pallas_call(
        functools.partial(_norm_proj_body, layer=layer),
        grid=(t // tm,),
        in_specs=[pl.BlockSpec((tm, dm), lambda i: (i, 0)), _layer(norm_g, layer),
                  pl.BlockSpec(memory_space=pl.ANY)],
        out_specs=[pl.BlockSpec((tm, n_nat), lambda i: (i, 0))] + [strided_spec(d) for d in DILATIONS[1:]],
        out_shape=[jax.ShapeDtypeStruct((t, n_nat), BF16)]
                  + [jax.ShapeDtypeStruct((batch, d, s // d, QKV_WIDTH), BF16) for d in DILATIONS[1:]],
        scratch_shapes=[pltpu.VMEM((tm, dm), BF16)] * N_GROUPS + [pltpu.VMEM((dm // LANES, tm, LANES), F32)]
                       + [pltpu.VMEM((dm, N_GROUPS * QKV_WIDTH), BF16), pltpu.VMEM((dm, dm), BF16),
                          pltpu.VMEM((2, dm, WEIGHT_CHUNK_COLS), F32), pltpu.SemaphoreType.DMA((2,))],
        compiler_params=pltpu.CompilerParams(dimension_semantics=("arbitrary",)),
        name="norm_proj",
    )(x2d, norm_g, w_in)


def _attn_body(q_ref, kp_ref, kc_ref, kn_ref, vp_ref, vc_ref, vn_ref, o_ref, lse_ref,
               kcat, vcat, bias, *, group, dilation, tq, residues):
    qi = pl.program_id(1)
    r = pl.program_id(2)
    n_q = pl.num_programs(1)
    kcol = lax.broadcasted_iota(jnp.int32, (Q_SUB, K_SUB), 1)
    slopes = [2.0 ** (-8.0 * (group * HEADS_PER_GROUP + h + 1) / N_ATT_HEADS) for h in range(HEADS_PER_GROUP)]

    @pl.when((pl.program_id(0) == 0) & (qi == 0) & (r == 0))
    def _():
        qrow = lax.broadcasted_iota(jnp.int32, (Q_SUB, K_SUB), 0)
        absd = jnp.abs(kcol - HALF_KEYS - qrow)
        dist = (absd * dilation).astype(F32)
        for h in range(HEADS_PER_GROUP):
            bias[h] = jnp.where(absd <= HALF_KEYS, (-slopes[h] * LOG2_E) * dist, NEG_INF)
            vcat[:, :, (2 * h + 1) * HEAD_DIM:(2 * h + 2) * HEAD_DIM] = jnp.ones(
                (residues, tq + 2 * HALF_KEYS, HEAD_DIM), BF16)

    lane = lax.broadcasted_iota(jnp.int32, (Q_SUB, LANES), 1)
    for res in range(residues):
        blk = (0,) if dilation == 1 else (0, res)
        _attn_residue(q_ref, kp_ref, kc_ref, kn_ref, vp_ref, vc_ref, vn_ref, o_ref, lse_ref,
                      kcat.at[res], vcat.at[res], bias, blk, qi, n_q, kcol, lane, tq)


def _attn_residue(q_ref, kp_ref, kc_ref, kn_ref, vp_ref, vc_ref, vn_ref, o_ref, lse_ref,
                  kcat, vcat, bias, blk, qi, n_q, kcol, lane, tq):
    kcat[0:HALF_KEYS] = kp_ref[blk]
    kcat[HALF_KEYS:HALF_KEYS + tq] = kc_ref[blk]
    kcat[HALF_KEYS + tq:] = kn_ref[blk]
    for h in range(HEADS_PER_GROUP):
        src = slice(h * HEAD_DIM, (h + 1) * HEAD_DIM)
        dst = slice(2 * h * HEAD_DIM, (2 * h + 1) * HEAD_DIM)
        vcat[0:HALF_KEYS, dst] = vp_ref[blk + (slice(None), src)]
        vcat[HALF_KEYS:HALF_KEYS + tq, dst] = vc_ref[blk + (slice(None), src)]
        vcat[HALF_KEYS + tq:, dst] = vn_ref[blk + (slice(None), src)]

    n_sub_tiles = tq // Q_SUB
    for sub in range(n_sub_tiles):
        r0 = sub * Q_SUB
        lse_tile = jnp.zeros((Q_SUB, LANES), F32)
        for h in range(HEADS_PER_GROUP):
            c0 = h * HEAD_DIM
            q = q_ref[blk + (slice(r0, r0 + Q_SUB), slice(c0, c0 + HEAD_DIM))]
            k = kcat[r0:r0 + K_SUB, c0:c0 + HEAD_DIM]
            raw = lax.dot_general(q, k, (((1,), (1,)), ((), ())), preferred_element_type=F32)
            s2 = raw + bias[h]
            if sub == 0:
                s2 = jnp.where((qi > 0) | (kcol >= HALF_KEYS), s2, NEG_INF)
            if sub == n_sub_tiles - 1:
                s2 = jnp.where((qi < n_q - 1) | (kcol < HALF_KEYS + Q_SUB), s2, NEG_INF)
            m2 = jnp.max(s2, axis=-1, keepdims=True)
            p = jnp.exp2(s2 - m2).astype(BF16)
            acc = jnp.dot(p, vcat[r0:r0 + K_SUB, 2 * c0:2 * c0 + 2 * HEAD_DIM], preferred_element_type=F32)
            den = acc[:, HEAD_DIM:]
            o = acc[:, :HEAD_DIM] / den
            o_ref[blk + (slice(r0, r0 + Q_SUB), slice(c0, c0 + HEAD_DIM))] = o.astype(BF16)
            lse = (m2 + jnp.log2(den)) * LN_2
            lse_tile = jnp.where((lane >= h * LSE_REP) & (lane < (h + 1) * LSE_REP), lse, lse_tile)
        lse_ref[blk + (slice(r0, r0 + Q_SUB), slice(None))] = lse_tile


def _attention_group(qkv, group, dilation):
    if dilation == 1:
        b, s, _ = qkv.shape
    else:
        b, _, n_sub, _ = qkv.shape
        s = n_sub * dilation
    n_sub = s // dilation
    tq = min(ATTN_Q_ROWS[dilation], n_sub)
    residues = ATTN_RESIDUES[dilation]
    q_col = NAT_QKV_COL if dilation == 1 else 0
    halo_per_tile = tq // HALF_KEYS
    last_halo = n_sub // HALF_KEYS - 1

    def spec(rows, row_index, col, width=GROUP_WIDTH):
        if dilation == 1:
            return pl.BlockSpec((1, rows, width), lambda bi, qi, r: (bi, row_index(qi), col))
        return pl.BlockSpec((1, residues, rows, width), lambda bi, qi, r: (bi, r, row_index(qi), col))

    def main_spec(col):
        return spec(tq, lambda qi: qi, col)

    def prev_spec(col):
        return spec(HALF_KEYS, lambda qi: jnp.maximum(qi * halo_per_tile - 1, 0), col)

    def next_spec(col):
        return spec(HALF_KEYS, lambda qi: jnp.minimum((qi + 1) * halo_per_tile, last_halo), col)

    rows_shape = (b, s) if dilation == 1 else (b, dilation, n_sub)
    return pl.pallas_call(
        functools.partial(_attn_body, group=group, dilation=dilation, tq=tq, residues=residues),
        grid=(b, n_sub // tq, dilation // residues),
        in_specs=[main_spec(q_col),
                  prev_spec(q_col + 1), main_spec(q_col + 1), next_spec(q_col + 1),
                  prev_spec(q_col + 2), main_spec(q_col + 2), next_spec(q_col + 2)],
        out_specs=[main_spec(0), spec(tq, lambda qi: qi, 0, LANES)],
        out_shape=[jax.ShapeDtypeStruct(rows_shape + (GROUP_WIDTH,), BF16),
                   jax.ShapeDtypeStruct(rows_shape + (LANES,), F32)],
        scratch_shapes=[pltpu.VMEM((residues, tq + 2 * HALF_KEYS, GROUP_WIDTH), BF16),
                        pltpu.VMEM((residues, tq + 2 * HALF_KEYS, 2 * GROUP_WIDTH), BF16),
                        pltpu.VMEM((HEADS_PER_GROUP, Q_SUB, K_SUB), F32)],
        compiler_params=pltpu.CompilerParams(dimension_semantics=("arbitrary", "arbitrary", "arbitrary")),
        name=f"attn_d{dilation}",
    )(qkv, qkv, qkv, qkv, qkv, qkv, qkv)


def _lru_direction(xp_ref, xc_ref, xn_ref, h_ref, tile, n_tiles, rev, dirn,
                   cw_ref, cb_ref, wa_ref, ba_ref, wx_ref, bx_ref, lam_ref,
                   xpad, a_s, b_s, hpad, carry):
    tt, width = xc_ref.shape[1:]
    bw = width // LRU_BLOCKS
    n_chunks = width // LANES
    seg_len = tt // SEGMENTS
    pitch = seg_len + SEGMENTS
    sub = lax.broadcasted_iota(jnp.int32, (SEGMENTS, width), 0)

    xf = xc_ref[0].astype(F32)
    for c in range(n_chunks):
        for k in range(SEGMENTS):
            xpad[c, k * pitch:k * pitch + seg_len, :] = xf[k * seg_len:(k + 1) * seg_len, c * LANES:(c + 1) * LANES]
    xs = jnp.concatenate(
        [jnp.concatenate([xpad[c, pl.ds(j, SEGMENTS, stride=pitch), :] for j in range(seg_len)], axis=0)
         for c in range(n_chunks)], axis=1)

    halo_prev = jnp.where(tile == 0, 0.0, xp_ref[0, BF16_ROWS - CONV_LEFT:BF16_ROWS, :].astype(F32))
    halo_next = jnp.where(tile == n_tiles - 1, 0.0, xn_ref[0, 0:1, :].astype(F32))
    before1 = jnp.where(sub == 0, halo_prev[1:2], pltpu.roll(xs[tt - SEGMENTS:tt], 1, 0))
    before2 = jnp.where(sub == 0, halo_prev[0:1], pltpu.roll(xs[tt - 2 * SEGMENTS:tt - SEGMENTS], 1, 0))
    after1 = jnp.where(sub == SEGMENTS - 1, halo_next, pltpu.roll(xs[0:SEGMENTS], SEGMENTS - 1, 0))
    xm2 = jnp.concatenate([before2, before1, xs[:tt - 2 * SEGMENTS]], axis=0)
    xm1 = jnp.concatenate([before1, xs[:tt - SEGMENTS]], axis=0)
    xp1 = jnp.concatenate([xs[SEGMENTS:], after1], axis=0)
    xc = cb_ref[...] + xm2 * cw_ref[0:1] + xm1 * cw_ref[1:2] + xs * cw_ref[2:3] + xp1 * cw_ref[3:4]
    xcb = xc.astype(BF16)

    lam = lam_ref[dirn:dirn + 1]
    neg_lam = -lam
    softplus = jnp.maximum(neg_lam, 0.0) + jnp.log1p(jnp.exp(-jnp.abs(neg_lam)))
    coef = -RG_C * softplus

    sub_blk = lax.broadcasted_iota(jnp.int32, (SEGMENTS, bw), 0)
    if rev:
        start_rows = slice(tt - SEGMENTS, tt)
        first = (tile == n_tiles - 1) & (sub_blk == SEGMENTS - 1)
    else:
        start_rows = slice(0, SEGMENTS)
        first = (tile == 0) & (sub_blk == 0)

    for blk in range(LRU_BLOCKS):
        c0 = blk * bw
        xblk = xcb[:, c0:c0 + bw]
        pa = jnp.dot(xblk, wa_ref[dirn, blk], preferred_element_type=F32) + ba_ref[dirn:dirn + 1, c0:c0 + bw] * -LOG2_E
        px = jnp.dot(xblk, wx_ref[dirn, blk], preferred_element_type=F32) + bx_ref[dirn:dirn + 1, c0:c0 + bw] * -LOG2_E
        log_a = coef[:, c0:c0 + bw] / (1.0 + jnp.exp2(pa))
        a = jnp.exp(log_a)
        u = jnp.tanh(log_a) * (-1.0 - a * a)
        mult = jnp.where(u > 0.0, u * lax.rsqrt(u), 0.0)
        gated = xc[:, c0:c0 + bw] / (1.0 + jnp.exp2(px))
        b = mult * gated
        a_s[:, c0:c0 + bw] = a
        b_s[:, c0:c0 + bw] = b
        b_s[start_rows, c0:c0 + bw] = jnp.where(first, gated[start_rows], b[start_rows])

    steps = range(seg_len - 1, -1, -1) if rev else range(seg_len)
    segs = range(SEGMENTS - 1, -1, -1) if rev else range(SEGMENTS)
    sub1 = lax.broadcasted_iota(jnp.int32, (SEGMENTS, LANES), 0)
    for c in range(n_chunks):
        cs = slice(c * LANES, (c + 1) * LANES)
        prod = jnp.ones((SEGMENTS, LANES), F32)
        h = jnp.zeros((SEGMENTS, LANES), F32)
        for j in steps:
            aj = a_s[j * SEGMENTS:(j + 1) * SEGMENTS, cs]
            h = aj * h + b_s[j * SEGMENTS:(j + 1) * SEGMENTS, cs]
            prod = aj * prod
        state = carry[dirn:dirn + 1, cs]
        h0 = jnp.zeros((SEGMENTS, LANES), F32)
        for k in segs:
            h0 = jnp.where(sub1 == k, state, h0)
            state = prod[k:k + 1] * state + h[k:k + 1]
        carry[dirn:dirn + 1, cs] = state
        h = h0
        for j in steps:
            h = a_s[j * SEGMENTS:(j + 1) * SEGMENTS, cs] * h + b_s[j * SEGMENTS:(j + 1) * SEGMENTS, cs]
            hpad[c, pl.ds(j, SEGMENTS, stride=pitch), :] = h
        for k in range(SEGMENTS):
            h_ref[0, k * seg_len:(k + 1) * seg_len, cs] = hpad[c, k * pitch:k * pitch + seg_len, :].astype(h_ref.dtype)


def _lru_body(xfp_ref, xfc_ref, xfn_ref, xbp_ref, xbc_ref, xbn_ref,
              cw_ref, cb_ref, wa_ref, ba_ref, wx_ref, bx_ref, lam_ref,
              hf_ref, hb_ref, xpad_f, a_f, b_f, hpad_f, xpad_b, a_b, b_b, hpad_b, carry):
    i = pl.program_id(1)
    n_tiles = pl.num_programs(1)

    @pl.when(i == 0)
    def _():
        carry[...] = jnp.zeros_like(carry)

    params = (cw_ref, cb_ref, wa_ref, ba_ref, wx_ref, bx_ref, lam_ref)
    _lru_direction(xfp_ref, xfc_ref, xfn_ref, hf_ref, i, n_tiles, False, 0, *params,
                   xpad_f, a_f, b_f, hpad_f, carry)
    _lru_direction(xbp_ref, xbc_ref, xbn_ref, hb_ref, n_tiles - 1 - i, n_tiles, True, 1, *params,
                   xpad_b, a_b, b_b, hpad_b, carry)


def _lru(nat, conv_w, conv_b, w_a, b_a, w_x, b_x, lam, layer):
    b, s, _ = nat.shape
    width = conv_w.shape[-1]
    tt = LRU_ROWS
    n_tiles = s // tt
    halo_per_tile = tt // BF16_ROWS
    last_halo = s // BF16_ROWS - 1

    def tile_of(i, rev):
        return n_tiles - 1 - i if rev else i

    def specs(rev):
        return [
            pl.BlockSpec((1, BF16_ROWS, width),
                         lambda bi, i: (bi, jnp.maximum(tile_of(i, rev) * halo_per_tile - 1, 0), XLRU_COL)),
            pl.BlockSpec((1, tt, width), lambda bi, i: (bi, tile_of(i, rev), XLRU_COL)),
            pl.BlockSpec((1, BF16_ROWS, width),
                         lambda bi, i: (bi, jnp.minimum((tile_of(i, rev) + 1) * halo_per_tile, last_halo), XLRU_COL)),
        ]

    consts = (conv_w, conv_b, w_a, b_a, w_x, b_x, lam)
    padded_rows = tt + SEGMENTS * SEGMENTS
    per_direction = [pltpu.VMEM((width // LANES, padded_rows, LANES), F32),
                     pltpu.VMEM((tt, width), F32),
                     pltpu.VMEM((tt, width), F32),
                     pltpu.VMEM((width // LANES, padded_rows, LANES), F32)]
    return pl.pallas_call(
        _lru_body,
        grid=(b, n_tiles),
        in_specs=specs(False) + specs(True) + [_layer(a, layer) for a in consts],
        out_specs=[pl.BlockSpec((1, tt, width), lambda bi, i: (bi, i, 0)),
                   pl.BlockSpec((1, tt, width), lambda bi, i: (bi, n_tiles - 1 - i, 0))],
        out_shape=[jax.ShapeDtypeStruct((b, s, width), BF16)] * 2,
        scratch_shapes=per_direction + per_direction + [pltpu.VMEM((F32_ROWS, width), F32)],
        compiler_params=pltpu.CompilerParams(dimension_semantics=("parallel", "arbitrary")),
        name="lru",
    )(nat, nat, nat, nat, nat, nat, *consts)


def _token_order_rows(ref, r0, n_rows, slab):
    d, width = ref.shape[1], ref.shape[3]
    per = n_rows // d
    for r in range(d):
        piece = ref[0, r, r0 // d:r0 // d + per, :].astype(F32)
        for c in range(width // LANES):
            slab[c, pl.ds(r, per, stride=d), :] = piece[:, c * LANES:(c + 1) * LANES]
    return [slab[c] for c in range(width // LANES)]


def _merge_body(o0_ref, o1_ref, o2_ref, l0_ref, l1_ref, l2_ref, hf_ref, hb_ref, x_ref,
                ng_ref, w_hbm, bg_ref, woa_ref, wol_ref, wout_ref, fg_ref, out_ref,
                o_slabs, l_slabs, wga_ref, wgo_ref, stage, sem, *, layer, final):
    tm, dm = x_ref.shape

    @pl.when(pl.program_id(0) == 0)
    def _():
        att_gate_col = N_GROUPS * QKV_WIDTH
        others_col = att_gate_col + GROUP_WIDTH + dm
        chunks = [(att_gate_col + c0, wga_ref, c0, 1.0) for c0 in range(0, GROUP_WIDTH, WEIGHT_CHUNK_COLS)]
        chunks += [(others_col + c0, wgo_ref, c0, 1.0 if c0 < dm else -LOG2_E)
                   for c0 in range(0, 3 * dm, WEIGHT_CHUNK_COLS)]
        _load_weight_columns(w_hbm, layer, chunks, stage, sem)
    for blk, r0 in enumerate(range(0, tm, MERGE_SUB_ROWS)):
        rows = slice(r0, r0 + MERGE_SUB_ROWS)
        xn = _rms_norm(x_ref[rows], ng_ref[...]).astype(BF16)

        gatt = jnp.dot(xn, wga_ref[...], preferred_element_type=F32)
        glru, logit_att, logit_lru = (
            jnp.dot(xn, wgo_ref[:, c0:c0 + dm], preferred_element_type=F32) for c0 in (0, dm, 2 * dm))

        o_dilated = [_token_order_rows(ref, r0, MERGE_SUB_ROWS, o_slabs.at[blk, g])
                     for g, ref in enumerate((o1_ref, o2_ref))]
        l1, l2 = (_token_order_rows(ref, r0, MERGE_SUB_ROWS, l_slabs.at[blk, g])[0]
                  for g, ref in enumerate((l1_ref, l2_ref)))
        l0 = l0_ref[rows]
        mx = jnp.maximum(jnp.maximum(l0, l1), l2)
        e0, e1, e2 = jnp.exp(l0 - mx), jnp.exp(l1 - mx), jnp.exp(l2 - mx)
        inv = 1.0 / (e0 + e1 + e2)
        alphas = (e0 * inv, e1 * inv, e2 * inv)

        heads = []
        for h in range(HEADS_PER_GROUP):
            group_outs = (o0_ref[rows, h * HEAD_DIM:(h + 1) * HEAD_DIM].astype(F32), o_dilated[0][h], o_dilated[1][h])
            acc = jnp.zeros((MERGE_SUB_ROWS, HEAD_DIM), F32)
            for g in range(N_GROUPS):
                w = jnp.broadcast_to(alphas[g][:, h * LSE_REP:h * LSE_REP + 1], (MERGE_SUB_ROWS, HEAD_DIM))
                acc = acc + w * group_outs[g]
            heads.append(acc)
        mixed = jnp.concatenate(heads, axis=-1)

        y_att = (mixed * (gatt * _sigmoid(gatt))).astype(BF16)
        h_sum = hf_ref[rows].astype(F32) + hb_ref[rows].astype(F32)
        y_lru = (h_sum * (glru * _sigmoid(glru))).astype(BF16)

        p_att = jnp.dot(y_att, woa_ref[...], preferred_element_type=F32)
        p_lru = jnp.dot(y_lru, wol_ref[...], preferred_element_type=F32)
        gate_att = 1.0 / (1.0 + jnp.exp2(logit_att + bg_ref[:, 0:dm] * -LOG2_E))
        gate_lru = 1.0 / (1.0 + jnp.exp2(logit_lru + bg_ref[:, dm:2 * dm] * -LOG2_E))
        merged = (gate_att * p_att + gate_lru * p_lru).astype(BF16)
        y = x_ref[rows] + jnp.dot(merged, wout_ref[...], preferred_element_type=F32)
        if final:
            y = _rms_norm(y, fg_ref[...])
        out_ref[rows] = y


def _merge(x2d, outs, lses, h_f, h_b, norm_g, w_in, b_gate, w_o_att, w_o_lru, w_out, final_g, layer, final):
    t, dm = x2d.shape
    tm = MERGE_ROWS
    tiles_per_b = t // outs[1].shape[0] // tm

    def rows(width):
        return pl.BlockSpec((tm, width), lambda i: (i, 0))

    def dilated_rows(a):
        d, width = a.shape[1], a.shape[3]
        return pl.BlockSpec((1, d, tm // d, width), lambda i: (i // tiles_per_b, 0, i % tiles_per_b, 0))

    n_blocks = tm // MERGE_SUB_ROWS
    return pl.pallas_call(
        functools.partial(_merge_body, layer=layer, final=final),
        grid=(t // tm,),
        in_specs=[rows(GROUP_WIDTH), dilated_rows(outs[1]), dilated_rows(outs[2]),
                  rows(LANES), dilated_rows(lses[1]), dilated_rows(lses[2])] + [rows(dm)] * 3
                 + [_layer(norm_g, layer), pl.BlockSpec(memory_space=pl.ANY), _layer(b_gate, layer),
                    _layer(w_o_att, layer), _layer(w_o_lru, layer), _layer(w_out, layer),
                    pl.BlockSpec(final_g.shape, lambda i: (0, 0), pipeline_mode=pl.Buffered(1))],
        out_specs=rows(dm),
        out_shape=jax.ShapeDtypeStruct((t, dm), F32),
        scratch_shapes=[pltpu.VMEM((n_blocks, N_GROUPS - 1, GROUP_WIDTH // LANES, MERGE_SUB_ROWS, LANES), F32),
                        pltpu.VMEM((n_blocks, N_GROUPS - 1, 1, MERGE_SUB_ROWS, LANES), F32),
                        pltpu.VMEM((dm, GROUP_WIDTH), BF16), pltpu.VMEM((dm, 3 * dm), BF16),
                        pltpu.VMEM((2, dm, WEIGHT_CHUNK_COLS), F32), pltpu.SemaphoreType.DMA((2,))],
        compiler_params=pltpu.CompilerParams(dimension_semantics=("arbitrary",)),
        name="merge_final" if final else "merge",
    )(*outs, *lses, h_f, h_b, x2d, norm_g, w_in, b_gate, w_o_att, w_o_lru, w_out, final_g)


def kernel(x, norm_g, w_in, b_gate, conv_w, conv_b, rg_w_a, rg_b_a, rg_w_x, rg_b_x, rg_lam,
           w_o_att, w_o_lru, w_out, final_g):
    b, s, dm = x.shape
    depth = w_in.shape[0]
    t = b * s
    w_o_att, w_o_lru, w_out = (w.astype(BF16) for w in (w_o_att, w_o_lru, w_out))
    rg_w_a, rg_w_x = ((w * -LOG2_E).astype(BF16) for w in (rg_w_a, rg_w_x))
    norm_g, b_gate, conv_b = (v.reshape(depth, 1, -1) for v in (norm_g, b_gate, conv_b))
    final_g = final_g.reshape(1, dm)

    x2d = x.reshape(t, dm)
    for layer in range(depth):
        nat2d, qkv4, qkv16 = _norm_proj(x2d, norm_g, w_in, layer, b)
        nat = nat2d.reshape(b, s, -1)
        outs, lses = [], []
        for group, qkv in enumerate((nat, qkv4, qkv16)):
            o, lse = _attention_group(qkv, group, ATT_PATTERNS[group][1])
            outs.append(o.reshape(t, GROUP_WIDTH) if group == 0 else o)
            lses.append(lse.reshape(t, LANES) if group == 0 else lse)
        h_f, h_b = _lru(nat, conv_w, conv_b, rg_w_a, rg_b_a, rg_w_x, rg_b_x, rg_lam, layer)
        x2d = _merge(x2d, outs, lses, h_f.reshape(t, dm), h_b.reshape(t, dm), norm_g, w_in, b_gate,
                     w_o_att, w_o_lru, w_out, final_g, layer, final=(layer == depth - 1))
    return x2d.reshape(b, s, dm)
```

```python
import functools

import jax
import jax.numpy as jnp
from jax import lax
from jax.experimental import pallas as pl
from jax.experimental.pallas import tpu as pltpu

F32 = jnp.float32
BF16 = jnp.bfloat16

HEAD_DIM = 128
HEADS_PER_GROUP = 4
ATT_PATTERNS = ((128, 1), (512, 4), (2048, 16))
N_GROUPS = len(ATT_PATTERNS)
N_ATT_HEADS = N_GROUPS * HEADS_PER_GROUP
GROUP_WIDTH = HEADS_PER_GROUP * HEAD_DIM
QKV_WIDTH = 3 * GROUP_WIDTH
NEG_INF = -1e30
LRU_BLOCKS = 4
RG_C = 8.0
NORM_EPS = 1e-6
CONV_LEFT = 2

XLRU_COL = 0
NAT_QKV_COL = 2

LANES = 128
F32_ROWS = 8
BF16_ROWS = 16
HALF_KEYS = 64
Q_SUB = 128
K_SUB = Q_SUB + 2 * HALF_KEYS
LSE_REP = LANES // HEADS_PER_GROUP
SEGMENTS = F32_ROWS

PROJ_ROWS = 1024
PROJ_COLS = 512
ATTN_Q_ROWS = {1: 2048, 4: 2048, 16: 512}
ATTN_RESIDUES = {1: 1, 4: 1, 16: 4}
LRU_ROWS = 1024
MERGE_ROWS = 1024
MERGE_SUB_ROWS = 256

LOG2_E = 1.4426950408889634
LN_2 = 0.6931471805599453


def _sigmoid(x):
    return 1.0 / (1.0 + jnp.exp2(x * -LOG2_E))


def _rms_norm(x, g):
    ms = jnp.mean(x * x, axis=-1, keepdims=True)
    return x * lax.rsqrt(ms + NORM_EPS) * g


def _col_block(first_col, cols):
    assert first_col % cols == 0, (first_col, cols)
    return first_col // cols


def _layer(stacked, layer, cols=None, col_block=0):
    block = stacked.shape[1:] if cols is None else stacked.shape[1:-1] + (cols,)
    index = (layer,) + (0,) * (len(block) - 1) + (col_block,)
    return pl.BlockSpec((None,) + block, lambda *_: index, pipeline_mode=pl.Buffered(1))


DILATIONS = tuple(d for _, d in ATT_PATTERNS)
assert all(w // (2 * d) == HALF_KEYS for w, d in ATT_PATTERNS)


def _norm_proj_body(x_ref, g_ref, wqkv_ref, wx_ref, nat_ref, *refs):
    dilated_out = refs[:N_GROUPS - 1]
    xn_ref, *dilated_lhs, slab_ref = refs[N_GROUPS - 1:]
    tm, dm = x_ref.shape
    width = wx_ref.shape[1]
    gw = GROUP_WIDTH

    y = _rms_norm(x_ref[...], g_ref[...])
    xn_ref[...] = y.astype(BF16)
    for c in range(dm // LANES):
        slab_ref[c] = y[:, c * LANES:(c + 1) * LANES]

    def project(lhs_ref, w_ref, col, cols):
        return jnp.dot(lhs_ref[...], w_ref[:, col:col + cols], preferred_element_type=F32).astype(BF16)

    for n0 in range(0, width, PROJ_COLS):
        nat_ref[:, n0:n0 + PROJ_COLS] = project(xn_ref, wx_ref, n0, PROJ_COLS)
    for part in range(3):
        nat_ref[:, width + part * gw:width + (part + 1) * gw] = project(xn_ref, wqkv_ref, part * N_GROUPS * gw, gw)

    for group, (lhs_ref, out_ref) in enumerate(zip(dilated_lhs, dilated_out), start=1):
        d = DILATIONS[group]
        rows = tm // d
        for c in range(dm // LANES):
            for r in range(d):
                lhs_ref[r * rows:(r + 1) * rows, c * LANES:(c + 1) * LANES] = (
                    slab_ref[c, pl.ds(r, rows, stride=d), :].astype(BF16))
        for part in range(3):
            out_ref[0, :, :, part * gw:(part + 1) * gw] = (
                project(lhs_ref, wqkv_ref, (part * N_GROUPS + group) * gw, gw).reshape(d, rows, gw))


def _norm_proj(x2d, norm_g, w_in, layer, batch):
    t, dm = x2d.shape
    s = t // batch
    n_nat = dm + QKV_WIDTH
    tm = PROJ_ROWS
    tiles_per_b = s // tm
    x_lru_block = _col_block(N_GROUPS * QKV_WIDTH + GROUP_WIDTH, dm)

    def strided_spec(d):
        return pl.BlockSpec((1, d, tm // d, QKV_WIDTH), lambda i: (i // tiles_per_b, 0, i % tiles_per_b, 0))

    return pl.pallas_call(
        _norm_proj_body,
        grid=(t // tm,),
        in_specs=[pl.BlockSpec((tm, dm), lambda i: (i, 0)), _layer(norm_g, layer),
                  _layer(w_in, layer, cols=N_GROUPS * QKV_WIDTH), _layer(w_in, layer, cols=dm, col_block=x_lru_block)],
        out_specs=[pl.BlockSpec((tm, n_nat), lambda i: (i, 0))] + [strided_spec(d) for d in DILATIONS[1:]],
        out_shape=[jax.ShapeDtypeStruct((t, n_nat), BF16)]
                  + [jax.ShapeDtypeStruct((batch, d, s // d, QKV_WIDTH), BF16) for d in DILATIONS[1:]],
        scratch_shapes=[pltpu.VMEM((tm, dm), BF16)] * N_GROUPS + [pltpu.VMEM((dm // LANES, tm, LANES), F32)],
        compiler_params=pltpu.CompilerParams(dimension_semantics=("parallel",)),
        name="norm_proj",
    )(x2d, norm_g, w_in, w_in)


def _attn_body(q_ref, kp_ref, kc_ref, kn_ref, vp_ref, vc_ref, vn_ref, o_ref, lse_ref,
               kcat, vcat, bias, *, group, dilation, tq, residues):
    qi = pl.program_id(1)
    r = pl.program_id(2)
    n_q = pl.num_programs(1)
    kcol = lax.broadcasted_iota(jnp.int32, (Q_SUB, K_SUB), 1)
    slopes = [2.0 ** (-8.0 * (group * HEADS_PER_GROUP + h + 1) / N_ATT_HEADS) for h in range(HEADS_PER_GROUP)]

    @pl.when((pl.program_id(0) == 0) & (qi == 0) & (r == 0))
    def _():
        qrow = lax.broadcasted_iota(jnp.int32, (Q_SUB, K_SUB), 0)
        absd = jnp.abs(kcol - HALF_KEYS - qrow)
        dist = (absd * dilation).astype(F32)
        for h in range(HEADS_PER_GROUP):
            bias[h] = jnp.where(absd <= HALF_KEYS, (-slopes[h] * LOG2_E) * dist, NEG_INF)
            vcat[:, :, (2 * h + 1) * HEAD_DIM:(2 * h + 2) * HEAD_DIM] = jnp.ones(
                (residues, tq + 2 * HALF_KEYS, HEAD_DIM), BF16)

    lane = lax.broadcasted_iota(jnp.int32, (Q_SUB, LANES), 1)
    for res in range(residues):
        blk = (0,) if dilation == 1 else (0, res)
        _attn_residue(q_ref, kp_ref, kc_ref, kn_ref, vp_ref, vc_ref, vn_ref, o_ref, lse_ref,
                      kcat.at[res], vcat.at[res], bias, blk, qi, n_q, kcol, lane, tq)


def _attn_residue(q_ref, kp_ref, kc_ref, kn_ref, vp_ref, vc_ref, vn_ref, o_ref, lse_ref,
                  kcat, vcat, bias, blk, qi, n_q, kcol, lane, tq):
    kcat[0:HALF_KEYS] = kp_ref[blk]
    kcat[HALF_KEYS:HALF_KEYS + tq] = kc_ref[blk]
    kcat[HALF_KEYS + tq:] = kn_ref[blk]
    for h in range(HEADS_PER_GROUP):
        src = slice(h * HEAD_DIM, (h + 1) * HEAD_DIM)
        dst = slice(2 * h * HEAD_DIM, (2 * h + 1) * HEAD_DIM)
        vcat[0:HALF_KEYS, dst] = vp_ref[blk + (slice(None), src)]
        vcat[HALF_KEYS:HALF_KEYS + tq, dst] = vc_ref[blk + (slice(None), src)]
        vcat[HALF_KEYS + tq:, dst] = vn_ref[blk + (slice(None), src)]

    n_sub_tiles = tq // Q_SUB
    for sub in range(n_sub_tiles):
        r0 = sub * Q_SUB
        m_tile = jnp.zeros((Q_SUB, LANES), F32)
        den_tile = jnp.ones((Q_SUB, LANES), F32)
        for h in range(HEADS_PER_GROUP):
            c0 = h * HEAD_DIM
            q = q_ref[blk + (slice(r0, r0 + Q_SUB), slice(c0, c0 + HEAD_DIM))]
            k = kcat[r0:r0 + K_SUB, c0:c0 + HEAD_DIM]
            raw = lax.dot_general(q, k, (((1,), (1,)), ((), ())), preferred_element_type=F32)
            s2 = raw + bias[h]
            if sub == 0:
                s2 = jnp.where((qi > 0) | (kcol >= HALF_KEYS), s2, NEG_INF)
            if sub == n_sub_tiles - 1:
                s2 = jnp.where((qi < n_q - 1) | (kcol < HALF_KEYS + Q_SUB), s2, NEG_INF)
            m2 = jnp.max(s2, axis=-1, keepdims=True)
            p = jnp.exp2(s2 - m2).astype(BF16)
            acc = jnp.dot(p, vcat[r0:r0 + K_SUB, 2 * c0:2 * c0 + 2 * HEAD_DIM], preferred_element_type=F32)
            den = acc[:, HEAD_DIM:]
            o = acc[:, :HEAD_DIM] / den
            o_ref[blk + (slice(r0, r0 + Q_SUB), slice(c0, c0 + HEAD_DIM))] = o.astype(BF16)
            head_lanes = (lane >= h * LSE_REP) & (lane < (h + 1) * LSE_REP)
            m_tile = jnp.where(head_lanes, m2, m_tile)
            den_tile = jnp.where(head_lanes, den[:, :LANES], den_tile)
        lse_ref[blk + (slice(r0, r0 + Q_SUB), slice(None))] = (m_tile + jnp.log2(den_tile)) * LN_2


def _attention_group(qkv, group, dilation):
    if dilation == 1:
        b, s, _ = qkv.shape
    else:
        b, _, n_sub, _ = qkv.shape
        s = n_sub * dilation
    n_sub = s // dilation
    tq = min(ATTN_Q_ROWS[dilation], n_sub)
    residues = ATTN_RESIDUES[dilation]
    q_col = NAT_QKV_COL if dilation == 1 else 0
    halo_per_tile = tq // HALF_KEYS
    last_halo = n_sub // HALF_KEYS - 1

    def spec(rows, row_index, col, width=GROUP_WIDTH):
        if dilation == 1:
            return pl.BlockSpec((1, rows, width), lambda bi, qi, r: (bi, row_index(qi), col))
        return pl.BlockSpec((1, residues, rows, width), lambda bi, qi, r: (bi, r, row_index(qi), col))

    def main_spec(col):
        return spec(tq, lambda qi: qi, col)

    def prev_spec(col):
        return spec(HALF_KEYS, lambda qi: jnp.maximum(qi * halo_per_tile - 1, 0), col)

    def next_spec(col):
        return spec(HALF_KEYS, lambda qi: jnp.minimum((qi + 1) * halo_per_tile, last_halo), col)

    rows_shape = (b, s) if dilation == 1 else (b, dilation, n_sub)
    return pl.pallas_call(
        functools.partial(_attn_body, group=group, dilation=dilation, tq=tq, residues=residues),
        grid=(b, n_sub // tq, dilation // residues),
        in_specs=[main_spec(q_col),
                  prev_spec(q_col + 1), main_spec(q_col + 1), next_spec(q_col + 1),
                  prev_spec(q_col + 2), main_spec(q_col + 2), next_spec(q_col + 2)],
        out_specs=[main_spec(0), spec(tq, lambda qi: qi, 0, LANES)],
        out_shape=[jax.ShapeDtypeStruct(rows_shape + (GROUP_WIDTH,), BF16),
                   jax.ShapeDtypeStruct(rows_shape + (LANES,), F32)],
        scratch_shapes=[pltpu.VMEM((residues, tq + 2 * HALF_KEYS, GROUP_WIDTH), BF16),
                        pltpu.VMEM((residues, tq + 2 * HALF_KEYS, 2 * GROUP_WIDTH), BF16),
                        pltpu.VMEM((HEADS_PER_GROUP, Q_SUB, K_SUB), F32)],
        compiler_params=pltpu.CompilerParams(dimension_semantics=("arbitrary", "arbitrary", "arbitrary")),
        name=f"attn_d{dilation}",
    )(qkv, qkv, qkv, qkv, qkv, qkv, qkv)


def _lru_direction(xp_ref, xc_ref, xn_ref, h_ref, tile, n_tiles, rev, dirn,
                   cw_ref, cb_ref, wa_ref, ba_ref, wx_ref, bx_ref, lam_ref,
                   xpad, a_s, b_s, hpad, carry):
    tt, width = xc_ref.shape[1:]
    bw = width // LRU_BLOCKS
    n_chunks = width // LANES
    seg_len = tt // SEGMENTS
    pitch = seg_len + SEGMENTS
    sub = lax.broadcasted_iota(jnp.int32, (SEGMENTS, width), 0)

    xf = xc_ref[0].astype(F32)
    for c in range(n_chunks):
        for k in range(SEGMENTS):
            xpad[c, k * pitch:k * pitch + seg_len, :] = xf[k * seg_len:(k + 1) * seg_len, c * LANES:(c + 1) * LANES]
    xs = jnp.concatenate(
        [jnp.concatenate([xpad[c, pl.ds(j, SEGMENTS, stride=pitch), :] for j in range(seg_len)], axis=0)
         for c in range(n_chunks)], axis=1)

    halo_prev = jnp.where(tile == 0, 0.0, xp_ref[0, BF16_ROWS - CONV_LEFT:BF16_ROWS, :].astype(F32))
    halo_next = jnp.where(tile == n_tiles - 1, 0.0, xn_ref[0, 0:1, :].astype(F32))
    before1 = jnp.where(sub == 0, halo_prev[1:2], pltpu.roll(xs[tt - SEGMENTS:tt], 1, 0))
    before2 = jnp.where(sub == 0, halo_prev[0:1], pltpu.roll(xs[tt - 2 * SEGMENTS:tt - SEGMENTS], 1, 0))
    after1 = jnp.where(sub == SEGMENTS - 1, halo_next, pltpu.roll(xs[0:SEGMENTS], SEGMENTS - 1, 0))
    xm2 = jnp.concatenate([before2, before1, xs[:tt - 2 * SEGMENTS]], axis=0)
    xm1 = jnp.concatenate([before1, xs[:tt - SEGMENTS]], axis=0)
    xp1 = jnp.concatenate([xs[SEGMENTS:], after1], axis=0)
    xc = cb_ref[...] + xm2 * cw_ref[0:1] + xm1 * cw_ref[1:2] + xs * cw_ref[2:3] + xp1 * cw_ref[3:4]
    xcb = xc.astype(BF16)

    lam = lam_ref[dirn:dirn + 1]
    neg_lam = -lam
    softplus = jnp.maximum(neg_lam, 0.0) + jnp.log1p(jnp.exp(-jnp.abs(neg_lam)))
    coef = -RG_C * softplus

    sub_blk = lax.broadcasted_iota(jnp.int32, (SEGMENTS, bw), 0)
    if rev:
        start_rows = slice(tt - SEGMENTS, tt)
        first = (tile == n_tiles - 1) & (sub_blk == SEGMENTS - 1)
    else:
        start_rows = slice(0, SEGMENTS)
        first = (tile == 0) & (sub_blk == 0)

    for blk in range(LRU_BLOCKS):
        c0 = blk * bw
        xblk = xcb[:, c0:c0 + bw]
        pa = jnp.dot(xblk, wa_ref[dirn, blk], preferred_element_type=F32) + ba_ref[dirn:dirn + 1, c0:c0 + bw] * -LOG2_E
        px = jnp.dot(xblk, wx_ref[dirn, blk], preferred_element_type=F32) + bx_ref[dirn:dirn + 1, c0:c0 + bw] * -LOG2_E
        log_a = coef[:, c0:c0 + bw] / (1.0 + jnp.exp2(pa))
        a = jnp.exp(log_a)
        u = jnp.tanh(log_a) * (-1.0 - a * a)
        mult = jnp.where(u > 0.0, u * lax.rsqrt(u), 0.0)
        gated = xc[:, c0:c0 + bw] / (1.0 + jnp.exp2(px))
        b = mult * gated
        a_s[:, c0:c0 + bw] = a
        b_s[:, c0:c0 + bw] = b
        b_s[start_rows, c0:c0 + bw] = jnp.where(first, gated[start_rows], b[start_rows])

    steps = range(seg_len - 1, -1, -1) if rev else range(seg_len)
    segs = range(SEGMENTS - 1, -1, -1) if rev else range(SEGMENTS)
    sub1 = lax.broadcasted_iota(jnp.int32, (SEGMENTS, LANES), 0)
    for c in range(n_chunks):
        cs = slice(c * LANES, (c + 1) * LANES)
        prod = jnp.ones((SEGMENTS, LANES), F32)
        h = jnp.zeros((SEGMENTS, LANES), F32)
        for j in steps:
            aj = a_s[j * SEGMENTS:(j + 1) * SEGMENTS, cs]
            h = aj * h + b_s[j * SEGMENTS:(j + 1) * SEGMENTS, cs]
            prod = aj * prod
        state = carry[dirn:dirn + 1, cs]
        h0 = jnp.zeros((SEGMENTS, LANES), F32)
        for k in segs:
            h0 = jnp.where(sub1 == k, state, h0)
            state = prod[k:k + 1] * state + h[k:k + 1]
        carry[dirn:dirn + 1, cs] = state
        h = h0
        for j in steps:
            h = a_s[j * SEGMENTS:(j + 1) * SEGMENTS, cs] * h + b_s[j * SEGMENTS:(j + 1) * SEGMENTS, cs]
            hpad[c, pl.ds(j, SEGMENTS, stride=pitch), :] = h
        for k in range(SEGMENTS):
            h_ref[0, k * seg_len:(k + 1) * seg_len, cs] = hpad[c, k * pitch:k * pitch + seg_len, :].astype(h_ref.dtype)


def _lru_body(xfp_ref, xfc_ref, xfn_ref, xbp_ref, xbc_ref, xbn_ref,
              cw_ref, cb_ref, wa_ref, ba_ref, wx_ref, bx_ref, lam_ref,
              hf_ref, hb_ref, xpad_f, a_f, b_f, hpad_f, xpad_b, a_b, b_b, hpad_b, carry):
    i = pl.program_id(1)
    n_tiles = pl.num_programs(1)

    @pl.when(i == 0)
    def _():
        carry[...] = jnp.zeros_like(carry)

    params = (cw_ref, cb_ref, wa_ref, ba_ref, wx_ref, bx_ref, lam_ref)
    _lru_direction(xfp_ref, xfc_ref, xfn_ref, hf_ref, i, n_tiles, False, 0, *params,
                   xpad_f, a_f, b_f, hpad_f, carry)
    _lru_direction(xbp_ref, xbc_ref, xbn_ref, hb_ref, n_tiles - 1 - i, n_tiles, True, 1, *params,
                   xpad_b, a_b, b_b, hpad_b, carry)


def _lru(nat, conv_w, conv_b, w_a, b_a, w_x, b_x, lam, layer):
    b, s, _ = nat.shape
    width = conv_w.shape[-1]
    tt = LRU_ROWS
    n_tiles = s // tt
    halo_per_tile = tt // BF16_ROWS
    last_halo = s // BF16_ROWS - 1

    def tile_of(i, rev):
        return n_tiles - 1 - i if rev else i

    def specs(rev):
        return [
            pl.BlockSpec((1, BF16_ROWS, width),
                         lambda bi, i: (bi, jnp.maximum(tile_of(i, rev) * halo_per_tile - 1, 0), XLRU_COL)),
            pl.BlockSpec((1, tt, width), lambda bi, i: (bi, tile_of(i, rev), XLRU_COL)),
            pl.BlockSpec((1, BF16_ROWS, width),
                         lambda bi, i: (bi, jnp.minimum((tile_of(i, rev) + 1) * halo_per_tile, last_halo), XLRU_COL)),
        ]

    consts = (conv_w, conv_b, w_a, b_a, w_x, b_x, lam)
    padded_rows = tt + SEGMENTS * SEGMENTS
    per_direction = [pltpu.VMEM((width // LANES, padded_rows, LANES), F32),
                     pltpu.VMEM((tt, width), F32),
                     pltpu.VMEM((tt, width), F32),
                     pltpu.VMEM((width // LANES, padded_rows, LANES), F32)]
    return pl.pallas_call(
        _lru_body,
        grid=(b, n_tiles),
        in_specs=specs(False) + specs(True) + [_layer(a, layer) for a in consts],
        out_specs=[pl.BlockSpec((1, tt, width), lambda bi, i: (bi, i, 0)),
                   pl.BlockSpec((1, tt, width), lambda bi, i: (bi, n_tiles - 1 - i, 0))],
        out_shape=[jax.ShapeDtypeStruct((b, s, width), BF16)] * 2,
        scratch_shapes=per_direction + per_direction + [pltpu.VMEM((F32_ROWS, width), F32)],
        compiler_params=pltpu.CompilerParams(dimension_semantics=("parallel", "arbitrary")),
        name="lru",
    )(nat, nat, nat, nat, nat, nat, *consts)


def _token_order_rows(ref, r0, n_rows, slab):
    d, width = ref.shape[1], ref.shape[3]
    per = n_rows // d
    for r in range(d):
        piece = ref[0, r, r0 // d:r0 // d + per, :].astype(F32)
        for c in range(width // LANES):
            slab[c, pl.ds(r, per, stride=d), :] = piece[:, c * LANES:(c + 1) * LANES]
    return [slab[c] for c in range(width // LANES)]


def _merge_body(o0_ref, o1_ref, o2_ref, l0_ref, l1_ref, l2_ref, hf_ref, hb_ref, x_ref,
                ng_ref, wga_ref, wgo_ref, bg_ref, woa_ref, wol_ref, wout_ref, fg_ref, out_ref,
                o_slabs, l_slabs, *, final):
    tm, dm = x_ref.shape
    for blk, r0 in enumerate(range(0, tm, MERGE_SUB_ROWS)):
        rows = slice(r0, r0 + MERGE_SUB_ROWS)
        xn = _rms_norm(x_ref[rows], ng_ref[...]).astype(BF16)

        gatt = jnp.dot(xn, wga_ref[...], preferred_element_type=F32)
        glru, logit_att, logit_lru = (
            jnp.dot(xn, wgo_ref[:, c0:c0 + dm], preferred_element_type=F32) for c0 in (0, dm, 2 * dm))

        o_dilated = [_token_order_rows(ref, r0, MERGE_SUB_ROWS, o_slabs.at[blk, g])
                     for g, ref in enumerate((o1_ref, o2_ref))]
        l1, l2 = (_token_order_rows(ref, r0, MERGE_SUB_ROWS, l_slabs.at[blk, g])[0]
                  for g, ref in enumerate((l1_ref, l2_ref)))
        l0 = l0_ref[rows]
        mx = jnp.maximum(jnp.maximum(l0, l1), l2)
        e0, e1, e2 = jnp.exp(l0 - mx), jnp.exp(l1 - mx), jnp.exp(l2 - mx)
        inv = 1.0 / (e0 + e1 + e2)
        alphas = (e0 * inv, e1 * inv, e2 * inv)

        heads = []
        for h in range(HEADS_PER_GROUP):
            group_outs = (o0_ref[rows, h * HEAD_DIM:(h + 1) * HEAD_DIM].astype(F32), o_dilated[0][h], o_dilated[1][h])
            acc = jnp.zeros((MERGE_SUB_ROWS, HEAD_DIM), F32)
            for g in range(N_GROUPS):
                w = jnp.broadcast_to(alphas[g][:, h * LSE_REP:h * LSE_REP + 1], (MERGE_SUB_ROWS, HEAD_DIM))
                acc = acc + w * group_outs[g]
            heads.append(acc)
        mixed = jnp.concatenate(heads, axis=-1)

        y_att = (mixed * (gatt * _sigmoid(gatt))).astype(BF16)
        h_sum = hf_ref[rows].astype(F32) + hb_ref[rows].astype(F32)
        y_lru = (h_sum * (glru * _sigmoid(glru))).astype(BF16)

        p_att = jnp.dot(y_att, woa_ref[...], preferred_element_type=F32)
        p_lru = jnp.dot(y_lru, wol_ref[...], preferred_element_type=F32)
        gate_att = 1.0 / (1.0 + jnp.exp2(logit_att + bg_ref[:, 0:dm] * -LOG2_E))
        gate_lru = 1.0 / (1.0 + jnp.exp2(logit_lru + bg_ref[:, dm:2 * dm] * -LOG2_E))
        merged = (gate_att * p_att + gate_lru * p_lru).astype(BF16)
        y = x_ref[rows] + jnp.dot(merged, wout_ref[...], preferred_element_type=F32)
        if final:
            y = _rms_norm(y, fg_ref[...])
        out_ref[rows] = y


def _merge(x2d, outs, lses, h_f, h_b, norm_g, w_in, b_gate, w_o_att, w_o_lru, w_out, final_g, layer, final):
    t, dm = x2d.shape
    tm = MERGE_ROWS
    tiles_per_b = t // outs[1].shape[0] // tm
    att_gate_block = _col_block(N_GROUPS * QKV_WIDTH, GROUP_WIDTH)
    others_block = _col_block(N_GROUPS * QKV_WIDTH + GROUP_WIDTH + dm, 3 * dm)

    def rows(width):
        return pl.BlockSpec((tm, width), lambda i: (i, 0))

    def dilated_rows(a):
        d, width = a.shape[1], a.shape[3]
        return pl.BlockSpec((1, d, tm // d, width), lambda i: (i // tiles_per_b, 0, i % tiles_per_b, 0))

    n_blocks = tm // MERGE_SUB_ROWS
    return pl.pallas_call(
        functools.partial(_merge_body, final=final),
        grid=(t // tm,),
        in_specs=[rows(GROUP_WIDTH), dilated_rows(outs[1]), dilated_rows(outs[2]),
                  rows(LANES), dilated_rows(lses[1]), dilated_rows(lses[2])] + [rows(dm)] * 3
                 + [_layer(norm_g, layer), _layer(w_in, layer, cols=GROUP_WIDTH, col_block=att_gate_block),
                    _layer(w_in, layer, cols=3 * dm, col_block=others_block), _layer(b_gate, layer),
                    _layer(w_o_att, layer), _layer(w_o_lru, layer), _layer(w_out, layer),
                    pl.BlockSpec(final_g.shape, lambda i: (0, 0), pipeline_mode=pl.Buffered(1))],
        out_specs=rows(dm),
        out_shape=jax.ShapeDtypeStruct((t, dm), F32),
        scratch_shapes=[pltpu.VMEM((n_blocks, N_GROUPS - 1, GROUP_WIDTH // LANES, MERGE_SUB_ROWS, LANES), F32),
                        pltpu.VMEM((n_blocks, N_GROUPS - 1, 1, MERGE_SUB_ROWS, LANES), F32)],
        compiler_params=pltpu.CompilerParams(dimension_semantics=("parallel",)),
        name="merge_final" if final else "merge",
    )(*outs, *lses, h_f, h_b, x2d, norm_g, w_in, w_in, b_gate, w_o_att, w_o_lru, w_out, final_g)


def kernel(x, norm_g, w_in, b_gate, conv_w, conv_b, rg_w_a, rg_b_a, rg_w_x, rg_b_x, rg_lam,
           w_o_att, w_o_lru, w_out, final_g):
    b, s, dm = x.shape
    depth = w_in.shape[0]
    t = b * s
    w_o_att, w_o_lru, w_out = (w.astype(BF16) for w in (w_o_att, w_o_lru, w_out))
    n_in = w_in.shape[-1]
    col = jnp.arange(n_in)
    col_scale = jnp.where(col < N_GROUPS * GROUP_WIDTH, HEAD_DIM ** -0.5 * LOG2_E,
                          jnp.where(col >= n_in - 2 * dm, -LOG2_E, 1.0)).astype(F32)
    w_in = (w_in * col_scale).astype(BF16)
    rg_w_a, rg_w_x = ((w * -LOG2_E).astype(BF16) for w in (rg_w_a, rg_w_x))
    norm_g, b_gate, conv_b = (v.reshape(depth, 1, -1) for v in (norm_g, b_gate, conv_b))
    final_g = final_g.reshape(1, dm)

    x2d = x.reshape(t, dm)
    for layer in range(depth):
        nat2d, qkv4, qkv16 = _norm_proj(x2d, norm_g, w_in, layer, b)
        nat = nat2d.reshape(b, s, -1)
        outs, lses = [], []
        for group, qkv in enumerate((nat, qkv4, qkv16)):
            o, lse = _attention_group(qkv, group, ATT_PATTERNS[group][1])
            outs.append(o.reshape(t, GROUP_WIDTH) if group == 0 else o)
            lses.append(lse.reshape(t, LANES) if group == 0 else lse)
        h_f, h_b = _lru(nat, conv_w, conv_b, rg_w_a, rg_b_a, rg_w_x, rg_b_x, rg_lam, layer)
        x2d = _merge(x2d, outs, lses, h_f.reshape(t, dm), h_b.reshape(t, dm), norm_g, w_in, b_gate,
                     w_o_att, w_o_lru, w_out, final_g, layer, final=(layer == depth - 1))
    return x2d.reshape(b, s, dm)
```

```python
import functools

import jax
import jax.numpy as jnp
from jax import lax
from jax.experimental import pallas as pl
from jax.experimental.pallas import tpu as pltpu

F32 = jnp.float32
BF16 = jnp.bfloat16

HEAD_DIM = 128
HEADS_PER_GROUP = 4
ATT_PATTERNS = ((128, 1), (512, 4), (2048, 16))
N_GROUPS = len(ATT_PATTERNS)
N_ATT_HEADS = N_GROUPS * HEADS_PER_GROUP
GROUP_WIDTH = HEADS_PER_GROUP * HEAD_DIM
QKV_WIDTH = 3 * GROUP_WIDTH
NEG_INF = -1e30
LRU_BLOCKS = 4
RG_C = 8.0
NORM_EPS = 1e-6
CONV_LEFT = 2

XLRU_COL = 0
NAT_QKV_COL = 2

LANES = 128
F32_ROWS = 8
BF16_ROWS = 16
HALF_KEYS = 64
Q_SUB = 128
K_SUB = Q_SUB + 2 * HALF_KEYS
LSE_REP = LANES // HEADS_PER_GROUP
SEGMENTS = F32_ROWS

PROJ_ROWS = 1024
PROJ_COLS = 512
ATTN_Q_ROWS = {1: 2048, 4: 2048, 16: 512}
ATTN_RESIDUES = {1: 1, 4: 1, 16: 4}
LRU_ROWS = 1024
MERGE_ROWS = 1024
MERGE_SUB_ROWS = 256

LOG2_E = 1.4426950408889634
LN_2 = 0.6931471805599453


def _sigmoid(x):
    return 1.0 / (1.0 + jnp.exp2(x * -LOG2_E))


def _rms_norm(x, g):
    ms = jnp.mean(x * x, axis=-1, keepdims=True)
    return x * lax.rsqrt(ms + NORM_EPS) * g


def _col_block(first_col, cols):
    assert first_col % cols == 0, (first_col, cols)
    return first_col // cols


def _layer(stacked, layer, cols=None, col_block=0):
    block = stacked.shape[1:] if cols is None else stacked.shape[1:-1] + (cols,)
    index = (layer,) + (0,) * (len(block) - 1) + (col_block,)
    return pl.BlockSpec((None,) + block, lambda *_: index, pipeline_mode=pl.Buffered(1))


DILATIONS = tuple(d for _, d in ATT_PATTERNS)
assert all(w // (2 * d) == HALF_KEYS for w, d in ATT_PATTERNS)


def _norm_proj_body(x_ref, g_ref, wqkv_ref, wx_ref, nat_ref, *refs):
    dilated_out = refs[:N_GROUPS - 1]
    xn_ref, *dilated_lhs, slab_ref = refs[N_GROUPS - 1:]
    tm, dm = x_ref.shape
    width = wx_ref.shape[1]
    gw = GROUP_WIDTH

    y = _rms_norm(x_ref[...], g_ref[...])
    xn_ref[...] = y.astype(BF16)
    for c in range(dm // LANES):
        slab_ref[c] = y[:, c * LANES:(c + 1) * LANES]

    def project(lhs_ref, w_ref, col, cols):
        return jnp.dot(lhs_ref[...], w_ref[:, col:col + cols], preferred_element_type=F32).astype(BF16)

    for n0 in range(0, width, PROJ_COLS):
        nat_ref[:, n0:n0 + PROJ_COLS] = project(xn_ref, wx_ref, n0, PROJ_COLS)
    for part in range(3):
        nat_ref[:, width + part * gw:width + (part + 1) * gw] = project(xn_ref, wqkv_ref, part * N_GROUPS * gw, gw)

    for group, (lhs_ref, out_ref) in enumerate(zip(dilated_lhs, dilated_out), start=1):
        d = DILATIONS[group]
        rows = tm // d
        for c in range(dm // LANES):
            for r in range(d):
                lhs_ref[r * rows:(r + 1) * rows, c * LANES:(c + 1) * LANES] = (
                    slab_ref[c, pl.ds(r, rows, stride=d), :].astype(BF16))
        for part in range(3):
            out_ref[0, :, :, part * gw:(part + 1) * gw] = (
                project(lhs_ref, wqkv_ref, (part * N_GROUPS + group) * gw, gw).reshape(d, rows, gw))


def _norm_proj(x2d, norm_g, w_in, layer, batch):
    t, dm = x2d.shape
    s = t // batch
    n_nat = dm + QKV_WIDTH
    tm = PROJ_ROWS
    tiles_per_b = s // tm
    x_lru_block = _col_block(N_GROUPS * QKV_WIDTH + GROUP_WIDTH, dm)

    def strided_spec(d):
        return pl.BlockSpec((1, d, tm // d, QKV_WIDTH), lambda i: (i // tiles_per_b, 0, i % tiles_per_b, 0))

    return pl.pallas_call(
        _norm_proj_body,
        grid=(t // tm,),
        in_specs=[pl.BlockSpec((tm, dm), lambda i: (i, 0)), _layer(norm_g, layer),
                  _layer(w_in, layer, cols=N_GROUPS * QKV_WIDTH), _layer(w_in, layer, cols=dm, col_block=x_lru_block)],
        out_specs=[pl.BlockSpec((tm, n_nat), lambda i: (i, 0))] + [strided_spec(d) for d in DILATIONS[1:]],
        out_shape=[jax.ShapeDtypeStruct((t, n_nat), BF16)]
                  + [jax.ShapeDtypeStruct((batch, d, s // d, QKV_WIDTH), BF16) for d in DILATIONS[1:]],
        scratch_shapes=[pltpu.VMEM((tm, dm), BF16)] * N_GROUPS + [pltpu.VMEM((dm // LANES, tm, LANES), F32)],
        compiler_params=pltpu.CompilerParams(dimension_semantics=("parallel",)),
        name="norm_proj",
    )(x2d, norm_g, w_in, w_in)


def _attn_body(q_ref, kp_ref, kc_ref, kn_ref, vp_ref, vc_ref, vn_ref, o_ref, lse_ref,
               kcat, vcat, bias, *, group, dilation, tq, residues):
    qi = pl.program_id(1)
    r = pl.program_id(2)
    n_q = pl.num_programs(1)
    kcol = lax.broadcasted_iota(jnp.int32, (Q_SUB, K_SUB), 1)
    slopes = [2.0 ** (-8.0 * (group * HEADS_PER_GROUP + h + 1) / N_ATT_HEADS) for h in range(HEADS_PER_GROUP)]

    @pl.when((pl.program_id(0) == 0) & (qi == 0) & (r == 0))
    def _():
        qrow = lax.broadcasted_iota(jnp.int32, (Q_SUB, K_SUB), 0)
        absd = jnp.abs(kcol - HALF_KEYS - qrow)
        dist = (absd * dilation).astype(F32)
        for h in range(HEADS_PER_GROUP):
            bias[h] = jnp.where(absd <= HALF_KEYS, (-slopes[h] * LOG2_E) * dist, NEG_INF)
            vcat[:, :, (2 * h + 1) * HEAD_DIM:(2 * h + 2) * HEAD_DIM] = jnp.ones(
                (residues, tq + 2 * HALF_KEYS, HEAD_DIM), BF16)

    lane = lax.broadcasted_iota(jnp.int32, (Q_SUB, LANES), 1)
    for res in range(residues):
        blk = (0,) if dilation == 1 else (0, res)
        _attn_residue(q_ref, kp_ref, kc_ref, kn_ref, vp_ref, vc_ref, vn_ref, o_ref, lse_ref,
                      kcat.at[res], vcat.at[res], bias, blk, qi, n_q, kcol, lane, tq)


def _attn_residue(q_ref, kp_ref, kc_ref, kn_ref, vp_ref, vc_ref, vn_ref, o_ref, lse_ref,
                  kcat, vcat, bias, blk, qi, n_q, kcol, lane, tq):
    kcat[0:HALF_KEYS] = kp_ref[blk]
    kcat[HALF_KEYS:HALF_KEYS + tq] = kc_ref[blk]
    kcat[HALF_KEYS + tq:] = kn_ref[blk]
    for h in range(HEADS_PER_GROUP):
        src = slice(h * HEAD_DIM, (h + 1) * HEAD_DIM)
        dst = slice(2 * h * HEAD_DIM, (2 * h + 1) * HEAD_DIM)
        vcat[0:HALF_KEYS, dst] = vp_ref[blk + (slice(None), src)]
        vcat[HALF_KEYS:HALF_KEYS + tq, dst] = vc_ref[blk + (slice(None), src)]
        vcat[HALF_KEYS + tq:, dst] = vn_ref[blk + (slice(None), src)]

    n_sub_tiles = tq // Q_SUB
    for sub in range(n_sub_tiles):
        r0 = sub * Q_SUB
        m_tile = jnp.zeros((Q_SUB, LANES), F32)
        den_tile = jnp.ones((Q_SUB, LANES), F32)
        for h in range(HEADS_PER_GROUP):
            c0 = h * HEAD_DIM
            q = q_ref[blk + (slice(r0, r0 + Q_SUB), slice(c0, c0 + HEAD_DIM))]
            k = kcat[r0:r0 + K_SUB, c0:c0 + HEAD_DIM]
            raw = lax.dot_general(q, k, (((1,), (1,)), ((), ())), preferred_element_type=F32)
            s2 = raw + bias[h]
            if sub == 0:
                s2 = jnp.where((qi > 0) | (kcol >= HALF_KEYS), s2, NEG_INF)
            if sub == n_sub_tiles - 1:
                s2 = jnp.where((qi < n_q - 1) | (kcol < HALF_KEYS + Q_SUB), s2, NEG_INF)
            m2 = jnp.max(s2, axis=-1, keepdims=True)
            p = jnp.exp2(s2 - m2).astype(BF16)
            acc = jnp.dot(p, vcat[r0:r0 + K_SUB, 2 * c0:2 * c0 + 2 * HEAD_DIM], preferred_element_type=F32)
            den = acc[:, HEAD_DIM:]
            o = acc[:, :HEAD_DIM] / den
            o_ref[blk + (slice(r0, r0 + Q_SUB), slice(c0, c0 + HEAD_DIM))] = o.astype(BF16)
            head_lanes = (lane >= h * LSE_REP) & (lane < (h + 1) * LSE_REP)
            m_tile = jnp.where(head_lanes, m2, m_tile)
            den_tile = jnp.where(head_lanes, den[:, :LANES], den_tile)
        lse_ref[blk + (slice(r0, r0 + Q_SUB), slice(None))] = (m_tile + jnp.log2(den_tile)) * LN_2


def _attention_group(qkv, group, dilation):
    if dilation == 1:
        b, s, _ = qkv.shape
    else:
        b, _, n_sub, _ = qkv.shape
        s = n_sub * dilation
    n_sub = s // dilation
    tq = min(ATTN_Q_ROWS[dilation], n_sub)
    residues = ATTN_RESIDUES[dilation]
    q_col = NAT_QKV_COL if dilation == 1 else 0
    halo_per_tile = tq // HALF_KEYS
    last_halo = n_sub // HALF_KEYS - 1

    def spec(rows, row_index, col, width=GROUP_WIDTH):
        if dilation == 1:
            return pl.BlockSpec((1, rows, width), lambda bi, qi, r: (bi, row_index(qi), col))
        return pl.BlockSpec((1, residues, rows, width), lambda bi, qi, r: (bi, r, row_index(qi), col))

    def main_spec(col):
        return spec(tq, lambda qi: qi, col)

    def prev_spec(col):
        return spec(HALF_KEYS, lambda qi: jnp.maximum(qi * halo_per_tile - 1, 0), col)

    def next_spec(col):
        return spec(HALF_KEYS, lambda qi: jnp.minimum((qi + 1) * halo_per_tile, last_halo), col)

    rows_shape = (b, s) if dilation == 1 else (b, dilation, n_sub)
    return pl.pallas_call(
        functools.partial(_attn_body, group=group, dilation=dilation, tq=tq, residues=residues),
        grid=(b, n_sub // tq, dilation // residues),
        in_specs=[main_spec(q_col),
                  prev_spec(q_col + 1), main_spec(q_col + 1), next_spec(q_col + 1),
                  prev_spec(q_col + 2), main_spec(q_col + 2), next_spec(q_col + 2)],
        out_specs=[main_spec(0), spec(tq, lambda qi: qi, 0, LANES)],
        out_shape=[jax.ShapeDtypeStruct(rows_shape + (GROUP_WIDTH,), BF16),
                   jax.ShapeDtypeStruct(rows_shape + (LANES,), F32)],
        scratch_shapes=[pltpu.VMEM((residues, tq + 2 * HALF_KEYS, GROUP_WIDTH), BF16),
                        pltpu.VMEM((residues, tq + 2 * HALF_KEYS, 2 * GROUP_WIDTH), BF16),
                        pltpu.VMEM((HEADS_PER_GROUP, Q_SUB, K_SUB), F32)],
        compiler_params=pltpu.CompilerParams(dimension_semantics=("arbitrary", "arbitrary", "arbitrary")),
        name=f"attn_d{dilation}",
    )(qkv, qkv, qkv, qkv, qkv, qkv, qkv)


def _lru_direction(xp_ref, xc_ref, xn_ref, h_ref, tile, n_tiles, rev, dirn,
                   cw_ref, cb_ref, wa_ref, ba_ref, wx_ref, bx_ref, lam_ref,
                   xpad, xs_s, xc_s, a_s, b_s, hpad, carry):
    tt, width = xc_ref.shape[1:]
    bw = width // LRU_BLOCKS
    chunks_per_block = bw // LANES
    seg_len = tt // SEGMENTS
    pitch = seg_len + SEGMENTS
    sub = lax.broadcasted_iota(jnp.int32, (SEGMENTS, bw), 0)
    if rev:
        start_rows = slice(tt - SEGMENTS, tt)
        first = (tile == n_tiles - 1) & (sub == SEGMENTS - 1)
    else:
        start_rows = slice(0, SEGMENTS)
        first = (tile == 0) & (sub == 0)
    steps = range(seg_len - 1, -1, -1) if rev else range(seg_len)
    segs = range(SEGMENTS - 1, -1, -1) if rev else range(SEGMENTS)
    sub1 = lax.broadcasted_iota(jnp.int32, (SEGMENTS, LANES), 0)

    for blk in range(LRU_BLOCKS):
        c0 = blk * bw
        cols = slice(c0, c0 + bw)
        for cc in range(chunks_per_block):
            c = blk * chunks_per_block + cc
            xf = xc_ref[0, :, c * LANES:(c + 1) * LANES].astype(F32)
            for k in range(SEGMENTS):
                xpad[c, k * pitch:k * pitch + seg_len, :] = xf[k * seg_len:(k + 1) * seg_len]
            for j in range(seg_len):
                xs_s[(j + CONV_LEFT) * SEGMENTS:(j + CONV_LEFT + 1) * SEGMENTS, cc * LANES:(cc + 1) * LANES] = (
                    xpad[c, pl.ds(j, SEGMENTS, stride=pitch), :])
        halo_prev = jnp.where(tile == 0, 0.0, xp_ref[0, BF16_ROWS - CONV_LEFT:BF16_ROWS, cols].astype(F32))
        halo_next = jnp.where(tile == n_tiles - 1, 0.0, xn_ref[0, 0:1, cols].astype(F32))
        last1 = xs_s[(seg_len + 1) * SEGMENTS:(seg_len + 2) * SEGMENTS]
        last2 = xs_s[seg_len * SEGMENTS:(seg_len + 1) * SEGMENTS]
        first0 = xs_s[CONV_LEFT * SEGMENTS:(CONV_LEFT + 1) * SEGMENTS]
        xs_s[SEGMENTS:2 * SEGMENTS] = jnp.where(sub == 0, halo_prev[1:2], pltpu.roll(last1, 1, 0))
        xs_s[0:SEGMENTS] = jnp.where(sub == 0, halo_prev[0:1], pltpu.roll(last2, 1, 0))
        xs_s[(seg_len + 2) * SEGMENTS:(seg_len + 3) * SEGMENTS] = jnp.where(
            sub == SEGMENTS - 1, halo_next, pltpu.roll(first0, SEGMENTS - 1, 0))
        xc_s[...] = cb_ref[:, cols] + sum(
            xs_s[j * SEGMENTS:j * SEGMENTS + tt] * cw_ref[j:j + 1, cols] for j in range(4))

        lam = lam_ref[dirn:dirn + 1, cols]
        neg_lam = -lam
        coef = -RG_C * (jnp.maximum(neg_lam, 0.0) + jnp.log1p(jnp.exp(-jnp.abs(neg_lam))))
        xblk = xc_s[...].astype(BF16)
        pa = jnp.dot(xblk, wa_ref[dirn, blk], preferred_element_type=F32) + ba_ref[dirn:dirn + 1, cols] * -LOG2_E
        px = jnp.dot(xblk, wx_ref[dirn, blk], preferred_element_type=F32) + bx_ref[dirn:dirn + 1, cols] * -LOG2_E
        log_a = coef / (1.0 + jnp.exp2(pa))
        a = jnp.exp(log_a)
        u = jnp.tanh(log_a) * (-1.0 - a * a)
        mult = jnp.where(u > 0.0, u * lax.rsqrt(u), 0.0)
        gated = xc_s[...] / (1.0 + jnp.exp2(px))
        b = mult * gated
        a_s[:, cols] = a
        b_s[:, cols] = b
        b_s[start_rows, cols] = jnp.where(first, gated[start_rows], b[start_rows])

        for cc in range(chunks_per_block):
            c = blk * chunks_per_block + cc
            cs = slice(c * LANES, (c + 1) * LANES)
            prod = jnp.ones((SEGMENTS, LANES), F32)
            h = jnp.zeros((SEGMENTS, LANES), F32)
            for j in steps:
                aj = a_s[j * SEGMENTS:(j + 1) * SEGMENTS, cs]
                h = aj * h + b_s[j * SEGMENTS:(j + 1) * SEGMENTS, cs]
                prod = aj * prod
            state = carry[dirn:dirn + 1, cs]
            h0 = jnp.zeros((SEGMENTS, LANES), F32)
            for k in segs:
                h0 = jnp.where(sub1 == k, state, h0)
                state = prod[k:k + 1] * state + h[k:k + 1]
            carry[dirn:dirn + 1, cs] = state
            h = h0
            for j in steps:
                h = a_s[j * SEGMENTS:(j + 1) * SEGMENTS, cs] * h + b_s[j * SEGMENTS:(j + 1) * SEGMENTS, cs]
                hpad[c, pl.ds(j, SEGMENTS, stride=pitch), :] = h
            for k in range(SEGMENTS):
                h_ref[0, k * seg_len:(k + 1) * seg_len, cs] = hpad[c, k * pitch:k * pitch + seg_len, :].astype(h_ref.dtype)


def _lru_body(xfp_ref, xfc_ref, xfn_ref, xbp_ref, xbc_ref, xbn_ref,
              cw_ref, cb_ref, wa_ref, ba_ref, wx_ref, bx_ref, lam_ref,
              hf_ref, hb_ref, xpad_f, xs_f, xc_f, a_f, b_f, hpad_f, xpad_b, xs_b, xc_b, a_b, b_b, hpad_b, carry):
    i = pl.program_id(1)
    n_tiles = pl.num_programs(1)

    @pl.when(i == 0)
    def _():
        carry[...] = jnp.zeros_like(carry)

    params = (cw_ref, cb_ref, wa_ref, ba_ref, wx_ref, bx_ref, lam_ref)
    _lru_direction(xfp_ref, xfc_ref, xfn_ref, hf_ref, i, n_tiles, False, 0, *params,
                   xpad_f, xs_f, xc_f, a_f, b_f, hpad_f, carry)
    _lru_direction(xbp_ref, xbc_ref, xbn_ref, hb_ref, n_tiles - 1 - i, n_tiles, True, 1, *params,
                   xpad_b, xs_b, xc_b, a_b, b_b, hpad_b, carry)


def _lru(nat, conv_w, conv_b, w_a, b_a, w_x, b_x, lam, layer):
    b, s, _ = nat.shape
    width = conv_w.shape[-1]
    tt = LRU_ROWS
    n_tiles = s // tt
    halo_per_tile = tt // BF16_ROWS
    last_halo = s // BF16_ROWS - 1

    def tile_of(i, rev):
        return n_tiles - 1 - i if rev else i

    def specs(rev):
        return [
            pl.BlockSpec((1, BF16_ROWS, width),
                         lambda bi, i: (bi, jnp.maximum(tile_of(i, rev) * halo_per_tile - 1, 0), XLRU_COL)),
            pl.BlockSpec((1, tt, width), lambda bi, i: (bi, tile_of(i, rev), XLRU_COL)),
            pl.BlockSpec((1, BF16_ROWS, width),
                         lambda bi, i: (bi, jnp.minimum((tile_of(i, rev) + 1) * halo_per_tile, last_halo), XLRU_COL)),
        ]

    consts = (conv_w, conv_b, w_a, b_a, w_x, b_x, lam)
    padded_rows = tt + SEGMENTS * SEGMENTS
    bw = width // LRU_BLOCKS
    per_direction = [pltpu.VMEM((width // LANES, padded_rows, LANES), F32),
                     pltpu.VMEM((tt + (CONV_LEFT + 1) * SEGMENTS, bw), F32),
                     pltpu.VMEM((tt, bw), F32),
                     pltpu.VMEM((tt, width), F32),
                     pltpu.VMEM((tt, width), F32),
                     pltpu.VMEM((width // LANES, padded_rows, LANES), F32)]
    return pl.pallas_call(
        _lru_body,
        grid=(b, n_tiles),
        in_specs=specs(False) + specs(True) + [_layer(a, layer) for a in consts],
        out_specs=[pl.BlockSpec((1, tt, width), lambda bi, i: (bi, i, 0)),
                   pl.BlockSpec((1, tt, width), lambda bi, i: (bi, n_tiles - 1 - i, 0))],
        out_shape=[jax.ShapeDtypeStruct((b, s, width), BF16)] * 2,
        scratch_shapes=per_direction + per_direction + [pltpu.VMEM((F32_ROWS, width), F32)],
        compiler_params=pltpu.CompilerParams(dimension_semantics=("parallel", "arbitrary")),
        name="lru",
    )(nat, nat, nat, nat, nat, nat, *consts)


def _token_order_rows(ref, r0, n_rows, slab):
    d, width = ref.shape[1], ref.shape[3]
    per = n_rows // d
    for r in range(d):
        piece = ref[0, r, r0 // d:r0 // d + per, :].astype(F32)
        for c in range(width // LANES):
            slab[c, pl.ds(r, per, stride=d), :] = piece[:, c * LANES:(c + 1) * LANES]
    return [slab[c] for c in range(width // LANES)]


def _merge_body(o0_ref, o1_ref, o2_ref, l0_ref, l1_ref, l2_ref, hf_ref, hb_ref, x_ref,
                ng_ref, wga_ref, wgo_ref, bg_ref, woa_ref, wol_ref, wout_ref, fg_ref, out_ref,
                o_slabs, l_slabs, *, final):
    tm, dm = x_ref.shape
    for blk, r0 in enumerate(range(0, tm, MERGE_SUB_ROWS)):
        rows = slice(r0, r0 + MERGE_SUB_ROWS)
        xn = _rms_norm(x_ref[rows], ng_ref[...]).astype(BF16)

        gatt = jnp.dot(xn, wga_ref[...], preferred_element_type=F32)
        glru, logit_att, logit_lru = (
            jnp.dot(xn, wgo_ref[:, c0:c0 + dm], preferred_element_type=F32) for c0 in (0, dm, 2 * dm))

        o_dilated = [_token_order_rows(ref, r0, MERGE_SUB_ROWS, o_slabs.at[blk, g])
                     for g, ref in enumerate((o1_ref, o2_ref))]
        l1, l2 = (_token_order_rows(ref, r0, MERGE_SUB_ROWS, l_slabs.at[blk, g])[0]
                  for g, ref in enumerate((l1_ref, l2_ref)))
        l0 = l0_ref[rows]
        mx = jnp.maximum(jnp.maximum(l0, l1), l2)
        e0, e1, e2 = jnp.exp(l0 - mx), jnp.exp(l1 - mx), jnp.exp(l2 - mx)
        inv = 1.0 / (e0 + e1 + e2)
        alphas = (e0 * inv, e1 * inv, e2 * inv)

        heads = []
        for h in range(HEADS_PER_GROUP):
            group_outs = (o0_ref[rows, h * HEAD_DIM:(h + 1) * HEAD_DIM].astype(F32), o_dilated[0][h], o_dilated[1][h])
            acc = jnp.zeros((MERGE_SUB_ROWS, HEAD_DIM), F32)
            for g in range(N_GROUPS):
                w = jnp.broadcast_to(alphas[g][:, h * LSE_REP:h * LSE_REP + 1], (MERGE_SUB_ROWS, HEAD_DIM))
                acc = acc + w * group_outs[g]
            heads.append(acc)
        mixed = jnp.concatenate(heads, axis=-1)

        y_att = (mixed * (gatt * _sigmoid(gatt))).astype(BF16)
        h_sum = hf_ref[rows].astype(F32) + hb_ref[rows].astype(F32)
        y_lru = (h_sum * (glru * _sigmoid(glru))).astype(BF16)

        p_att = jnp.dot(y_att, woa_ref[...], preferred_element_type=F32)
        p_lru = jnp.dot(y_lru, wol_ref[...], preferred_element_type=F32)
        gate_att = 1.0 / (1.0 + jnp.exp2(logit_att + bg_ref[:, 0:dm] * -LOG2_E))
        gate_lru = 1.0 / (1.0 + jnp.exp2(logit_lru + bg_ref[:, dm:2 * dm] * -LOG2_E))
        merged = (gate_att * p_att + gate_lru * p_lru).astype(BF16)
        y = x_ref[rows] + jnp.dot(merged, wout_ref[...], preferred_element_type=F32)
        if final:
            y = _rms_norm(y, fg_ref[...])
        out_ref[rows] = y


def _merge(x2d, outs, lses, h_f, h_b, norm_g, w_in, b_gate, w_o_att, w_o_lru, w_out, final_g, layer, final):
    t, dm = x2d.shape
    tm = MERGE_ROWS
    tiles_per_b = t // outs[1].shape[0] // tm
    att_gate_block = _col_block(N_GROUPS * QKV_WIDTH, GROUP_WIDTH)
    others_block = _col_block(N_GROUPS * QKV_WIDTH + GROUP_WIDTH + dm, 3 * dm)

    def rows(width):
        return pl.BlockSpec((tm, width), lambda i: (i, 0))

    def dilated_rows(a):
        d, width = a.shape[1], a.shape[3]
        return pl.BlockSpec((1, d, tm // d, width), lambda i: (i // tiles_per_b, 0, i % tiles_per_b, 0))

    n_blocks = tm // MERGE_SUB_ROWS
    return pl.pallas_call(
        functools.partial(_merge_body, final=final),
        grid=(t // tm,),
        in_specs=[rows(GROUP_WIDTH), dilated_rows(outs[1]), dilated_rows(outs[2]),
                  rows(LANES), dilated_rows(lses[1]), dilated_rows(lses[2])] + [rows(dm)] * 3
                 + [_layer(norm_g, layer), _layer(w_in, layer, cols=GROUP_WIDTH, col_block=att_gate_block),
                    _layer(w_in, layer, cols=3 * dm, col_block=others_block), _layer(b_gate, layer),
                    _layer(w_o_att, layer), _layer(w_o_lru, layer), _layer(w_out, layer),
                    pl.BlockSpec(final_g.shape, lambda i: (0, 0), pipeline_mode=pl.Buffered(1))],
        out_specs=rows(dm),
        out_shape=jax.ShapeDtypeStruct((t, dm), F32),
        scratch_shapes=[pltpu.VMEM((n_blocks, N_GROUPS - 1, GROUP_WIDTH // LANES, MERGE_SUB_ROWS, LANES), F32),
                        pltpu.VMEM((n_blocks, N_GROUPS - 1, 1, MERGE_SUB_ROWS, LANES), F32)],
        compiler_params=pltpu.CompilerParams(dimension_semantics=("parallel",)),
        name="merge_final" if final else "merge",
    )(*outs, *lses, h_f, h_b, x2d, norm_g, w_in, w_in, b_gate, w_o_att, w_o_lru, w_out, final_g)


def kernel(x, norm_g, w_in, b_gate, conv_w, conv_b, rg_w_a, rg_b_a, rg_w_x, rg_b_x, rg_lam,
           w_o_att, w_o_lru, w_out, final_g):
    b, s, dm = x.shape
    depth = w_in.shape[0]
    t = b * s
    w_o_att, w_o_lru, w_out = (w.astype(BF16) for w in (w_o_att, w_o_lru, w_out))
    n_in = w_in.shape[-1]
    col = jnp.arange(n_in)
    col_scale = jnp.where(col < N_GROUPS * GROUP_WIDTH, HEAD_DIM ** -0.5 * LOG2_E,
                          jnp.where(col >= n_in - 2 * dm, -LOG2_E, 1.0)).astype(F32)
    w_in = (w_in * col_scale).astype(BF16)
    rg_w_a, rg_w_x = ((w * -LOG2_E).astype(BF16) for w in (rg_w_a, rg_w_x))
    norm_g, b_gate, conv_b = (v.reshape(depth, 1, -1) for v in (norm_g, b_gate, conv_b))
    final_g = final_g.reshape(1, dm)

    x2d = x.reshape(t, dm)
    for layer in range(depth):
        nat2d, qkv4, qkv16 = _norm_proj(x2d, norm_g, w_in, layer, b)
        nat = nat2d.reshape(b, s, -1)
        outs, lses = [], []
        for group, qkv in enumerate((nat, qkv4, qkv16)):
            o, lse = _attention_group(qkv, group, ATT_PATTERNS[group][1])
            outs.append(o.reshape(t, GROUP_WIDTH) if group == 0 else o)
            lses.append(lse.reshape(t, LANES) if group == 0 else lse)
        h_f, h_b = _lru(nat, conv_w, conv_b, rg_w_a, rg_b_a, rg_w_x, rg_b_x, rg_lam, layer)
        x2d = _merge(x2d, outs, lses, h_f.reshape(t, dm), h_b.reshape(t, dm), norm_g, w_in, b_gate,
                     w_o_att, w_o_lru, w_out, final_g, layer, final=(layer == depth - 1))
    return x2d.reshape(b, s, dm)
```

```python
import functools

import jax
import jax.numpy as jnp
from jax import lax
from jax.experimental import pallas as pl
from jax.experimental.pallas import tpu as pltpu

F32 = jnp.float32
BF16 = jnp.bfloat16

HEAD_DIM = 128
HEADS_PER_GROUP = 4
ATT_PATTERNS = ((128, 1), (512, 4), (2048, 16))
N_GROUPS = len(ATT_PATTERNS)
N_ATT_HEADS = N_GROUPS * HEADS_PER_GROUP
GROUP_WIDTH = HEADS_PER_GROUP * HEAD_DIM
QKV_WIDTH = 3 * GROUP_WIDTH
NEG_INF = -1e30
LRU_BLOCKS = 4
RG_C = 8.0
NORM_EPS = 1e-6
CONV_LEFT = 2

XLRU_COL = 0
NAT_QKV_COL = 2

LANES = 128
F32_ROWS = 8
BF16_ROWS = 16
HALF_KEYS = 64
Q_SUB = 128
K_SUB = Q_SUB + 2 * HALF_KEYS
LSE_REP = LANES // HEADS_PER_GROUP
SEGMENTS = F32_ROWS

PROJ_ROWS = 1024
PROJ_COLS = 512
ATTN_Q_ROWS = {1: 4096, 4: 2048, 16: 512}
ATTN_RESIDUES = {1: 1, 4: 2, 16: 4}
LRU_ROWS = 1024
MERGE_ROWS = 1024
MERGE_SUB_ROWS = 256

LOG2_E = 1.4426950408889634
LN_2 = 0.6931471805599453


def _sigmoid(x):
    return 1.0 / (1.0 + jnp.exp2(x * -LOG2_E))


def _rms_norm(x, g):
    ms = jnp.mean(x * x, axis=-1, keepdims=True)
    return x * lax.rsqrt(ms + NORM_EPS) * g


def _col_block(first_col, cols):
    assert first_col % cols == 0, (first_col, cols)
    return first_col // cols


def _layer(stacked, layer, cols=None, col_block=0):
    block = stacked.shape[1:] if cols is None else stacked.shape[1:-1] + (cols,)
    index = (layer,) + (0,) * (len(block) - 1) + (col_block,)
    return pl.BlockSpec((None,) + block, lambda *_: index, pipeline_mode=pl.Buffered(1))


DILATIONS = tuple(d for _, d in ATT_PATTERNS)
assert all(w // (2 * d) == HALF_KEYS for w, d in ATT_PATTERNS)


def _norm_proj_body(x_ref, g_ref, wqkv_ref, wx_ref, nat_ref, *refs):
    dilated_out = refs[:N_GROUPS - 1]
    xn_ref, *dilated_lhs, slab_ref = refs[N_GROUPS - 1:]
    tm, dm = x_ref.shape
    width = wx_ref.shape[1]
    gw = GROUP_WIDTH

    y = _rms_norm(x_ref[...], g_ref[...])
    xn_ref[...] = y.astype(BF16)
    for c in range(dm // LANES):
        slab_ref[c] = y[:, c * LANES:(c + 1) * LANES]

    def project(lhs_ref, w_ref, col, cols):
        return jnp.dot(lhs_ref[...], w_ref[:, col:col + cols], preferred_element_type=F32).astype(BF16)

    for n0 in range(0, width, PROJ_COLS):
        nat_ref[:, n0:n0 + PROJ_COLS] = project(xn_ref, wx_ref, n0, PROJ_COLS)
    for part in range(3):
        nat_ref[:, width + part * gw:width + (part + 1) * gw] = project(xn_ref, wqkv_ref, part * N_GROUPS * gw, gw)

    for group, (lhs_ref, out_ref) in enumerate(zip(dilated_lhs, dilated_out), start=1):
        d = DILATIONS[group]
        rows = tm // d
        for c in range(dm // LANES):
            for r in range(d):
                lhs_ref[r * rows:(r + 1) * rows, c * LANES:(c + 1) * LANES] = (
                    slab_ref[c, pl.ds(r, rows, stride=d), :].astype(BF16))
        for part in range(3):
            out_ref[0, :, :, part * gw:(part + 1) * gw] = (
                project(lhs_ref, wqkv_ref, (part * N_GROUPS + group) * gw, gw).reshape(d, rows, gw))


def _norm_proj(x2d, norm_g, w_in, layer, batch):
    t, dm = x2d.shape
    s = t // batch
    n_nat = dm + QKV_WIDTH
    tm = PROJ_ROWS
    tiles_per_b = s // tm
    x_lru_block = _col_block(N_GROUPS * QKV_WIDTH + GROUP_WIDTH, dm)

    def strided_spec(d):
        return pl.BlockSpec((1, d, tm // d, QKV_WIDTH), lambda i: (i // tiles_per_b, 0, i % tiles_per_b, 0))

    return pl.pallas_call(
        _norm_proj_body,
        grid=(t // tm,),
        in_specs=[pl.BlockSpec((tm, dm), lambda i: (i, 0)), _layer(norm_g, layer),
                  _layer(w_in, layer, cols=N_GROUPS * QKV_WIDTH), _layer(w_in, layer, cols=dm, col_block=x_lru_block)],
        out_specs=[pl.BlockSpec((tm, n_nat), lambda i: (i, 0))] + [strided_spec(d) for d in DILATIONS[1:]],
        out_shape=[jax.ShapeDtypeStruct((t, n_nat), BF16)]
                  + [jax.ShapeDtypeStruct((batch, d, s // d, QKV_WIDTH), BF16) for d in DILATIONS[1:]],
        scratch_shapes=[pltpu.VMEM((tm, dm), BF16)] * N_GROUPS + [pltpu.VMEM((dm // LANES, tm, LANES), F32)],
        compiler_params=pltpu.CompilerParams(dimension_semantics=("parallel",)),
        name="norm_proj",
    )(x2d, norm_g, w_in, w_in)


def _attn_body(q_ref, kp_ref, kc_ref, kn_ref, vp_ref, vc_ref, vn_ref, o_ref, lse_ref,
               kcat, vcat, bias, *, group, dilation, tq, residues):
    qi = pl.program_id(1)
    r = pl.program_id(2)
    n_q = pl.num_programs(1)
    kcol = lax.broadcasted_iota(jnp.int32, (Q_SUB, K_SUB), 1)
    slopes = [2.0 ** (-8.0 * (group * HEADS_PER_GROUP + h + 1) / N_ATT_HEADS) for h in range(HEADS_PER_GROUP)]

    @pl.when((pl.program_id(0) == 0) & (qi == 0) & (r == 0))
    def _():
        qrow = lax.broadcasted_iota(jnp.int32, (Q_SUB, K_SUB), 0)
        absd = jnp.abs(kcol - HALF_KEYS - qrow)
        dist = (absd * dilation).astype(F32)
        for h in range(HEADS_PER_GROUP):
            bias[h] = jnp.where(absd <= HALF_KEYS, (-slopes[h] * LOG2_E) * dist, NEG_INF)
            vcat[:, :, (2 * h + 1) * HEAD_DIM:(2 * h + 2) * HEAD_DIM] = jnp.ones(
                (residues, tq + 2 * HALF_KEYS, HEAD_DIM), BF16)

    lane = lax.broadcasted_iota(jnp.int32, (Q_SUB, LANES), 1)
    for res in range(residues):
        blk = (0,) if dilation == 1 else (0, res)
        _attn_residue(q_ref, kp_ref, kc_ref, kn_ref, vp_ref, vc_ref, vn_ref, o_ref, lse_ref,
                      kcat.at[res], vcat.at[res], bias, blk, qi, n_q, kcol, lane, tq)


def _attn_residue(q_ref, kp_ref, kc_ref, kn_ref, vp_ref, vc_ref, vn_ref, o_ref, lse_ref,
                  kcat, vcat, bias, blk, qi, n_q, kcol, lane, tq):
    kcat[0:HALF_KEYS] = kp_ref[blk]
    kcat[HALF_KEYS:HALF_KEYS + tq] = kc_ref[blk]
    kcat[HALF_KEYS + tq:] = kn_ref[blk]
    for h in range(HEADS_PER_GROUP):
        src = slice(h * HEAD_DIM, (h + 1) * HEAD_DIM)
        dst = slice(2 * h * HEAD_DIM, (2 * h + 1) * HEAD_DIM)
        vcat[0:HALF_KEYS, dst] = vp_ref[blk + (slice(None), src)]
        vcat[HALF_KEYS:HALF_KEYS + tq, dst] = vc_ref[blk + (slice(None), src)]
        vcat[HALF_KEYS + tq:, dst] = vn_ref[blk + (slice(None), src)]

    n_sub_tiles = tq // Q_SUB
    for sub in range(n_sub_tiles):
        r0 = sub * Q_SUB
        m_tile = jnp.zeros((Q_SUB, LANES), F32)
        den_tile = jnp.ones((Q_SUB, LANES), F32)
        for h in range(HEADS_PER_GROUP):
            c0 = h * HEAD_DIM
            q = q_ref[blk + (slice(r0, r0 + Q_SUB), slice(c0, c0 + HEAD_DIM))]
            k = kcat[r0:r0 + K_SUB, c0:c0 + HEAD_DIM]
            raw = lax.dot_general(q, k, (((1,), (1,)), ((), ())), preferred_element_type=F32)
            s2 = raw + bias[h]
            if sub == 0:
                s2 = jnp.where((qi > 0) | (kcol >= HALF_KEYS), s2, NEG_INF)
            if sub == n_sub_tiles - 1:
                s2 = jnp.where((qi < n_q - 1) | (kcol < HALF_KEYS + Q_SUB), s2, NEG_INF)
            m2 = jnp.max(s2, axis=-1, keepdims=True)
            p = jnp.exp2(s2 - m2).astype(BF16)
            acc = jnp.dot(p, vcat[r0:r0 + K_SUB, 2 * c0:2 * c0 + 2 * HEAD_DIM], preferred_element_type=F32)
            den = acc[:, HEAD_DIM:]
            o = acc[:, :HEAD_DIM] / den
            o_ref[blk + (slice(r0, r0 + Q_SUB), slice(c0, c0 + HEAD_DIM))] = o.astype(BF16)
            head_lanes = (lane >= h * LSE_REP) & (lane < (h + 1) * LSE_REP)
            m_tile = jnp.where(head_lanes, m2, m_tile)
            den_tile = jnp.where(head_lanes, den[:, :LANES], den_tile)
        lse_ref[blk + (slice(r0, r0 + Q_SUB), slice(None))] = (m_tile + jnp.log2(den_tile)) * LN_2


def _attention_group(qkv, group, dilation):
    if dilation == 1:
        b, s, _ = qkv.shape
    else:
        b, _, n_sub, _ = qkv.shape
        s = n_sub * dilation
    n_sub = s // dilation
    tq = min(ATTN_Q_ROWS[dilation], n_sub)
    residues = ATTN_RESIDUES[dilation]
    q_col = NAT_QKV_COL if dilation == 1 else 0
    halo_per_tile = tq // HALF_KEYS
    last_halo = n_sub // HALF_KEYS - 1

    def spec(rows, row_index, col, width=GROUP_WIDTH):
        if dilation == 1:
            return pl.BlockSpec((1, rows, width), lambda bi, qi, r: (bi, row_index(qi), col))
        return pl.BlockSpec((1, residues, rows, width), lambda bi, qi, r: (bi, r, row_index(qi), col))

    def main_spec(col):
        return spec(tq, lambda qi: qi, col)

    def prev_spec(col):
        return spec(HALF_KEYS, lambda qi: jnp.maximum(qi * halo_per_tile - 1, 0), col)

    def next_spec(col):
        return spec(HALF_KEYS, lambda qi: jnp.minimum((qi + 1) * halo_per_tile, last_halo), col)

    rows_shape = (b, s) if dilation == 1 else (b, dilation, n_sub)
    return pl.pallas_call(
        functools.partial(_attn_body, group=group, dilation=dilation, tq=tq, residues=residues),
        grid=(b, n_sub // tq, dilation // residues),
        in_specs=[main_spec(q_col),
                  prev_spec(q_col + 1), main_spec(q_col + 1), next_spec(q_col + 1),
                  prev_spec(q_col + 2), main_spec(q_col + 2), next_spec(q_col + 2)],
        out_specs=[main_spec(0), spec(tq, lambda qi: qi, 0, LANES)],
        out_shape=[jax.ShapeDtypeStruct(rows_shape + (GROUP_WIDTH,), BF16),
                   jax.ShapeDtypeStruct(rows_shape + (LANES,), F32)],
        scratch_shapes=[pltpu.VMEM((residues, tq + 2 * HALF_KEYS, GROUP_WIDTH), BF16),
                        pltpu.VMEM((residues, tq + 2 * HALF_KEYS, 2 * GROUP_WIDTH), BF16),
                        pltpu.VMEM((HEADS_PER_GROUP, Q_SUB, K_SUB), F32)],
        compiler_params=pltpu.CompilerParams(dimension_semantics=("arbitrary", "arbitrary", "arbitrary")),
        name=f"attn_d{dilation}",
    )(qkv, qkv, qkv, qkv, qkv, qkv, qkv)


def _lru_direction(xp_ref, xc_ref, xn_ref, h_ref, tile, n_tiles, rev, dirn,
                   cw_ref, cb_ref, wa_ref, ba_ref, wx_ref, bx_ref, lam_ref,
                   xpad, a_s, b_s, hpad, carry):
    tt, width = xc_ref.shape[1:]
    bw = width // LRU_BLOCKS
    n_chunks = width // LANES
    seg_len = tt // SEGMENTS
    pitch = seg_len + SEGMENTS
    sub = lax.broadcasted_iota(jnp.int32, (SEGMENTS, width), 0)

    xf = xc_ref[0].astype(F32)
    for c in range(n_chunks):
        for k in range(SEGMENTS):
            xpad[c, k * pitch:k * pitch + seg_len, :] = xf[k * seg_len:(k + 1) * seg_len, c * LANES:(c + 1) * LANES]
    xs = jnp.concatenate(
        [jnp.concatenate([xpad[c, pl.ds(j, SEGMENTS, stride=pitch), :] for j in range(seg_len)], axis=0)
         for c in range(n_chunks)], axis=1)

    halo_prev = jnp.where(tile == 0, 0.0, xp_ref[0, BF16_ROWS - CONV_LEFT:BF16_ROWS, :].astype(F32))
    halo_next = jnp.where(tile == n_tiles - 1, 0.0, xn_ref[0, 0:1, :].astype(F32))
    before1 = jnp.where(sub == 0, halo_prev[1:2], pltpu.roll(xs[tt - SEGMENTS:tt], 1, 0))
    before2 = jnp.where(sub == 0, halo_prev[0:1], pltpu.roll(xs[tt - 2 * SEGMENTS:tt - SEGMENTS], 1, 0))
    after1 = jnp.where(sub == SEGMENTS - 1, halo_next, pltpu.roll(xs[0:SEGMENTS], SEGMENTS - 1, 0))
    xm2 = jnp.concatenate([before2, before1, xs[:tt - 2 * SEGMENTS]], axis=0)
    xm1 = jnp.concatenate([before1, xs[:tt - SEGMENTS]], axis=0)
    xp1 = jnp.concatenate([xs[SEGMENTS:], after1], axis=0)
    xc = cb_ref[...] + xm2 * cw_ref[0:1] + xm1 * cw_ref[1:2] + xs * cw_ref[2:3] + xp1 * cw_ref[3:4]
    xcb = xc.astype(BF16)

    lam = lam_ref[dirn:dirn + 1]
    neg_lam = -lam
    softplus = jnp.maximum(neg_lam, 0.0) + jnp.log1p(jnp.exp(-jnp.abs(neg_lam)))
    coef = -RG_C * softplus

    sub_blk = lax.broadcasted_iota(jnp.int32, (SEGMENTS, bw), 0)
    if rev:
        start_rows = slice(tt - SEGMENTS, tt)
        first = (tile == n_tiles - 1) & (sub_blk == SEGMENTS - 1)
    else:
        start_rows = slice(0, SEGMENTS)
        first = (tile == 0) & (sub_blk == 0)

    for blk in range(LRU_BLOCKS):
        c0 = blk * bw
        xblk = xcb[:, c0:c0 + bw]
        pa = jnp.dot(xblk, wa_ref[dirn, blk], preferred_element_type=F32) + ba_ref[dirn:dirn + 1, c0:c0 + bw] * -LOG2_E
        px = jnp.dot(xblk, wx_ref[dirn, blk], preferred_element_type=F32) + bx_ref[dirn:dirn + 1, c0:c0 + bw] * -LOG2_E
        log_a = coef[:, c0:c0 + bw] / (1.0 + jnp.exp2(pa))
        a = jnp.exp(log_a)
        u = jnp.tanh(log_a) * (-1.0 - a * a)
        mult = jnp.where(u > 0.0, u * lax.rsqrt(u), 0.0)
        gated = xc[:, c0:c0 + bw] / (1.0 + jnp.exp2(px))
        b = mult * gated
        a_s[:, c0:c0 + bw] = a
        b_s[:, c0:c0 + bw] = b
        b_s[start_rows, c0:c0 + bw] = jnp.where(first, gated[start_rows], b[start_rows])

    steps = range(seg_len - 1, -1, -1) if rev else range(seg_len)
    segs = range(SEGMENTS - 1, -1, -1) if rev else range(SEGMENTS)
    sub1 = lax.broadcasted_iota(jnp.int32, (SEGMENTS, LANES), 0)
    for c in range(n_chunks):
        cs = slice(c * LANES, (c + 1) * LANES)
        prod = jnp.ones((SEGMENTS, LANES), F32)
        h = jnp.zeros((SEGMENTS, LANES), F32)
        for j in steps:
            aj = a_s[j * SEGMENTS:(j + 1) * SEGMENTS, cs]
            h = aj * h + b_s[j * SEGMENTS:(j + 1) * SEGMENTS, cs]
            prod = aj * prod
        state = carry[dirn:dirn + 1, cs]
        h0 = jnp.zeros((SEGMENTS, LANES), F32)
        for k in segs:
            h0 = jnp.where(sub1 == k, state, h0)
            state = prod[k:k + 1] * state + h[k:k + 1]
        carry[dirn:dirn + 1, cs] = state
        h = h0
        for j in steps:
            h = a_s[j * SEGMENTS:(j + 1) * SEGMENTS, cs] * h + b_s[j * SEGMENTS:(j + 1) * SEGMENTS, cs]
            hpad[c, pl.ds(j, SEGMENTS, stride=pitch), :] = h
        for k in range(SEGMENTS):
            h_ref[0, k * seg_len:(k + 1) * seg_len, cs] = hpad[c, k * pitch:k * pitch + seg_len, :].astype(h_ref.dtype)


def _lru_body(xfp_ref, xfc_ref, xfn_ref, xbp_ref, xbc_ref, xbn_ref,
              cw_ref, cb_ref, wa_ref, ba_ref, wx_ref, bx_ref, lam_ref,
              hf_ref, hb_ref, xpad_f, a_f, b_f, hpad_f, xpad_b, a_b, b_b, hpad_b, carry):
    i = pl.program_id(1)
    n_tiles = pl.num_programs(1)

    @pl.when(i == 0)
    def _():
        carry[...] = jnp.zeros_like(carry)

    params = (cw_ref, cb_ref, wa_ref, ba_ref, wx_ref, bx_ref, lam_ref)
    _lru_direction(xfp_ref, xfc_ref, xfn_ref, hf_ref, i, n_tiles, False, 0, *params,
                   xpad_f, a_f, b_f, hpad_f, carry)
    _lru_direction(xbp_ref, xbc_ref, xbn_ref, hb_ref, n_tiles - 1 - i, n_tiles, True, 1, *params,
                   xpad_b, a_b, b_b, hpad_b, carry)


def _lru(nat, conv_w, conv_b, w_a, b_a, w_x, b_x, lam, layer):
    b, s, _ = nat.shape
    width = conv_w.shape[-1]
    tt = LRU_ROWS
    n_tiles = s // tt
    halo_per_tile = tt // BF16_ROWS
    last_halo = s // BF16_ROWS - 1

    def tile_of(i, rev):
        return n_tiles - 1 - i if rev else i

    def specs(rev):
        return [
            pl.BlockSpec((1, BF16_ROWS, width),
                         lambda bi, i: (bi, jnp.maximum(tile_of(i, rev) * halo_per_tile - 1, 0), XLRU_COL)),
            pl.BlockSpec((1, tt, width), lambda bi, i: (bi, tile_of(i, rev), XLRU_COL)),
            pl.BlockSpec((1, BF16_ROWS, width),
                         lambda bi, i: (bi, jnp.minimum((tile_of(i, rev) + 1) * halo_per_tile, last_halo), XLRU_COL)),
        ]

    consts = (conv_w, conv_b, w_a, b_a, w_x, b_x, lam)
    padded_rows = tt + SEGMENTS * SEGMENTS
    per_direction = [pltpu.VMEM((width // LANES, padded_rows, LANES), F32),
                     pltpu.VMEM((tt, width), F32),
                     pltpu.VMEM((tt, width), F32),
                     pltpu.VMEM((width // LANES, padded_rows, LANES), F32)]
    return pl.pallas_call(
        _lru_body,
        grid=(b, n_tiles),
        in_specs=specs(False) + specs(True) + [_layer(a, layer) for a in consts],
        out_specs=[pl.BlockSpec((1, tt, width), lambda bi, i: (bi, i, 0)),
                   pl.BlockSpec((1, tt, width), lambda bi, i: (bi, n_tiles - 1 - i, 0))],
        out_shape=[jax.ShapeDtypeStruct((b, s, width), BF16)] * 2,
        scratch_shapes=per_direction + per_direction + [pltpu.VMEM((F32_ROWS, width), F32)],
        compiler_params=pltpu.CompilerParams(dimension_semantics=("parallel", "arbitrary")),
        name="lru",
    )(nat, nat, nat, nat, nat, nat, *consts)


def _token_order_rows(ref, r0, n_rows, slab):
    d, width = ref.shape[1], ref.shape[3]
    per = n_rows // d
    for r in range(d):
        piece = ref[0, r, r0 // d:r0 // d + per, :].astype(F32)
        for c in range(width // LANES):
            slab[c, pl.ds(r, per, stride=d), :] = piece[:, c * LANES:(c + 1) * LANES]
    return [slab[c] for c in range(width // LANES)]


def _merge_body(o0_ref, o1_ref, o2_ref, l0_ref, l1_ref, l2_ref, hf_ref, hb_ref, x_ref,
                ng_ref, wga_ref, wgo_ref, bg_ref, woa_ref, wol_ref, wout_ref, fg_ref, out_ref,
                o_slabs, l_slabs, *, final):
    tm, dm = x_ref.shape
    for blk, r0 in enumerate(range(0, tm, MERGE_SUB_ROWS)):
        rows = slice(r0, r0 + MERGE_SUB_ROWS)
        xn = _rms_norm(x_ref[rows], ng_ref[...]).astype(BF16)

        gatt = jnp.dot(xn, wga_ref[...], preferred_element_type=F32)
        glru, logit_att, logit_lru = (
            jnp.dot(xn, wgo_ref[:, c0:c0 + dm], preferred_element_type=F32) for c0 in (0, dm, 2 * dm))

        o_dilated = [_token_order_rows(ref, r0, MERGE_SUB_ROWS, o_slabs.at[blk, g])
                     for g, ref in enumerate((o1_ref, o2_ref))]
        l1, l2 = (_token_order_rows(ref, r0, MERGE_SUB_ROWS, l_slabs.at[blk, g])[0]
                  for g, ref in enumerate((l1_ref, l2_ref)))
        l0 = l0_ref[rows]
        mx = jnp.maximum(jnp.maximum(l0, l1), l2)
        e0, e1, e2 = jnp.exp(l0 - mx), jnp.exp(l1 - mx), jnp.exp(l2 - mx)
        inv = 1.0 / (e0 + e1 + e2)
        alphas = (e0 * inv, e1 * inv, e2 * inv)

        heads = []
        for h in range(HEADS_PER_GROUP):
            group_outs = (o0_ref[rows, h * HEAD_DIM:(h + 1) * HEAD_DIM].astype(F32), o_dilated[0][h], o_dilated[1][h])
            acc = jnp.zeros((MERGE_SUB_ROWS, HEAD_DIM), F32)
            for g in range(N_GROUPS):
                w = jnp.broadcast_to(alphas[g][:, h * LSE_REP:h * LSE_REP + 1], (MERGE_SUB_ROWS, HEAD_DIM))
                acc = acc + w * group_outs[g]
            heads.append(acc)
        mixed = jnp.concatenate(heads, axis=-1)

        y_att = (mixed * (gatt * _sigmoid(gatt))).astype(BF16)
        h_sum = hf_ref[rows].astype(F32) + hb_ref[rows].astype(F32)
        y_lru = (h_sum * (glru * _sigmoid(glru))).astype(BF16)

        p_att = jnp.dot(y_att, woa_ref[...], preferred_element_type=F32)
        p_lru = jnp.dot(y_lru, wol_ref[...], preferred_element_type=F32)
        gate_att = 1.0 / (1.0 + jnp.exp2(logit_att + bg_ref[:, 0:dm] * -LOG2_E))
        gate_lru = 1.0 / (1.0 + jnp.exp2(logit_lru + bg_ref[:, dm:2 * dm] * -LOG2_E))
        merged = (gate_att * p_att + gate_lru * p_lru).astype(BF16)
        y = x_ref[rows] + jnp.dot(merged, wout_ref[...], preferred_element_type=F32)
        if final:
            y = _rms_norm(y, fg_ref[...])
        out_ref[rows] = y


def _merge(x2d, outs, lses, h_f, h_b, norm_g, w_in, b_gate, w_o_att, w_o_lru, w_out, final_g, layer, final):
    t, dm = x2d.shape
    tm = MERGE_ROWS
    tiles_per_b = t // outs[1].shape[0] // tm
    att_gate_block = _col_block(N_GROUPS * QKV_WIDTH, GROUP_WIDTH)
    others_block = _col_block(N_GROUPS * QKV_WIDTH + GROUP_WIDTH + dm, 3 * dm)

    def rows(width):
        return pl.BlockSpec((tm, width), lambda i: (i, 0))

    def dilated_rows(a):
        d, width = a.shape[1], a.shape[3]
        return pl.BlockSpec((1, d, tm // d, width), lambda i: (i // tiles_per_b, 0, i % tiles_per_b, 0))

    n_blocks = tm // MERGE_SUB_ROWS
    return pl.pallas_call(
        functools.partial(_merge_body, final=final),
        grid=(t // tm,),
        in_specs=[rows(GROUP_WIDTH), dilated_rows(outs[1]), dilated_rows(outs[2]),
                  rows(LANES), dilated_rows(lses[1]), dilated_rows(lses[2])] + [rows(dm)] * 3
                 + [_layer(norm_g, layer), _layer(w_in, layer, cols=GROUP_WIDTH, col_block=att_gate_block),
                    _layer(w_in, layer, cols=3 * dm, col_block=others_block), _layer(b_gate, layer),
                    _layer(w_o_att, layer), _layer(w_o_lru, layer), _layer(w_out, layer),
                    pl.BlockSpec(final_g.shape, lambda i: (0, 0), pipeline_mode=pl.Buffered(1))],
        out_specs=rows(dm),
        out_shape=jax.ShapeDtypeStruct((t, dm), F32),
        scratch_shapes=[pltpu.VMEM((n_blocks, N_GROUPS - 1, GROUP_WIDTH // LANES, MERGE_SUB_ROWS, LANES), F32),
                        pltpu.VMEM((n_blocks, N_GROUPS - 1, 1, MERGE_SUB_ROWS, LANES), F32)],
        compiler_params=pltpu.CompilerParams(dimension_semantics=("parallel",)),
        name="merge_final" if final else "merge",
    )(*outs, *lses, h_f, h_b, x2d, norm_g, w_in, w_in, b_gate, w_o_att, w_o_lru, w_out, final_g)


def kernel(x, norm_g, w_in, b_gate, conv_w, conv_b, rg_w_a, rg_b_a, rg_w_x, rg_b_x, rg_lam,
           w_o_att, w_o_lru, w_out, final_g):
    b, s, dm = x.shape
    depth = w_in.shape[0]
    t = b * s
    w_o_att, w_o_lru, w_out = (w.astype(BF16) for w in (w_o_att, w_o_lru, w_out))
    n_in = w_in.shape[-1]
    col = jnp.arange(n_in)
    col_scale = jnp.where(col < N_GROUPS * GROUP_WIDTH, HEAD_DIM ** -0.5 * LOG2_E,
                          jnp.where(col >= n_in - 2 * dm, -LOG2_E, 1.0)).astype(F32)
    w_in = (w_in * col_scale).astype(BF16)
    rg_w_a, rg_w_x = ((w * -LOG2_E).astype(BF16) for w in (rg_w_a, rg_w_x))
    norm_g, b_gate, conv_b = (v.reshape(depth, 1, -1) for v in (norm_g, b_gate, conv_b))
    final_g = final_g.reshape(1, dm)

    x2d = x.reshape(t, dm)
    for layer in range(depth):
        nat2d, qkv4, qkv16 = _norm_proj(x2d, norm_g, w_in, layer, b)
        nat = nat2d.reshape(b, s, -1)
        outs, lses = [], []
        for group, qkv in enumerate((nat, qkv4, qkv16)):
            o, lse = _attention_group(qkv, group, ATT_PATTERNS[group][1])
            outs.append(o.reshape(t, GROUP_WIDTH) if group == 0 else o)
            lses.append(lse.reshape(t, LANES) if group == 0 else lse)
        h_f, h_b = _lru(nat, conv_w, conv_b, rg_w_a, rg_b_a, rg_w_x, rg_b_x, rg_lam, layer)
        x2d = _merge(x2d, outs, lses, h_f.reshape(t, dm), h_b.reshape(t, dm), norm_g, w_in, b_gate,
                     w_o_att, w_o_lru, w_out, final_g, layer, final=(layer == depth - 1))
    return x2d.reshape(b, s, dm)
```
